```python
import jax, jax.numpy as jnp
from jax import lax
import numpy as np

D_MODEL = 1024
BATCH = 16
SEQ = 2048
DEPTH = 4

GRID_W = 64
CTX_LEN = 256
EPS = 1e-6

HG_HEADS = 8
HG_DK = 128
HG_DV = 128
HG_WIDTH = HG_HEADS * HG_DK
HG_CHUNK = 16

ATT_HEADS = 8
ATT_KV_HEADS = 2
ATT_GROUP = ATT_HEADS // ATT_KV_HEADS
ATT_HD = 128
ATT_Q_BLOCK = 128
ATT_SCALE = ATT_HD ** -0.5
ROPE_THETA = 10000.0
ROPE_AXIS_DIM = ATT_HD // 2

LRU_WIDTH = 1024
LRU_BLOCKS = 4
LRU_BW = LRU_WIDTH // LRU_BLOCKS
LRU_CONV = 4
LRU_C = 8.0

N_BRANCH = 3
FFN_HIDDEN = ((8 * D_MODEL + 3 * 256 - 1) // (3 * 256)) * 256
N_MOD = 6

IN_SIZES = (HG_WIDTH, HG_WIDTH, HG_WIDTH, HG_WIDTH, HG_WIDTH,
            ATT_HEADS * ATT_HD, ATT_KV_HEADS * ATT_HD, ATT_KV_HEADS * ATT_HD,
            LRU_WIDTH, LRU_WIDTH,
            N_BRANCH * D_MODEL)
IN_TOTAL = sum(IN_SIZES)

kernel_name = "hybrid_hgrn2_gqa_rglru_diffusion_trunk"


def rms_norm(x, g):
    xf = x.astype(jnp.float32)
    y = xf * lax.rsqrt(jnp.mean(xf * xf, axis=-1, keepdims=True) + EPS)
    return (y * g.astype(jnp.float32)).astype(x.dtype)


def split_in(z):
    idx = [int(v) for v in np.cumsum(IN_SIZES)[:-1]]
    return jnp.split(z, idx, axis=-1)


def axial_rope_tables(n):
    rows = n // GRID_W
    r = jnp.repeat(jnp.arange(rows), GRID_W).astype(jnp.float32)
    col = jnp.tile(jnp.arange(GRID_W), rows).astype(jnp.float32)
    inv = jnp.power(ROPE_THETA, -jnp.arange(0, ROPE_AXIS_DIM, 2, dtype=jnp.float32) / ROPE_AXIS_DIM)
    ar = r[:, None] * inv
    ac = col[:, None] * inv
    return jnp.cos(ar), jnp.sin(ar), jnp.cos(ac), jnp.sin(ac)


def _rot_half(x, cos, sin):
    x1, x2 = jnp.split(x, 2, axis=-1)
    cos = cos[:, None, :]
    sin = sin[:, None, :]
    return jnp.concatenate([x1 * cos - x2 * sin, x2 * cos + x1 * sin], axis=-1)


def apply_axial_rope(x, tabs):
    cr, sr, cc, sc = tabs
    xf = x.astype(jnp.float32)
    xr, xcol = jnp.split(xf, 2, axis=-1)
    return jnp.concatenate([_rot_half(xr, cr, sr), _rot_half(xcol, cc, sc)], axis=-1).astype(x.dtype)


def gla_chunk_scan(q, k, v, logf, s0):
    B, n, H, K = q.shape
    V = v.shape[-1]
    C = HG_CHUNK
    nc = n // C

    def to_chunks(a):
        return a.reshape(B, nc, C, H, a.shape[-1]).transpose(1, 0, 3, 2, 4)

    mask = jnp.tril(jnp.ones((C, C), dtype=bool))[:, :, None]

    def step(S, inp):
        qc, kc, vc, gc = inp
        b = jnp.cumsum(gc, axis=2)
        o_inter = jnp.einsum('bhtk,bhkv->bhtv', qc * jnp.exp(b), S)
        diff = b[:, :, :, None, :] - b[:, :, None, :, :]
        dec = jnp.exp(jnp.where(mask, diff, -jnp.inf))
        A = jnp.einsum('bhtk,bhsk,bhtsk->bhts', qc, kc, dec)
        o_intra = jnp.einsum('bhts,bhsv->bhtv', A, vc)
        b_last = b[:, :, -1:, :]
        S_new = jnp.exp(b_last[:, :, 0, :])[..., None] * S + jnp.einsum('bhsk,bhsv->bhkv', kc * jnp.exp(b_last - b), vc)
        return S_new, o_inter + o_intra

    s_fin, o = lax.scan(step, s0, (to_chunks(q), to_chunks(k), to_chunks(v), to_chunks(logf)))
    o = o.transpose(1, 0, 3, 2, 4).reshape(B, n, H, V)
    return o, s_fin


def _hgrn2_prep(z, lb_f, lb_b):
    q, i, f_fwd, f_bwd, g = z
    B, n, _ = q.shape
    sh = (B, n, HG_HEADS, HG_DK)
    qh = (jax.nn.silu(q.astype(jnp.float32)) * (HG_DK ** -0.5)).reshape(sh)
    vh = i.astype(jnp.float32).reshape(B, n, HG_HEADS, HG_DV)
    dirs = []
    for f, lb in ((f_fwd, lb_f), (f_bwd, lb_b)):
        fg = lb + (1.0 - lb) * jax.nn.sigmoid(f.astype(jnp.float32))
        dirs.append(((1.0 - fg).reshape(sh), jnp.log(fg).reshape(sh)))
    return qh, vh, dirs, g


def _flip(a):
    return jnp.flip(a, axis=1)


def hgrn2_branch(zc, zl, lb_f, lb_b, gain, need_ctx):
    qc, vc, dc, gc = _hgrn2_prep(zc, lb_f, lb_b)
    ql, vl, dl, gl = _hgrn2_prep(zl, lb_f, lb_b)
    B = ql.shape[0]
    s0 = jnp.zeros((B, HG_HEADS, HG_DK, HG_DV), jnp.float32)
    o_cf, s_cf = gla_chunk_scan(qc, dc[0][0], vc, dc[0][1], s0)
    o_lf, _ = gla_chunk_scan(ql, dl[0][0], vl, dl[0][1], s_cf)
    o_cb, s_cb = gla_chunk_scan(_flip(qc), _flip(dc[1][0]), _flip(vc), _flip(dc[1][1]), s0)
    o_lb, _ = gla_chunk_scan(_flip(ql), _flip(dl[1][0]), _flip(vl), _flip(dl[1][1]), s_cb)

    def readout(o, g):
        Bn, n = o.shape[:2]
        gh = jax.nn.silu(g.astype(jnp.float32)).reshape(Bn, n, HG_HEADS, HG_DV)
        return (rms_norm(o, gain) * gh).reshape(Bn, n, HG_WIDTH).astype(g.dtype)

    y_l = readout(o_lf + _flip(o_lb), gl)
    y_c = readout(o_cf + _flip(o_cb), gc) if need_ctx else None
    return y_c, y_l


def _attend(q, k, v):
    B, m = q.shape[:2]
    qg = q.reshape(B, m, ATT_KV_HEADS, ATT_GROUP, ATT_HD)
    s = jnp.einsum('bqkgd,bskd->bkgqs', qg, k, preferred_element_type=jnp.float32) * ATT_SCALE
    p = jax.nn.softmax(s, axis=-1).astype(v.dtype)
    o = jnp.einsum('bkgqs,bskd->bqkgd', p, v)
    return o.reshape(B, m, ATT_HEADS * ATT_HD)


def gqa_branch(zc, zl, q_gain, k_gain, tabs, need_ctx):
    q_c, k_c, v_c = zc
    q_l, k_l, v_l = zl

    def heads(a, h):
        return a.reshape(a.shape[0], a.shape[1], h, ATT_HD)

    B, n = q_l.shape[:2]
    ql = apply_axial_rope(rms_norm(heads(q_l, ATT_HEADS), q_gain), tabs)
    kl = apply_axial_rope(rms_norm(heads(k_l, ATT_KV_HEADS), k_gain), tabs)
    vl = heads(v_l, ATT_KV_HEADS)
    kc = rms_norm(heads(k_c, ATT_KV_HEADS), k_gain)
    vc = heads(v_c, ATT_KV_HEADS)
    k_all = jnp.concatenate([kc, kl], axis=1)
    v_all = jnp.concatenate([vc, vl], axis=1)
    nb = n // ATT_Q_BLOCK
    q_blocks = ql.reshape(B, nb, ATT_Q_BLOCK, ATT_HEADS, ATT_HD).swapaxes(0, 1)
    o_l = lax.map(lambda qb: _attend(qb, k_all, v_all), q_blocks)
    y_l = o_l.swapaxes(0, 1).reshape(B, n, ATT_HEADS * ATT_HD)
    y_c = _attend(rms_norm(heads(q_c, ATT_HEADS), q_gain), kc, vc) if need_ctx else None
    return y_c, y_l


def conv_centred(x, w, b):
    pad = (LRU_CONV // 2, LRU_CONV - 1 - LRU_CONV // 2)
    y = lax.conv_general_dilated(x, w[:, None, :].astype(x.dtype), window_strides=(1,), padding=[pad],
                                 dimension_numbers=('NWC', 'WIO', 'NWC'), feature_group_count=x.shape[-1])
    return y + b.astype(x.dtype)


def lru_gates(x, w_a, b_a, w_x, b_x, lam):
    B, n, W = x.shape
    xf = x.astype(jnp.float32)
    xb = xf.reshape(B, n, LRU_BLOCKS, LRU_BW)
    r = jax.nn.sigmoid(jnp.einsum('bnhi,hij->bnhj', xb, w_a.astype(jnp.float32)).reshape(B, n, W) + b_a.astype(jnp.float32))
    i = jax.nn.sigmoid(jnp.einsum('bnhi,hij->bnhj', xb, w_x.astype(jnp.float32)).reshape(B, n, W) + b_x.astype(jnp.float32))
    log_a = -LRU_C * r * jax.nn.softplus(-lam.astype(jnp.float32))
    a = jnp.exp(log_a)
    mult = jnp.sqrt(-jnp.expm1(2.0 * log_a))
    return a, mult * (i * xf)


def linear_scan(a, u, h0, reverse):
    if h0 is not None:
        edge = -1 if reverse else 0
        u = u.at[:, edge].add(a[:, edge] * h0)

    def comb(e, l):
        return e[0] * l[0], l[0] * e[1] + l[1]

    _, h = lax.associative_scan(comb, (a, u), reverse=reverse, axis=1)
    return h


def rglru_branch(zc, zl, conv_w, conv_b, w_a, b_a, w_x, b_x, lam, need_ctx):
    xc, gc = zc
    xl, gl = zl
    xc = conv_centred(xc, conv_w, conv_b)
    xl = conv_centred(xl, conv_w, conv_b)
    h_c_sum = 0.0
    h_l_sum = 0.0
    for d, rev in ((0, False), (1, True)):
        a_c, u_c = lru_gates(xc, w_a[d], b_a[d], w_x[d], b_x[d], lam[d])
        h_c = linear_scan(a_c, u_c, None, rev)
        h0 = h_c[:, 0] if rev else h_c[:, -1]
        a_l, u_l = lru_gates(xl, w_a[d], b_a[d], w_x[d], b_x[d], lam[d])
        h_l = linear_scan(a_l, u_l, h0, rev)
        h_c_sum = h_c_sum + h_c
        h_l_sum = h_l_sum + h_l
    y_l = (h_l_sum * jax.nn.gelu(gl.astype(jnp.float32), approximate=True)).astype(gl.dtype)
    y_c = (h_c_sum * jax.nn.gelu(gc.astype(jnp.float32), approximate=True)).astype(gc.dtype) if need_ctx else None
    return y_c, y_l


def merge_branches(gate_pre, outs, w_br, w_o):
    gates = jnp.split(jax.nn.sigmoid(gate_pre), N_BRANCH, axis=-1)
    y = gates[0] * (outs[0] @ w_br[0])
    for b in range(1, N_BRANCH):
        y = y + gates[b] * (outs[b] @ w_br[b])
    return y @ w_o


def swiglu(h, w_in, w_out):
    gate, up = jnp.split(h @ w_in, 2, axis=-1)
    return (jax.nn.silu(gate) * up) @ w_out


def modulate(h, shift, scale):
    return h * (1.0 + scale) + shift


def setup_inputs(seed: int = 0) -> dict:
    key = jax.random.key(seed)
    ks = list(jax.random.split(key, 32))
    f32 = jnp.float32
    D = D_MODEL

    def nrm(k, shape, scale):
        return jax.random.normal(k, shape, f32) * scale

    u = jax.random.uniform(ks[20], (DEPTH, 2, LRU_WIDTH), f32, minval=0.81, maxval=0.998)
    sp = -0.5 * jnp.log(u)
    lru_lambda = -jnp.log(jnp.expm1(sp))
    return {
        "x": nrm(ks[0], (BATCH, SEQ, D), 1.0),
        "c": nrm(ks[1], (BATCH, D), 1.0),
        "ctx": nrm(ks[2], (BATCH, CTX_LEN, D), 1.0),
        "c_ctx": nrm(ks[3], (D,), 1.0),
        "w_mod": nrm(ks[4], (DEPTH, D, N_MOD * D), 0.5 * D ** -0.5),
        "b_mod": nrm(ks[5], (DEPTH, N_MOD * D), 0.02),
        "norm1": 1.0 + nrm(ks[6], (DEPTH, D), 0.02),
        "w_in": nrm(ks[7], (DEPTH, D, IN_TOTAL), D ** -0.5),
        "hg_lb_logits": nrm(ks[8], (2, DEPTH, HG_WIDTH), 0.5),
        "hg_out_norm": 1.0 + nrm(ks[9], (DEPTH, HG_DV), 0.02),
        "att_q_norm": 1.0 + nrm(ks[10], (DEPTH, ATT_HD), 0.02),
        "att_k_norm": 1.0 + nrm(ks[11], (DEPTH, ATT_HD), 0.02),
        "lru_conv_w": nrm(ks[12], (DEPTH, LRU_CONV, LRU_WIDTH), LRU_CONV ** -0.5),
        "lru_conv_b": nrm(ks[13], (DEPTH, LRU_WIDTH), 0.02),
        "lru_w_a": nrm(ks[14], (DEPTH, 2, LRU_BLOCKS, LRU_BW, LRU_BW), LRU_BW ** -0.5),
        "lru_b_a": nrm(ks[15], (DEPTH, 2, LRU_WIDTH), 0.02),
        "lru_w_x": nrm(ks[16], (DEPTH, 2, LRU_BLOCKS, LRU_BW, LRU_BW), LRU_BW ** -0.5),
        "lru_b_x": nrm(ks[17], (DEPTH, 2, LRU_WIDTH), 0.02),
        "lru_lambda": lru_lambda,
        "w_branch": nrm(ks[18], (DEPTH, N_BRANCH, HG_WIDTH, D), HG_WIDTH ** -0.5),
        "w_out": nrm(ks[19], (DEPTH, D, D), D ** -0.5),
        "norm2": 1.0 + nrm(ks[21], (DEPTH, D), 0.02),
        "w_ffn_in": nrm(ks[22], (DEPTH, D, 2 * FFN_HIDDEN), D ** -0.5),
        "w_ffn_out": nrm(ks[23], (DEPTH, FFN_HIDDEN, D), FFN_HIDDEN ** -0.5),
        "norm_final": 1.0 + nrm(ks[24], (D,), 0.02),
    }


def reference(x, c, ctx, c_ctx, w_mod, b_mod, norm1, w_in, hg_lb_logits, hg_out_norm, att_q_norm, att_k_norm,
              lru_conv_w, lru_conv_b, lru_w_a, lru_b_a, lru_w_x, lru_b_x, lru_lambda, w_branch, w_out, norm2,
              w_ffn_in, w_ffn_out, norm_final):
    lb = jnp.cumsum(jax.nn.softmax(hg_lb_logits.astype(jnp.float32), axis=1), axis=1)
    lb = lb - lb[:, :1]
    tabs = axial_rope_tables(x.shape[1])
    c_act = jax.nn.silu(c)
    cc_act = jax.nn.silu(c_ctx)[None]
    xl, xc = x, ctx
    for l in range(DEPTH):
        need_ctx = l < DEPTH - 1
        ml = jnp.split((c_act @ w_mod[l] + b_mod[l])[:, None, :], N_MOD, axis=-1)
        mc = jnp.split((cc_act @ w_mod[l] + b_mod[l])[:, None, :], N_MOD, axis=-1)
        hl = modulate(rms_norm(xl, norm1[l]), ml[0], ml[1])
        hc = modulate(rms_norm(xc, norm1[l]), mc[0], mc[1])
        zl = split_in(hl @ w_in[l])
        zc = split_in(hc @ w_in[l])
        a_c, a_l = hgrn2_branch(zc[0:5], zl[0:5], lb[0, l], lb[1, l], hg_out_norm[l], need_ctx)
        b_c, b_l = gqa_branch(zc[5:8], zl[5:8], att_q_norm[l], att_k_norm[l], tabs, need_ctx)
        r_c, r_l = rglru_branch(zc[8:10], zl[8:10], lru_conv_w[l], lru_conv_b[l], lru_w_a[l], lru_b_a[l],
                                lru_w_x[l], lru_b_x[l], lru_lambda[l], need_ctx)
        xl = xl + ml[2] * merge_branches(zl[10], (a_l, b_l, r_l), w_branch[l], w_out[l])
        xl = xl + ml[5] * swiglu(modulate(rms_norm(xl, norm2[l]), ml[3], ml[4]), w_ffn_in[l], w_ffn_out[l])
        if need_ctx:
            xc = xc + mc[2] * merge_branches(zc[10], (a_c, b_c, r_c), w_branch[l], w_out[l])
            xc = xc + mc[5] * swiglu(modulate(rms_norm(xc, norm2[l]), mc[3], mc[4]), w_ffn_in[l], w_ffn_out[l])
    return rms_norm(xl, norm_final)
```

```python
import functools

import numpy as np
import jax
import jax.numpy as jnp
from jax import lax
from jax.experimental import pallas as pl
from jax.experimental.pallas import tpu as pltpu

F32 = jnp.float32
BF16 = jnp.bfloat16

EPS = 1e-6
GRID_W = 64
HG_HEADS = 8
HG_DK = 128
ATT_HEADS = 8
ATT_KV_HEADS = 2
ATT_GROUP = ATT_HEADS // ATT_KV_HEADS
ATT_HD = 128
ROPE_THETA = 10000.0
LRU_BLOCKS = 4
LRU_CONV = 4
LRU_C = 8.0
N_BRANCH = 3
N_MOD = 6

V7X_VMEM_BYTES = 64 * 1024 * 1024
V7X_LANES = 128
V7X_SUBLANES = 8
VMEM_LIMIT = (V7X_VMEM_BYTES * 3) // 4

TM = 256
HG_CHUNK = 128
HG_SUB = V7X_SUBLANES
SCAN_ROWS = V7X_SUBLANES


def _silu(x):
    return x * jax.nn.sigmoid(x)


def _gelu_tanh(x):
    return 0.5 * x * (1.0 + jnp.tanh(np.sqrt(2.0 / np.pi) * (x + 0.044715 * (x * x * x))))


def _dot(a, b):
    return jnp.dot(a, b, preferred_element_type=F32)


def _dot_nt(a, b):
    return lax.dot_general(a, b, (((1,), (1,)), ((), ())), preferred_element_type=F32)


def _dot_tn(a, b):
    return lax.dot_general(a, b, (((0,), (0,)), ((), ())), preferred_element_type=F32)


def _params(*sem):
    return pltpu.CompilerParams(dimension_semantics=sem, vmem_limit_bytes=VMEM_LIMIT)


def _mod_kernel(c_ref, w_ref, b_ref, o_ref):
    ca = _silu(c_ref[...]).astype(BF16)
    o_ref[...] = _dot(ca, w_ref[...].astype(BF16)) + b_ref[...]


def _mod_vectors(c_all, w_mod, b_mod):
    depth, d, n = w_mod.shape
    rows = c_all.shape[0]
    tn = n // 4
    return pl.pallas_call(
        _mod_kernel,
        grid=(depth, n // tn),
        in_specs=[pl.BlockSpec((rows, d), lambda l, j: (0, 0)),
                  pl.BlockSpec((None, d, tn), lambda l, j: (l, 0, j)),
                  pl.BlockSpec((None, 1, tn), lambda l, j: (l, 0, j))],
        out_specs=pl.BlockSpec((None, rows, tn), lambda l, j: (l, 0, j)),
        out_shape=jax.ShapeDtypeStruct((depth, rows, n), F32),
        compiler_params=_params("parallel", "parallel"),
        name="mod_vectors",
    )(c_all, w_mod, b_mod.reshape(depth, 1, n))


def _lb_kernel(x_ref, o_ref):
    depth = x_ref.shape[1]
    rows = [x_ref[:, l, :] for l in range(depth)]
    m = functools.reduce(jnp.maximum, rows)
    e = [jnp.exp(r - m) for r in rows]
    inv = 1.0 / functools.reduce(lambda a, b: a + b, e)
    acc = jnp.zeros_like(rows[0])
    o_ref[:, 0, :] = acc
    for l in range(1, depth):
        acc = acc + e[l] * inv
        o_ref[:, l, :] = acc


def _lower_bounds(logits):
    return pl.pallas_call(
        _lb_kernel,
        out_shape=jax.ShapeDtypeStruct(logits.shape, F32),
        name="hg_lower_bounds",
    )(logits.astype(F32))


def _norm_mod(x, g, shift, scale):
    ms = jnp.mean(x * x, axis=-1, keepdims=True)
    y = x * lax.rsqrt(ms + EPS) * g
    return y * (1.0 + scale) + shift


def _inproj_hg_kernel(x_ref, g_ref, sh_ref, sc_ref, w_ref, lbf_ref, lbb_ref,
                      qs_ref, v_ref, ff_ref, fb_ref, sg_ref):
    h = _norm_mod(x_ref[...], g_ref[...], sh_ref[...], sc_ref[...]).astype(BF16)
    w = qs_ref.shape[-1]
    z = _dot(h, w_ref[:, 0:w])
    qs_ref[...] = (_silu(z) * (HG_DK ** -0.5)).astype(BF16)
    v_ref[...] = _dot(h, w_ref[:, w:2 * w]).astype(BF16)
    for k, (lb_ref, f_ref) in enumerate(((lbf_ref, ff_ref), (lbb_ref, fb_ref))):
        lb = lb_ref[...]
        z = _dot(h, w_ref[:, (2 + k) * w:(3 + k) * w])
        f_ref[...] = lb + (1.0 - lb) * jax.nn.sigmoid(z)
    sg_ref[...] = _silu(_dot(h, w_ref[:, 4 * w:5 * w])).astype(BF16)


def _inproj_rest_kernel(x_ref, g_ref, sh_ref, sc_ref, w_ref, qg_ref, kg_ref, cos_ref, sin_ref,
                        q_ref, k_ref, v_ref, lx_ref, lg_ref, mg_ref):
    h = _norm_mod(x_ref[...], g_ref[...], sh_ref[...], sc_ref[...]).astype(BF16)
    cos = cos_ref[...]
    sin = sin_ref[...]
    lane = lax.broadcasted_iota(jnp.int32, cos.shape, 1)
    quarter = ATT_HD // 4
    first = (lane % (2 * quarter)) < quarter

    def norm_rope(z, gain, scale):
        ms = jnp.mean(z * z, axis=-1, keepdims=True)
        y = z * lax.rsqrt(ms + EPS) * gain
        swapped = jnp.where(first, pltpu.roll(y, ATT_HD - quarter, 1), pltpu.roll(y, quarter, 1))
        return ((y * cos + swapped * sin) * scale).astype(BF16)

    nq = q_ref.shape[-1]
    nkv = k_ref.shape[-1]
    nl = lx_ref.shape[-1]
    z = _dot(h, w_ref[:, 0:nq])
    for i in range(nq // ATT_HD):
        sl = slice(i * ATT_HD, (i + 1) * ATT_HD)
        q_ref[:, sl] = norm_rope(z[:, sl], qg_ref[...], ATT_HD ** -0.5)
    z = _dot(h, w_ref[:, nq:nq + nkv])
    for i in range(nkv // ATT_HD):
        sl = slice(i * ATT_HD, (i + 1) * ATT_HD)
        k_ref[:, sl] = norm_rope(z[:, sl], kg_ref[...], 1.0)
    off = nq + nkv
    v_ref[...] = _dot(h, w_ref[:, off:off + nkv]).astype(BF16)
    off += nkv
    lx_ref[...] = _dot(h, w_ref[:, off:off + nl])
    off += nl
    lg_ref[...] = _gelu_tanh(_dot(h, w_ref[:, off:off + nl])).astype(BF16)
    off += nl
    mg_ref[...] = jax.nn.sigmoid(_dot(h, w_ref[:, off:off + mg_ref.shape[-1]])).astype(BF16)


def _tile_spec(width, dtype_unused=None):
    return pl.BlockSpec((None, TM, width), lambda b, t: (b, t, 0))


def _vec_spec(width):
    return pl.BlockSpec((1, width), lambda b, t: (0, 0))


def _mod_spec(mods, which, n_ctx_tiles, toff=0):
    ctx_row = mods.shape[0] - 1
    d = mods.shape[-1]
    return pl.BlockSpec((None, None, 1, d),
                        lambda b, t: (jnp.where(t + toff < n_ctx_tiles, ctx_row, b), which, 0, 0))


def _inproj(x, mods, g1, w_hg, w_rest, lbf, lbb, qg, kg, cos_t, sin_t, n_ctx_tiles):
    bsz, t_all, d = x.shape
    grid = (bsz, t_all // TM)
    hw = w_hg.shape[1] // 5
    nq = ATT_HEADS * ATT_HD
    nkv = ATT_KV_HEADS * ATT_HD
    common = [_tile_spec(d), _vec_spec(d), _mod_spec(mods, 0, n_ctx_tiles),
              _mod_spec(mods, 1, n_ctx_tiles)]
    sds = lambda w, dt: jax.ShapeDtypeStruct((bsz, t_all, w), dt)
    hg = pl.pallas_call(
        _inproj_hg_kernel,
        grid=grid,
        in_specs=common + [pl.BlockSpec(w_hg.shape, lambda b, t: (0, 0)), _vec_spec(hw), _vec_spec(hw)],
        out_specs=[_tile_spec(hw)] * 5,
        out_shape=[sds(hw, BF16), sds(hw, BF16), sds(hw, F32), sds(hw, F32), sds(hw, BF16)],
        compiler_params=_params("parallel", "parallel"),
        name="inproj_hgrn",
    )(x, g1, mods, mods, w_hg, lbf, lbb)
    nl = hw
    ng = w_rest.shape[1] - nq - 2 * nkv - 2 * nl
    rest = pl.pallas_call(
        _inproj_rest_kernel,
        grid=grid,
        in_specs=common + [pl.BlockSpec(w_rest.shape, lambda b, t: (0, 0)),
                           _vec_spec(ATT_HD), _vec_spec(ATT_HD),
                           pl.BlockSpec((TM, ATT_HD), lambda b, t: (t, 0)),
                           pl.BlockSpec((TM, ATT_HD), lambda b, t: (t, 0))],
        out_specs=[_tile_spec(nq), _tile_spec(nkv), _tile_spec(nkv), _tile_spec(nl), _tile_spec(nl),
                   _tile_spec(ng)],
        out_shape=[sds(nq, BF16), sds(nkv, BF16), sds(nkv, BF16), sds(nl, F32), sds(nl, BF16),
                   sds(ng, BF16)],
        compiler_params=_params("parallel", "parallel"),
        name="inproj_rest",
    )(x, g1, mods, mods, w_rest, qg, kg, cos_t, sin_t)
    return hg, rest


def _split3(g):
    g1 = g.astype(BF16)
    r1 = g - g1.astype(F32)
    g2 = r1.astype(BF16)
    g3 = (r1 - g2.astype(F32)).astype(BF16)
    return jnp.concatenate([g1, g2, g3], axis=-1)


def _hg_chunk(q, v, f, st, reverse):
    c, kdim = q.shape
    g = jnp.log(f)
    kk = 1.0 - f
    row = lax.broadcasted_iota(jnp.int32, (c, c), 0)
    col = lax.broadcasted_iota(jnp.int32, (c, c), 1)
    tri = (col >= row) if reverse else (col <= row)
    tri = jnp.where(tri, 1.0, 0.0).astype(BF16)
    parts = _dot(tri, _split3(g))
    b = parts[:, 0:kdim] + parts[:, kdim:2 * kdim] + parts[:, 2 * kdim:3 * kdim]

    qe = (q * jnp.exp(b)).astype(BF16)
    o = _dot_nt(qe, st.astype(BF16))
    b_end = b[0:1, :] if reverse else b[c - 1:c, :]
    ke = (kk * jnp.exp(b_end - b)).astype(BF16)
    st_new = st * jnp.exp(b_end) + _dot_tn(v, ke)

    trow = lax.broadcasted_iota(jnp.int32, (c, kdim), 0)
    a_hi = jnp.zeros((c, c), F32)
    blk = 2 * HG_SUB
    while blk <= c:
        half = blk // 2
        ref_row = half - 1 if reverse else half
        b_ref = b.reshape(c // blk, blk, kdim)[:, ref_row:ref_row + 1, :]
        b_ref = jnp.broadcast_to(b_ref, (c // blk, blk, kdim)).reshape(c, kdim)
        e = jnp.exp(-jnp.abs(b - b_ref))
        in_right = (trow % blk) >= half
        is_q = jnp.logical_not(in_right) if reverse else in_right
        wq = jnp.where(is_q, q * e, 0.0).astype(BF16)
        wk = jnp.where(is_q, 0.0, kk * e).astype(BF16)
        a_lvl = _dot_nt(wq, wk)
        a_hi = a_hi + jnp.where((row // blk) == (col // blk), a_lvl, 0.0)
        blk *= 2
    o = o + _dot(a_hi.astype(BF16), v)

    nb = c // HG_SUB
    shp = (nb, HG_SUB, kdim)
    q3, k3, f3, v3 = (a.reshape(shp) for a in (q, kk, f, v.astype(F32)))
    sub = lax.broadcasted_iota(jnp.int32, (nb, HG_SUB, 1), 1)
    o3 = jnp.sum(q3 * k3, axis=-1, keepdims=True) * v3
    dec = jnp.ones(shp, F32)
    for d in range(1, HG_SUB):
        sh_prev = (HG_SUB - (d - 1)) % HG_SUB if reverse else (d - 1)
        sh = (HG_SUB - d) if reverse else d
        dec = dec * (pltpu.roll(f3, sh_prev, 1) if sh_prev else f3)
        a_d = jnp.sum(q3 * dec * pltpu.roll(k3, sh, 1), axis=-1, keepdims=True)
        valid = (sub + d < HG_SUB) if reverse else (sub >= d)
        o3 = o3 + jnp.where(valid, a_d, 0.0) * pltpu.roll(v3, sh, 1)
    return o + o3.reshape(c, kdim), st_new


def _hgrn_kernel(qs_ref, v_ref, ff_ref, fb_ref, sg_ref, gain_ref, y_ref, of_ref, ob_ref, *, n_ctx):
    t_all, kdim = qs_ref.shape
    c = HG_CHUNK
    n_chunks = t_all // c
    n_ctx_chunks = n_ctx // c

    def run(f_ref, o_ref, reverse):
        def body(i, st):
            if reverse:
                idx = jnp.where(i < n_ctx_chunks, n_ctx_chunks - 1 - i, n_chunks - 1 - (i - n_ctx_chunks))
            else:
                idx = i
            rows = pl.ds(pl.multiple_of(idx * c, c), c)
            o, st = _hg_chunk(qs_ref[rows, :].astype(F32), v_ref[rows, :], f_ref[rows, :], st, reverse)
            o_ref[rows, :] = o
            return st
        lax.fori_loop(0, n_chunks, body, jnp.zeros((kdim, kdim), F32))

    run(ff_ref, of_ref, False)
    run(fb_ref, ob_ref, True)

    def readout(i, carry):
        rows = pl.ds(pl.multiple_of(i * c, c), c)
        o = of_ref[rows, :] + ob_ref[rows, :]
        ms = jnp.mean(o * o, axis=-1, keepdims=True)
        y = o * lax.rsqrt(ms + EPS) * gain_ref[...]
        y_ref[rows, :] = (y * sg_ref[rows, :].astype(F32)).astype(BF16)
        return carry
    lax.fori_loop(0, n_chunks, readout, 0)


def _hgrn(qs, v, ff, fb, sg, gain, n_ctx):
    bsz, t_all, width = qs.shape
    kdim = HG_DK
    spec = pl.BlockSpec((None, t_all, kdim), lambda b, h: (b, 0, h))
    return pl.pallas_call(
        functools.partial(_hgrn_kernel, n_ctx=n_ctx),
        grid=(bsz, width // kdim),
        in_specs=[spec] * 5 + [pl.BlockSpec((1, kdim), lambda b, h: (0, 0))],
        out_specs=spec,
        out_shape=jax.ShapeDtypeStruct((bsz, t_all, width), BF16),
        scratch_shapes=[pltpu.VMEM((t_all, kdim), F32), pltpu.VMEM((t_all, kdim), F32)],
        compiler_params=_params("parallel", "parallel"),
        name="hgrn2_mixer",
    )(qs, v, ff, fb, sg, gain)


def _att_kernel(q_ref, k_ref, v_ref, o_ref, *, n_ctx):
    qi = pl.program_id(2)
    n_ctx_tiles = n_ctx // q_ref.shape[0]

    def run(nk):
        k = k_ref[0:nk, :]
        v = v_ref[0:nk, :]
        for g in range(ATT_GROUP):
            sl = slice(g * ATT_HD, (g + 1) * ATT_HD)
            s = _dot_nt(q_ref[:, sl], k)
            p = jnp.exp(s - jnp.max(s, axis=-1, keepdims=True))
            den = jnp.sum(p, axis=-1, keepdims=True)
            o_ref[:, sl] = (_dot(p.astype(BF16), v) / den).astype(BF16)

    @pl.when(qi < n_ctx_tiles)
    def _():
        run(n_ctx)

    @pl.when(qi >= n_ctx_tiles)
    def _():
        run(k_ref.shape[0])


def _attention(q, k, v, n_ctx):
    bsz, t_all, nq = q.shape
    gw = ATT_GROUP * ATT_HD
    return pl.pallas_call(
        functools.partial(_att_kernel, n_ctx=n_ctx),
        grid=(bsz, ATT_KV_HEADS, t_all // TM),
        in_specs=[pl.BlockSpec((None, TM, gw), lambda b, h, i: (b, i, h)),
                  pl.BlockSpec((None, t_all, ATT_HD), lambda b, h, i: (b, 0, h)),
                  pl.BlockSpec((None, t_all, ATT_HD), lambda b, h, i: (b, 0, h))],
        out_specs=pl.BlockSpec((None, TM, gw), lambda b, h, i: (b, i, h)),
        out_shape=jax.ShapeDtypeStruct((bsz, t_all, nq), BF16),
        compiler_params=_params("parallel", "parallel", "arbitrary"),
        name="gqa_mixer",
    )(q, k, v)


def _lru_kernel(x_ref, gl_ref, cw_ref, cb_ref, wa_ref, ba_ref, wx_ref, bx_ref, lam_ref, y_ref,
                xp_ref, af_ref, uf_ref, ab_ref, ub_ref, *, n_ctx):
    t_all, w = x_ref.shape
    pad = V7X_SUBLANES
    rc = TM
    n_rc = t_all // rc

    xp_ref[0:pad, :] = jnp.zeros((pad, w), F32)
    xp_ref[pad + t_all:pad + t_all + pad, :] = jnp.zeros((pad, w), F32)

    def copy(i, carry):
        rows = pl.multiple_of(i * rc, rc)
        xp_ref[pl.ds(rows + pad, rc), :] = x_ref[pl.ds(rows, rc), :]
        return carry
    lax.fori_loop(0, n_rc, copy, 0)

    a_refs = (af_ref, ab_ref)
    u_refs = (uf_ref, ub_ref)
    sp = [jax.nn.softplus(-lam_ref[d:d + 1, :]) for d in range(2)]

    def gates(i, carry):
        r0 = pl.multiple_of(i * rc, rc)
        win = xp_ref[pl.ds(r0, rc + 2 * pad), :]
        trow = r0 + lax.broadcasted_iota(jnp.int32, (rc, 1), 0)
        xc = jnp.zeros((rc, w), F32) + cb_ref[...]
        for j in range(LRU_CONV):
            o = j - LRU_CONV // 2
            shifted = pltpu.roll(win, (rc + 2 * pad - o) % (rc + 2 * pad), 0)[pad:pad + rc, :]
            same_seg = ((trow + o) >= n_ctx) == (trow >= n_ctx)
            xc = xc + jnp.where(same_seg, shifted, 0.0) * cw_ref[j:j + 1, :]
        xcb = xc.astype(BF16)
        for d in range(2):
            r = jax.nn.sigmoid(_dot(xcb, wa_ref[d]) + ba_ref[d:d + 1, :])
            ig = jax.nn.sigmoid(_dot(xcb, wx_ref[d]) + bx_ref[d:d + 1, :])
            log_a = (-LRU_C) * r * sp[d]
            a = jnp.exp(log_a)
            one_minus_a2 = jnp.tanh(-log_a) * (1.0 + a * a)
            a_refs[d][pl.ds(r0, rc), :] = a
            u_refs[d][pl.ds(r0, rc), :] = jnp.sqrt(one_minus_a2) * (ig * xc)
        return carry
    lax.fori_loop(0, n_rc, gates, 0)

    n_tiles = t_all // SCAN_ROWS
    n_ctx_tiles = n_ctx // SCAN_ROWS
    sub = lax.broadcasted_iota(jnp.int32, (SCAN_ROWS, 1), 0)

    def tile_scan(a, u, h_in, reverse):
        s = 1
        while s < SCAN_ROWS:
            sh = (SCAN_ROWS - s) if reverse else s
            keep = (sub + s < SCAN_ROWS) if reverse else (sub >= s)
            a_s = jnp.where(keep, pltpu.roll(a, sh, 0), 1.0)
            u_s = jnp.where(keep, pltpu.roll(u, sh, 0), 0.0)
            u = u + a * u_s
            a = a * a_s
            s *= 2
        h = u + a * h_in
        edge = h[0:1, :] if reverse else h[SCAN_ROWS - 1:SCAN_ROWS, :]
        return h, jnp.broadcast_to(edge, h.shape)

    def scan_body(i, carry):
        hf, hb = carry
        rows = pl.ds(pl.multiple_of(i * SCAN_ROWS, SCAN_ROWS), SCAN_ROWS)
        h, hf = tile_scan(af_ref[rows, :], uf_ref[rows, :], hf, False)
        uf_ref[rows, :] = h
        ib = jnp.where(i < n_ctx_tiles, n_ctx_tiles - 1 - i, n_tiles - 1 - (i - n_ctx_tiles))
        rows = pl.ds(pl.multiple_of(ib * SCAN_ROWS, SCAN_ROWS), SCAN_ROWS)
        h, hb = tile_scan(ab_ref[rows, :], ub_ref[rows, :], hb, True)
        ub_ref[rows, :] = h
        return hf, hb
    zero = jnp.zeros((SCAN_ROWS, w), F32)
    lax.fori_loop(0, n_tiles, scan_body, (zero, zero))

    def out(i, carry):
        rows = pl.ds(pl.multiple_of(i * rc, rc), rc)
        y_ref[rows, :] = ((uf_ref[rows, :] + ub_ref[rows, :]) * gl_ref[rows, :].astype(F32)).astype(BF16)
        return carry
    lax.fori_loop(0, n_rc, out, 0)


def _rglru(lx, lg, conv_w, conv_b, w_a, b_a, w_x, b_x, lam, n_ctx):
    bsz, t_all, width = lx.shape
    bw = width // LRU_BLOCKS
    seq = pl.BlockSpec((None, t_all, bw), lambda b, j: (b, 0, j))
    vec = lambda rows: pl.BlockSpec((rows, bw), lambda b, j: (0, j))
    wspec = pl.BlockSpec((2, None, bw, bw), lambda b, j: (0, j, 0, 0))
    return pl.pallas_call(
        functools.partial(_lru_kernel, n_ctx=n_ctx),
        grid=(bsz, LRU_BLOCKS),
        in_specs=[seq, seq, vec(LRU_CONV), vec(1), wspec, vec(2), wspec, vec(2), vec(2)],
        out_specs=seq,
        out_shape=jax.ShapeDtypeStruct((bsz, t_all, width), BF16),
        scratch_shapes=[pltpu.VMEM((t_all + 2 * V7X_SUBLANES, bw), F32)]
                       + [pltpu.VMEM((t_all, bw), F32)] * 4,
        compiler_params=_params("parallel", "parallel"),
        name="rglru_mixer",
    )(lx, lg, conv_w, conv_b, w_a, b_a, w_x, b_x, lam)


def _merge_kernel(x_ref, a_ref, b_ref, r_ref, mg_ref, wbr_ref, wo_ref, gate_ref, o_ref):
    d = x_ref.shape[-1]
    y = None
    for i, br in enumerate((a_ref, b_ref, r_ref)):
        term = mg_ref[:, i * d:(i + 1) * d].astype(F32) * _dot(br[...], wbr_ref[i])
        y = term if y is None else y + term
    o_ref[...] = x_ref[...] + gate_ref[...] * _dot(y.astype(BF16), wo_ref[...])


def _ffn_kernel(x_ref, g_ref, sh_ref, sc_ref, gate_ref, wg_ref, wu_ref, wd_ref, *rest, final):
    x = x_ref[...]
    h = _norm_mod(x, g_ref[...], sh_ref[...], sc_ref[...]).astype(BF16)
    act = (_silu(_dot(h, wg_ref[...])) * _dot(h, wu_ref[...])).astype(BF16)
    y = x + gate_ref[...] * _dot(act, wd_ref[...])
    if final:
        gf_ref, o_ref = rest
        ms = jnp.mean(y * y, axis=-1, keepdims=True)
        y = y * lax.rsqrt(ms + EPS) * gf_ref[...]
    else:
        (o_ref,) = rest
    o_ref[...] = y


def _merge_ffn(x, ya, yb, yr, mg, mods, w_br, w_o, g2, w_g, w_u, w_d, n_ctx_tiles, g_final):
    bsz, t_all, d = x.shape
    final = g_final is not None
    toff = n_ctx_tiles if final else 0
    n_t = t_all // TM - toff
    tile = lambda w: pl.BlockSpec((None, TM, w), lambda b, t: (b, t + toff, 0))
    out_tile = pl.BlockSpec((None, TM, d), lambda b, t: (b, t, 0))
    mod = lambda which: _mod_spec(mods, which, n_ctx_tiles, toff)
    full = lambda a: pl.BlockSpec(a.shape, lambda b, t: (0,) * a.ndim)
    out_sds = jax.ShapeDtypeStruct((bsz, n_t * TM, d), F32)
    x1 = pl.pallas_call(
        _merge_kernel,
        grid=(bsz, n_t),
        in_specs=[tile(d), tile(d), tile(d), tile(d), tile(N_BRANCH * d), full(w_br), full(w_o), mod(2)],
        out_specs=out_tile,
        out_shape=out_sds,
        compiler_params=_params("parallel", "parallel"),
        name="merge",
    )(x, ya, yb, yr, mg, w_br, w_o, mods)
    x1_tile = pl.BlockSpec((None, TM, d), lambda b, t: (b, t, 0))
    args = [x1, g2, mods, mods, mods, w_g, w_u, w_d]
    specs = [x1_tile, _vec_spec(d), mod(3), mod(4), mod(5), full(w_g), full(w_u), full(w_d)]
    if final:
        args.append(g_final)
        specs.append(_vec_spec(d))
    return pl.pallas_call(
        functools.partial(_ffn_kernel, final=final),
        grid=(bsz, n_t),
        in_specs=specs,
        out_specs=out_tile,
        out_shape=out_sds,
        compiler_params=_params("parallel", "parallel"),
        name="ffn",
    )(*args)


def _rope_tables(n_ctx, n_lat):
    rows = n_lat // GRID_W
    r = np.repeat(np.arange(rows), GRID_W).astype(np.float32)
    col = np.tile(np.arange(GRID_W), rows).astype(np.float32)
    half = ATT_HD // 2
    inv = np.power(np.float32(ROPE_THETA), -np.arange(0, half, 2, dtype=np.float32) / np.float32(half))
    ar = jnp.asarray(r[:, None] * inv[None, :], F32)
    ac = jnp.asarray(col[:, None] * inv[None, :], F32)
    cr, sr, cc, sc = jnp.cos(ar), jnp.sin(ar), jnp.cos(ac), jnp.sin(ac)
    cos_l = jnp.concatenate([cr, cr, cc, cc], axis=-1)
    sin_l = jnp.concatenate([-sr, sr, -sc, sc], axis=-1)
    cos_t = jnp.concatenate([jnp.ones((n_ctx, ATT_HD), F32), cos_l], axis=0)
    sin_t = jnp.concatenate([jnp.zeros((n_ctx, ATT_HD), F32), sin_l], axis=0)
    return cos_t, sin_t


def kernel(x, c, ctx, c_ctx, w_mod, b_mod, norm1, w_in, hg_lb_logits, hg_out_norm, att_q_norm, att_k_norm, lru_conv_w, lru_conv_b, lru_w_a, lru_b_a, lru_w_x, lru_b_x, lru_lambda, w_branch, w_out, norm2, w_ffn_in, w_ffn_out, norm_final):
    bsz, n_lat, d = x.shape
    n_ctx = ctx.shape[1]
    depth = w_in.shape[0]
    assert n_ctx % TM == 0 and n_lat % TM == 0 and n_lat % GRID_W == 0
    n_ctx_tiles = n_ctx // TM
    hg_w = HG_HEADS * HG_DK
    ffn_h = w_ffn_out.shape[1]

    c_all = jnp.concatenate([c, c_ctx[None, :]], axis=0)
    mods = _mod_vectors(c_all, w_mod, b_mod).reshape(depth, bsz + 1, N_MOD, 1, d)
    lb = _lower_bounds(hg_lb_logits)
    cos_t, sin_t = _rope_tables(n_ctx, n_lat)

    xs = jnp.concatenate([ctx, x], axis=1)
    for l in range(depth):
        last = l == depth - 1
        w_hg = w_in[l, :, :5 * hg_w].astype(BF16)
        w_rest = w_in[l, :, 5 * hg_w:].astype(BF16)
        hg, rest = _inproj(xs, mods[l], norm1[l][None, :], w_hg, w_rest,
                           lb[0, l][None, :], lb[1, l][None, :],
                           att_q_norm[l][None, :], att_k_norm[l][None, :], cos_t, sin_t, n_ctx_tiles)
        qs, hv, ff, fb, sg = hg
        aq, ak, av, lx, lg, mg = rest
        ya = _hgrn(qs, hv, ff, fb, sg, hg_out_norm[l][None, :], n_ctx)
        yb = _attention(aq, ak, av, n_ctx)
        yr = _rglru(lx, lg, lru_conv_w[l], lru_conv_b[l][None, :], lru_w_a[l].astype(BF16), lru_b_a[l],
                    lru_w_x[l].astype(BF16), lru_b_x[l], lru_lambda[l], n_ctx)
        xs = _merge_ffn(xs, ya, yb, yr, mg, mods[l], w_branch[l].astype(BF16), w_out[l].astype(BF16),
                        norm2[l][None, :], w_ffn_in[l, :, :ffn_h].astype(BF16),
                        w_ffn_in[l, :, ffn_h:].astype(BF16), w_ffn_out[l].astype(BF16),
                        n_ctx_tiles, norm_final[None, :] if last else None)
    return xs
```

```python
import functools

import numpy as np
import jax
import jax.numpy as jnp
from jax import lax
from jax.experimental import pallas as pl
from jax.experimental.pallas import tpu as pltpu

F32 = jnp.float32
BF16 = jnp.bfloat16

EPS = 1e-6
GRID_W = 64
HG_HEADS = 8
HG_DK = 128
ATT_HEADS = 8
ATT_KV_HEADS = 2
ATT_GROUP = ATT_HEADS // ATT_KV_HEADS
ATT_HD = 128
ROPE_THETA = 10000.0
LRU_BLOCKS = 4
LRU_CONV = 4
LRU_C = 8.0
N_BRANCH = 3
N_MOD = 6

V7X_VMEM_BYTES = 64 * 1024 * 1024
V7X_LANES = 128
V7X_SUBLANES = 8
VMEM_LIMIT = (V7X_VMEM_BYTES * 3) // 4

TM = 256
HG_CHUNK = 128
HG_HEADS_PER_STEP = 2
SCAN_ROWS = V7X_SUBLANES


def _silu(x):
    return x * jax.nn.sigmoid(x)


def _gelu_tanh(x):
    return 0.5 * x * (1.0 + jnp.tanh(np.sqrt(2.0 / np.pi) * (x + 0.044715 * (x * x * x))))


def _dot(a, b):
    return jnp.dot(a, b, preferred_element_type=F32)


def _dot_nt(a, b):
    return lax.dot_general(a, b, (((1,), (1,)), ((), ())), preferred_element_type=F32)


def _dot_tn(a, b):
    return lax.dot_general(a, b, (((0,), (0,)), ((), ())), preferred_element_type=F32)


def _params(*sem):
    return pltpu.CompilerParams(dimension_semantics=sem, vmem_limit_bytes=VMEM_LIMIT)


def _mod_kernel(c_ref, w_ref, b_ref, o_ref):
    ca = _silu(c_ref[...]).astype(BF16)
    o_ref[...] = _dot(ca, w_ref[...].astype(BF16)) + b_ref[...]


def _mod_vectors(c_all, w_mod, b_mod):
    depth, d, n = w_mod.shape
    rows = c_all.shape[0]
    tn = n // 4
    return pl.pallas_call(
        _mod_kernel,
        grid=(depth, n // tn),
        in_specs=[pl.BlockSpec((rows, d), lambda l, j: (0, 0)),
                  pl.BlockSpec((None, d, tn), lambda l, j: (l, 0, j)),
                  pl.BlockSpec((None, 1, tn), lambda l, j: (l, 0, j))],
        out_specs=pl.BlockSpec((None, rows, tn), lambda l, j: (l, 0, j)),
        out_shape=jax.ShapeDtypeStruct((depth, rows, n), F32),
        compiler_params=_params("parallel", "parallel"),
        name="mod_vectors",
    )(c_all, w_mod, b_mod.reshape(depth, 1, n))


def _lb_kernel(x_ref, o_ref):
    depth = x_ref.shape[1]
    rows = [x_ref[:, l, :] for l in range(depth)]
    m = functools.reduce(jnp.maximum, rows)
    e = [jnp.exp(r - m) for r in rows]
    inv = 1.0 / functools.reduce(lambda a, b: a + b, e)
    acc = jnp.zeros_like(rows[0])
    o_ref[:, 0, :] = acc
    for l in range(1, depth):
        acc = acc + e[l] * inv
        o_ref[:, l, :] = acc


def _lower_bounds(logits):
    return pl.pallas_call(
        _lb_kernel,
        out_shape=jax.ShapeDtypeStruct(logits.shape, F32),
        name="hg_lower_bounds",
    )(logits.astype(F32))


def _norm_mod(x, g, shift, scale):
    ms = jnp.mean(x * x, axis=-1, keepdims=True)
    y = x * lax.rsqrt(ms + EPS) * g
    return y * (1.0 + scale) + shift


def _inproj_hg_kernel(x_ref, g_ref, sh_ref, sc_ref, w_ref, lbf_ref, lbb_ref,
                      qs_ref, v_ref, ff_ref, fb_ref, sg_ref):
    h = _norm_mod(x_ref[...], g_ref[...], sh_ref[...], sc_ref[...]).astype(BF16)
    w = qs_ref.shape[-1]
    z = _dot(h, w_ref[:, 0:w])
    qs_ref[...] = (_silu(z) * (HG_DK ** -0.5)).astype(BF16)
    v_ref[...] = _dot(h, w_ref[:, w:2 * w]).astype(BF16)
    for k, (lb_ref, f_ref) in enumerate(((lbf_ref, ff_ref), (lbb_ref, fb_ref))):
        lb = lb_ref[...]
        z = _dot(h, w_ref[:, (2 + k) * w:(3 + k) * w])
        f_ref[...] = lb + (1.0 - lb) * jax.nn.sigmoid(z)
    sg_ref[...] = _silu(_dot(h, w_ref[:, 4 * w:5 * w])).astype(BF16)


def _inproj_rest_kernel(x_ref, g_ref, sh_ref, sc_ref, w_ref, qg_ref, kg_ref, cos_ref, sin_ref,
                        q_ref, k_ref, v_ref, lx_ref, lg_ref, mg_ref):
    h = _norm_mod(x_ref[...], g_ref[...], sh_ref[...], sc_ref[...]).astype(BF16)
    cos = cos_ref[...]
    sin = sin_ref[...]
    lane = lax.broadcasted_iota(jnp.int32, cos.shape, 1)
    quarter = ATT_HD // 4
    first = (lane % (2 * quarter)) < quarter

    def norm_rope(z, gain, scale):
        ms = jnp.mean(z * z, axis=-1, keepdims=True)
        y = z * lax.rsqrt(ms + EPS) * gain
        swapped = jnp.where(first, pltpu.roll(y, ATT_HD - quarter, 1), pltpu.roll(y, quarter, 1))
        return ((y * cos + swapped * sin) * scale).astype(BF16)

    nq = q_ref.shape[-1]
    nkv = k_ref.shape[-1]
    nl = lx_ref.shape[-1]
    z = _dot(h, w_ref[:, 0:nq])
    for i in range(nq // ATT_HD):
        sl = slice(i * ATT_HD, (i + 1) * ATT_HD)
        q_ref[:, sl] = norm_rope(z[:, sl], qg_ref[...], ATT_HD ** -0.5)
    z = _dot(h, w_ref[:, nq:nq + nkv])
    for i in range(nkv // ATT_HD):
        sl = slice(i * ATT_HD, (i + 1) * ATT_HD)
        k_ref[:, sl] = norm_rope(z[:, sl], kg_ref[...], 1.0)
    off = nq + nkv
    v_ref[...] = _dot(h, w_ref[:, off:off + nkv]).astype(BF16)
    off += nkv
    lx_ref[...] = _dot(h, w_ref[:, off:off + nl])
    off += nl
    lg_ref[...] = _gelu_tanh(_dot(h, w_ref[:, off:off + nl])).astype(BF16)
    off += nl
    mg_ref[...] = jax.nn.sigmoid(_dot(h, w_ref[:, off:off + mg_ref.shape[-1]])).astype(BF16)


def _tile_spec(width, dtype_unused=None):
    return pl.BlockSpec((None, TM, width), lambda b, t: (b, t, 0))


def _vec_spec(width):
    return pl.BlockSpec((1, width), lambda b, t: (0, 0))


def _mod_spec(mods, which, n_ctx_tiles, toff=0):
    ctx_row = mods.shape[0] - 1
    d = mods.shape[-1]
    return pl.BlockSpec((None, None, 1, d),
                        lambda b, t: (jnp.where(t + toff < n_ctx_tiles, ctx_row, b), which, 0, 0))


def _inproj(x, mods, g1, w_hg, w_rest, lbf, lbb, qg, kg, cos_t, sin_t, n_ctx_tiles):
    bsz, t_all, d = x.shape
    grid = (bsz, t_all // TM)
    hw = w_hg.shape[1] // 5
    nq = ATT_HEADS * ATT_HD
    nkv = ATT_KV_HEADS * ATT_HD
    common = [_tile_spec(d), _vec_spec(d), _mod_spec(mods, 0, n_ctx_tiles),
              _mod_spec(mods, 1, n_ctx_tiles)]
    sds = lambda w, dt: jax.ShapeDtypeStruct((bsz, t_all, w), dt)
    hg = pl.pallas_call(
        _inproj_hg_kernel,
        grid=grid,
        in_specs=common + [pl.BlockSpec(w_hg.shape, lambda b, t: (0, 0)), _vec_spec(hw), _vec_spec(hw)],
        out_specs=[_tile_spec(hw)] * 5,
        out_shape=[sds(hw, BF16), sds(hw, BF16), sds(hw, F32), sds(hw, F32), sds(hw, BF16)],
        compiler_params=_params("parallel", "parallel"),
        name="inproj_hgrn",
    )(x, g1, mods, mods, w_hg, lbf, lbb)
    nl = hw
    ng = w_rest.shape[1] - nq - 2 * nkv - 2 * nl
    rest = pl.pallas_call(
        _inproj_rest_kernel,
        grid=grid,
        in_specs=common + [pl.BlockSpec(w_rest.shape, lambda b, t: (0, 0)),
                           _vec_spec(ATT_HD), _vec_spec(ATT_HD),
                           pl.BlockSpec((TM, ATT_HD), lambda b, t: (t, 0)),
                           pl.BlockSpec((TM, ATT_HD), lambda b, t: (t, 0))],
        out_specs=[_tile_spec(nq), _tile_spec(nkv), _tile_spec(nkv), _tile_spec(nl), _tile_spec(nl),
                   _tile_spec(ng)],
        out_shape=[sds(nq, BF16), sds(nkv, BF16), sds(nkv, BF16), sds(nl, F32), sds(nl, BF16),
                   sds(ng, BF16)],
        compiler_params=_params("parallel", "parallel"),
        name="inproj_rest",
    )(x, g1, mods, mods, w_rest, qg, kg, cos_t, sin_t)
    return hg, rest


def _split3(g):
    g1 = g.astype(BF16)
    r1 = g - g1.astype(F32)
    g2 = r1.astype(BF16)
    g3 = (r1 - g2.astype(F32)).astype(BF16)
    return jnp.concatenate([g1, g2, g3], axis=-1)


def _block_row(b, blk, ref_row):
    c, kdim = b.shape
    y = b.reshape(c // V7X_SUBLANES, V7X_SUBLANES, kdim)
    pos = lax.broadcasted_iota(jnp.int32, (1, V7X_SUBLANES, 1), 1) % blk
    s = 1
    while ref_row + s < blk:
        y = jnp.where((pos >= ref_row + s) & (pos < ref_row + 2 * s), pltpu.roll(y, s, 1), y)
        s *= 2
    s = 1
    while ref_row - s >= 0:
        y = jnp.where((pos <= ref_row - s) & (pos > ref_row - 2 * s),
                      pltpu.roll(y, V7X_SUBLANES - s, 1), y)
        s *= 2
    return y.reshape(c, kdim)


def _neg_abs(x):
    bits = lax.bitcast_convert_type(x, jnp.uint32) | jnp.uint32(0x80000000)
    return lax.bitcast_convert_type(bits, F32)


def _hg_chunks(probs, pair_xor):
    c, width = probs[0]["q"].shape
    heads = [slice(h * HG_DK, (h + 1) * HG_DK) for h in range(width // HG_DK)]
    for p in probs:
        p["kk"] = (1.0 - p["f"]).astype(BF16)
        parts = _dot(p["tri"], _split3(jnp.log2(p["f"])))
        p["b"] = parts[:, 0:width] + parts[:, width:2 * width] + parts[:, 2 * width:3 * width]

    for p in probs:
        b = p["b"]
        qe = p["q"] * jnp.exp2(b).astype(BF16)
        p["o"] = [_dot_nt(qe[:, h], s.astype(BF16)) for h, s in zip(heads, p["st"])]
        b_end = b[0:1, :] if p["reverse"] else b[c - 1:c, :]
        ke = p["kk"] * jnp.exp2(b_end - b).astype(BF16)
        st_scale = jnp.exp2(b_end)
        p["st"] = [s * st_scale[:, h] + _dot_tn(p["v"][:, h], ke[:, h]) for h, s in zip(heads, p["st"])]
        p["a"] = None

    blk = c
    while blk >= 2:
        half = blk // 2
        for p in probs:
            b = p["b"]
            ref_row = half - 1 if p["reverse"] else half
            if blk > V7X_SUBLANES:
                b_ref = b.reshape(c // blk, blk, width)[:, ref_row:ref_row + 1, :]
                b_ref = jnp.broadcast_to(b_ref, (c // blk, blk, width)).reshape(c, width)
            else:
                b_ref = _block_row(b, blk, ref_row)
            e = jnp.exp2(_neg_abs(b - b_ref)).astype(BF16)
            qw = p["q"] * e
            kw = p["kk"] * e
            a_lvl = [_dot_nt(qw[:, h], kw[:, h]) for h in heads]
            p["a"] = a_lvl if p["a"] is None else [jnp.where(pair_xor < blk, al, ah)
                                                   for al, ah in zip(a_lvl, p["a"])]
        blk = half
    outs = []
    for p in probs:
        out = []
        for h, ah, oh in zip(heads, p["a"], p["o"]):
            ah = jnp.where(pair_xor == 0, _dot_nt(p["q"][:, h], p["kk"][:, h]), ah)
            ah = jnp.where(p["causal"], ah, 0.0)
            out.append(oh + _dot(ah.astype(BF16), p["v"][:, h]))
        outs.append(jnp.concatenate(out, axis=-1))
    return outs, [p["st"] for p in probs]


def _hgrn_kernel(qs_ref, v_ref, ff_ref, fb_ref, sg_ref, gain_ref, y_ref, of_ref, ob_ref, *, n_ctx):
    t_all, width = qs_ref.shape
    kdim = HG_DK
    c = HG_CHUNK
    n_chunks = t_all // c
    n_ctx_chunks = n_ctx // c
    row = lax.broadcasted_iota(jnp.int32, (c, c), 0)
    col = lax.broadcasted_iota(jnp.int32, (c, c), 1)
    pair_xor = row ^ col
    tri_f = jnp.where(col <= row, 1.0, 0.0).astype(BF16)
    tri_b = jnp.where(col >= row, 1.0, 0.0).astype(BF16)

    def body(i, carry):
        st_f, st_b = carry
        rows_f = pl.ds(pl.multiple_of(i * c, c), c)
        ib = jnp.where(i < n_ctx_chunks, n_ctx_chunks - 1 - i, n_chunks - 1 - (i - n_ctx_chunks))
        rows_b = pl.ds(pl.multiple_of(ib * c, c), c)
        probs = [dict(q=qs_ref[rows_f, :], v=v_ref[rows_f, :], f=ff_ref[rows_f, :], st=st_f,
                      reverse=False, tri=tri_f, causal=col <= row),
                 dict(q=qs_ref[rows_b, :], v=v_ref[rows_b, :], f=fb_ref[rows_b, :], st=st_b,
                      reverse=True, tri=tri_b, causal=col >= row)]
        (o_f, o_b), (st_f, st_b) = _hg_chunks(probs, pair_xor)
        of_ref[rows_f, :] = o_f
        ob_ref[rows_b, :] = o_b
        return st_f, st_b
    zero = [jnp.zeros((kdim, kdim), F32)] * (width // kdim)
    lax.fori_loop(0, n_chunks, body, (zero, zero))

    def readout(i, carry):
        rows = pl.ds(pl.multiple_of(i * c, c), c)
        for h in range(width // kdim):
            sl = slice(h * kdim, (h + 1) * kdim)
            o = of_ref[rows, sl] + ob_ref[rows, sl]
            ms = jnp.mean(o * o, axis=-1, keepdims=True)
            y = o * lax.rsqrt(ms + EPS) * gain_ref[...]
            y_ref[rows, sl] = (y * sg_ref[rows, sl].astype(F32)).astype(BF16)
        return carry
    lax.fori_loop(0, n_chunks, readout, 0)


def _hgrn(qs, v, ff, fb, sg, gain, n_ctx):
    bsz, t_all, width = qs.shape
    bw = HG_HEADS_PER_STEP * HG_DK
    spec = pl.BlockSpec((None, t_all, bw), lambda b, h: (b, 0, h))
    return pl.pallas_call(
        functools.partial(_hgrn_kernel, n_ctx=n_ctx),
        grid=(bsz, width // bw),
        in_specs=[spec] * 5 + [pl.BlockSpec((1, HG_DK), lambda b, h: (0, 0))],
        out_specs=spec,
        out_shape=jax.ShapeDtypeStruct((bsz, t_all, width), BF16),
        scratch_shapes=[pltpu.VMEM((t_all, bw), F32), pltpu.VMEM((t_all, bw), F32)],
        compiler_params=_params("parallel", "parallel"),
        name="hgrn2_mixer",
    )(qs, v, ff, fb, sg, gain)


def _att_kernel(q_ref, k_ref, v_ref, o_ref, *, n_ctx):
    qi = pl.program_id(2)
    n_ctx_tiles = n_ctx // q_ref.shape[0]

    def run(nk):
        k = k_ref[0:nk, :]
        v = v_ref[0:nk, :]
        for g in range(ATT_GROUP):
            sl = slice(g * ATT_HD, (g + 1) * ATT_HD)
            s = _dot_nt(q_ref[:, sl], k)
            p = jnp.exp(s - jnp.max(s, axis=-1, keepdims=True))
            den = jnp.sum(p, axis=-1, keepdims=True)
            o_ref[:, sl] = (_dot(p.astype(BF16), v) / den).astype(BF16)

    @pl.when(qi < n_ctx_tiles)
    def _():
        run(n_ctx)

    @pl.when(qi >= n_ctx_tiles)
    def _():
        run(k_ref.shape[0])


def _attention(q, k, v, n_ctx):
    bsz, t_all, nq = q.shape
    gw = ATT_GROUP * ATT_HD
    return pl.pallas_call(
        functools.partial(_att_kernel, n_ctx=n_ctx),
        grid=(bsz, ATT_KV_HEADS, t_all // TM),
        in_specs=[pl.BlockSpec((None, TM, gw), lambda b, h, i: (b, i, h)),
                  pl.BlockSpec((None, t_all, ATT_HD), lambda b, h, i: (b, 0, h)),
                  pl.BlockSpec((None, t_all, ATT_HD), lambda b, h, i: (b, 0, h))],
        out_specs=pl.BlockSpec((None, TM, gw), lambda b, h, i: (b, i, h)),
        out_shape=jax.ShapeDtypeStruct((bsz, t_all, nq), BF16),
        compiler_params=_params("parallel", "parallel", "arbitrary"),
        name="gqa_mixer",
    )(q, k, v)


def _lru_kernel(x_ref, gl_ref, cw_ref, cb_ref, wa_ref, ba_ref, wx_ref, bx_ref, lam_ref, y_ref,
                xp_ref, af_ref, uf_ref, ab_ref, ub_ref, *, n_ctx):
    t_all, w = x_ref.shape
    pad = V7X_SUBLANES
    rc = TM
    n_rc = t_all // rc

    xp_ref[0:pad, :] = jnp.zeros((pad, w), F32)
    xp_ref[pad + t_all:pad + t_all + pad, :] = jnp.zeros((pad, w), F32)

    def copy(i, carry):
        rows = pl.multiple_of(i * rc, rc)
        xp_ref[pl.ds(rows + pad, rc), :] = x_ref[pl.ds(rows, rc), :]
        return carry
    lax.fori_loop(0, n_rc, copy, 0)

    a_refs = (af_ref, ab_ref)
    u_refs = (uf_ref, ub_ref)
    sp = [jax.nn.softplus(-lam_ref[d:d + 1, :]) for d in range(2)]

    def gates(i, carry):
        r0 = pl.multiple_of(i * rc, rc)
        win = xp_ref[pl.ds(r0, rc + 2 * pad), :]
        trow = r0 + lax.broadcasted_iota(jnp.int32, (rc, 1), 0)
        xc = jnp.zeros((rc, w), F32) + cb_ref[...]
        for j in range(LRU_CONV):
            o = j - LRU_CONV // 2
            shifted = pltpu.roll(win, (rc + 2 * pad - o) % (rc + 2 * pad), 0)[pad:pad + rc, :]
            same_seg = ((trow + o) >= n_ctx) == (trow >= n_ctx)
            xc = xc + jnp.where(same_seg, shifted, 0.0) * cw_ref[j:j + 1, :]
        xcb = xc.astype(BF16)
        for d in range(2):
            r = jax.nn.sigmoid(_dot(xcb, wa_ref[d]) + ba_ref[d:d + 1, :])
            ig = jax.nn.sigmoid(_dot(xcb, wx_ref[d]) + bx_ref[d:d + 1, :])
            log_a = (-LRU_C) * r * sp[d]
            a = jnp.exp(log_a)
            one_minus_a2 = jnp.tanh(-log_a) * (1.0 + a * a)
            a_refs[d][pl.ds(r0, rc), :] = a
            u_refs[d][pl.ds(r0, rc), :] = jnp.sqrt(one_minus_a2) * (ig * xc)
        return carry
    lax.fori_loop(0, n_rc, gates, 0)

    n_tiles = t_all // SCAN_ROWS
    n_ctx_tiles = n_ctx // SCAN_ROWS
    sub = lax.broadcasted_iota(jnp.int32, (SCAN_ROWS, 1), 0)

    def tile_scan(a, u, h_in, reverse):
        s = 1
        while s < SCAN_ROWS:
            sh = (SCAN_ROWS - s) if reverse else s
            keep = (sub + s < SCAN_ROWS) if reverse else (sub >= s)
            a_s = jnp.where(keep, pltpu.roll(a, sh, 0), 1.0)
            u_s = jnp.where(keep, pltpu.roll(u, sh, 0), 0.0)
            u = u + a * u_s
            a = a * a_s
            s *= 2
        h = u + a * h_in
        edge = h[0:1, :] if reverse else h[SCAN_ROWS - 1:SCAN_ROWS, :]
        return h, jnp.broadcast_to(edge, h.shape)

    def scan_body(i, carry):
        hf, hb = carry
        rows = pl.ds(pl.multiple_of(i * SCAN_ROWS, SCAN_ROWS), SCAN_ROWS)
        h, hf = tile_scan(af_ref[rows, :], uf_ref[rows, :], hf, False)
        uf_ref[rows, :] = h
        ib = jnp.where(i < n_ctx_tiles, n_ctx_tiles - 1 - i, n_tiles - 1 - (i - n_ctx_tiles))
        rows = pl.ds(pl.multiple_of(ib * SCAN_ROWS, SCAN_ROWS), SCAN_ROWS)
        h, hb = tile_scan(ab_ref[rows, :], ub_ref[rows, :], hb, True)
        ub_ref[rows, :] = h
        return hf, hb
    zero = jnp.zeros((SCAN_ROWS, w), F32)
    lax.fori_loop(0, n_tiles, scan_body, (zero, zero))

    def out(i, carry):
        rows = pl.ds(pl.multiple_of(i * rc, rc), rc)
        y_ref[rows, :] = ((uf_ref[rows, :] + ub_ref[rows, :]) * gl_ref[rows, :].astype(F32)).astype(BF16)
        return carry
    lax.fori_loop(0, n_rc, out, 0)


def _rglru(lx, lg, conv_w, conv_b, w_a, b_a, w_x, b_x, lam, n_ctx):
    bsz, t_all, width = lx.shape
    bw = width // LRU_BLOCKS
    seq = pl.BlockSpec((None, t_all, bw), lambda b, j: (b, 0, j))
    vec = lambda rows: pl.BlockSpec((rows, bw), lambda b, j: (0, j))
    wspec = pl.BlockSpec((2, None, bw, bw), lambda b, j: (0, j, 0, 0))
    return pl.pallas_call(
        functools.partial(_lru_kernel, n_ctx=n_ctx),
        grid=(bsz, LRU_BLOCKS),
        in_specs=[seq, seq, vec(LRU_CONV), vec(1), wspec, vec(2), wspec, vec(2), vec(2)],
        out_specs=seq,
        out_shape=jax.ShapeDtypeStruct((bsz, t_all, width), BF16),
        scratch_shapes=[pltpu.VMEM((t_all + 2 * V7X_SUBLANES, bw), F32)]
                       + [pltpu.VMEM((t_all, bw), F32)] * 4,
        compiler_params=_params("parallel", "parallel"),
        name="rglru_mixer",
    )(lx, lg, conv_w, conv_b, w_a, b_a, w_x, b_x, lam)


def _merge_kernel(x_ref, a_ref, b_ref, r_ref, mg_ref, wbr_ref, wo_ref, gate_ref, o_ref):
    d = x_ref.shape[-1]
    y = None
    for i, br in enumerate((a_ref, b_ref, r_ref)):
        term = mg_ref[:, i * d:(i + 1) * d].astype(F32) * _dot(br[...], wbr_ref[i])
        y = term if y is None else y + term
    o_ref[...] = x_ref[...] + gate_ref[...] * _dot(y.astype(BF16), wo_ref[...])


def _ffn_kernel(x_ref, g_ref, sh_ref, sc_ref, gate_ref, wg_ref, wu_ref, wd_ref, *rest, final):
    x = x_ref[...]
    h = _norm_mod(x, g_ref[...], sh_ref[...], sc_ref[...]).astype(BF16)
    act = (_silu(_dot(h, wg_ref[...])) * _dot(h, wu_ref[...])).astype(BF16)
    y = x + gate_ref[...] * _dot(act, wd_ref[...])
    if final:
        gf_ref, o_ref = rest
        ms = jnp.mean(y * y, axis=-1, keepdims=True)
        y = y * lax.rsqrt(ms + EPS) * gf_ref[...]
    else:
        (o_ref,) = rest
    o_ref[...] = y


def _merge_ffn(x, ya, yb, yr, mg, mods, w_br, w_o, g2, w_g, w_u, w_d, n_ctx_tiles, g_final):
    bsz, t_all, d = x.shape
    final = g_final is not None
    toff = n_ctx_tiles if final else 0
    n_t = t_all // TM - toff
    tile = lambda w: pl.BlockSpec((None, TM, w), lambda b, t: (b, t + toff, 0))
    out_tile = pl.BlockSpec((None, TM, d), lambda b, t: (b, t, 0))
    mod = lambda which: _mod_spec(mods, which, n_ctx_tiles, toff)
    full = lambda a: pl.BlockSpec(a.shape, lambda b, t: (0,) * a.ndim)
    out_sds = jax.ShapeDtypeStruct((bsz, n_t * TM, d), F32)
    x1 = pl.pallas_call(
        _merge_kernel,
        grid=(bsz, n_t),
        in_specs=[tile(d), tile(d), tile(d), tile(d), tile(N_BRANCH * d), full(w_br), full(w_o), mod(2)],
        out_specs=out_tile,
        out_shape=out_sds,
        compiler_params=_params("parallel", "parallel"),
        name="merge",
    )(x, ya, yb, yr, mg, w_br, w_o, mods)
    x1_tile = pl.BlockSpec((None, TM, d), lambda b, t: (b, t, 0))
    args = [x1, g2, mods, mods, mods, w_g, w_u, w_d]
    specs = [x1_tile, _vec_spec(d), mod(3), mod(4), mod(5), full(w_g), full(w_u), full(w_d)]
    if final:
        args.append(g_final)
        specs.append(_vec_spec(d))
    return pl.pallas_call(
        functools.partial(_ffn_kernel, final=final),
        grid=(bsz, n_t),
        in_specs=specs,
        out_specs=out_tile,
        out_shape=out_sds,
        compiler_params=_params("parallel", "parallel"),
        name="ffn",
    )(*args)


def _rope_tables(n_ctx, n_lat):
    rows = n_lat // GRID_W
    r = np.repeat(np.arange(rows), GRID_W).astype(np.float32)
    col = np.tile(np.arange(GRID_W), rows).astype(np.float32)
    half = ATT_HD // 2
    inv = np.power(np.float32(ROPE_THETA), -np.arange(0, half, 2, dtype=np.float32) / np.float32(half))
    ar = jnp.asarray(r[:, None] * inv[None, :], F32)
    ac = jnp.asarray(col[:, None] * inv[None, :], F32)
    cr, sr, cc, sc = jnp.cos(ar), jnp.sin(ar), jnp.cos(ac), jnp.sin(ac)
    cos_l = jnp.concatenate([cr, cr, cc, cc], axis=-1)
    sin_l = jnp.concatenate([-sr, sr, -sc, sc], axis=-1)
    cos_t = jnp.concatenate([jnp.ones((n_ctx, ATT_HD), F32), cos_l], axis=0)
    sin_t = jnp.concatenate([jnp.zeros((n_ctx, ATT_HD), F32), sin_l], axis=0)
    return cos_t, sin_t


def kernel(x, c, ctx, c_ctx, w_mod, b_mod, norm1, w_in, hg_lb_logits, hg_out_norm, att_q_norm, att_k_norm, lru_conv_w, lru_conv_b, lru_w_a, lru_b_a, lru_w_x, lru_b_x, lru_lambda, w_branch, w_out, norm2, w_ffn_in, w_ffn_out, norm_final):
    bsz, n_lat, d = x.shape
    n_ctx = ctx.shape[1]
    depth = w_in.shape[0]
    assert n_ctx % TM == 0 and n_lat % TM == 0 and n_lat % GRID_W == 0
    n_ctx_tiles = n_ctx // TM
    hg_w = HG_HEADS * HG_DK
    ffn_h = w_ffn_out.shape[1]

    c_all = jnp.concatenate([c, c_ctx[None, :]], axis=0)
    mods = _mod_vectors(c_all, w_mod, b_mod).reshape(depth, bsz + 1, N_MOD, 1, d)
    lb = _lower_bounds(hg_lb_logits)
    cos_t, sin_t = _rope_tables(n_ctx, n_lat)

    xs = jnp.concatenate([ctx, x], axis=1)
    for l in range(depth):
        last = l == depth - 1
        w_hg = w_in[l, :, :5 * hg_w].astype(BF16)
        w_rest = w_in[l, :, 5 * hg_w:].astype(BF16)
        hg, rest = _inproj(xs, mods[l], norm1[l][None, :], w_hg, w_rest,
                           lb[0, l][None, :], lb[1, l][None, :],
                           att_q_norm[l][None, :], att_k_norm[l][None, :], cos_t, sin_t, n_ctx_tiles)
        qs, hv, ff, fb, sg = hg
        aq, ak, av, lx, lg, mg = rest
        ya = _hgrn(qs, hv, ff, fb, sg, hg_out_norm[l][None, :], n_ctx)
        yb = _attention(aq, ak, av, n_ctx)
        yr = _rglru(lx, lg, lru_conv_w[l], lru_conv_b[l][None, :], lru_w_a[l].astype(BF16), lru_b_a[l],
                    lru_w_x[l].astype(BF16), lru_b_x[l], lru_lambda[l], n_ctx)
        xs = _merge_ffn(xs, ya, yb, yr, mg, mods[l], w_branch[l].astype(BF16), w_out[l].astype(BF16),
                        norm2[l][None, :], w_ffn_in[l, :, :ffn_h].astype(BF16),
                        w_ffn_in[l, :, ffn_h:].astype(BF16), w_ffn_out[l].astype(BF16),
                        n_ctx_tiles, norm_final[None, :] if last else None)
    return xs
```

```python
import functools

import numpy as np
import jax
import jax.numpy as jnp
from jax import lax
from jax.experimental import pallas as pl
from jax.experimental.pallas import tpu as pltpu

F32 = jnp.float32
BF16 = jnp.bfloat16

EPS = 1e-6
GRID_W = 64
HG_HEADS = 8
HG_DK = 128
ATT_HEADS = 8
ATT_KV_HEADS = 2
ATT_GROUP = ATT_HEADS // ATT_KV_HEADS
ATT_HD = 128
ROPE_THETA = 10000.0
LRU_BLOCKS = 4
LRU_CONV = 4
LRU_C = 8.0
N_BRANCH = 3
N_MOD = 6

V7X_VMEM_BYTES = 64 * 1024 * 1024
V7X_LANES = 128
V7X_SUBLANES = 8
VMEM_LIMIT = (V7X_VMEM_BYTES * 3) // 4

TM = 256
HG_CHUNK = 128
HG_HEADS_PER_STEP = 2
SCAN_ROWS = V7X_SUBLANES


def _silu(x):
    return x * jax.nn.sigmoid(x)


def _gelu_tanh(x):
    return 0.5 * x * (1.0 + jnp.tanh(np.sqrt(2.0 / np.pi) * (x + 0.044715 * (x * x * x))))


def _dot(a, b):
    return jnp.dot(a, b, preferred_element_type=F32)


def _dot_nt(a, b):
    return lax.dot_general(a, b, (((1,), (1,)), ((), ())), preferred_element_type=F32)


def _dot_tn(a, b):
    return lax.dot_general(a, b, (((0,), (0,)), ((), ())), preferred_element_type=F32)


def _params(*sem):
    return pltpu.CompilerParams(dimension_semantics=sem, vmem_limit_bytes=VMEM_LIMIT)


def _mod_kernel(c_ref, w_ref, b_ref, o_ref):
    ca = _silu(c_ref[...]).astype(BF16)
    o_ref[...] = _dot(ca, w_ref[...].astype(BF16)) + b_ref[...]


def _mod_vectors(c_all, w_mod, b_mod):
    depth, d, n = w_mod.shape
    rows = c_all.shape[0]
    tn = n // 4
    return pl.pallas_call(
        _mod_kernel,
        grid=(depth, n // tn),
        in_specs=[pl.BlockSpec((rows, d), lambda l, j: (0, 0)),
                  pl.BlockSpec((None, d, tn), lambda l, j: (l, 0, j)),
                  pl.BlockSpec((None, 1, tn), lambda l, j: (l, 0, j))],
        out_specs=pl.BlockSpec((None, rows, tn), lambda l, j: (l, 0, j)),
        out_shape=jax.ShapeDtypeStruct((depth, rows, n), F32),
        compiler_params=_params("parallel", "parallel"),
        name="mod_vectors",
    )(c_all, w_mod, b_mod.reshape(depth, 1, n))


def _lb_kernel(x_ref, o_ref):
    depth = x_ref.shape[1]
    rows = [x_ref[:, l, :] for l in range(depth)]
    m = functools.reduce(jnp.maximum, rows)
    e = [jnp.exp(r - m) for r in rows]
    inv = 1.0 / functools.reduce(lambda a, b: a + b, e)
    acc = jnp.zeros_like(rows[0])
    o_ref[:, 0, :] = acc
    for l in range(1, depth):
        acc = acc + e[l] * inv
        o_ref[:, l, :] = acc


def _lower_bounds(logits):
    return pl.pallas_call(
        _lb_kernel,
        out_shape=jax.ShapeDtypeStruct(logits.shape, F32),
        name="hg_lower_bounds",
    )(logits.astype(F32))


def _norm_mod(x, g, shift, scale):
    ms = jnp.mean(x * x, axis=-1, keepdims=True)
    y = x * lax.rsqrt(ms + EPS) * g
    return y * (1.0 + scale) + shift


def _inproj_hg_kernel(x_ref, g_ref, sh_ref, sc_ref, w_ref, lbf_ref, lbb_ref,
                      qs_ref, v_ref, ff_ref, fb_ref, sg_ref):
    h = _norm_mod(x_ref[...], g_ref[...], sh_ref[...], sc_ref[...]).astype(BF16)
    w = qs_ref.shape[-1]
    z = _dot(h, w_ref[:, 0:w])
    qs_ref[...] = (_silu(z) * (HG_DK ** -0.5)).astype(BF16)
    v_ref[...] = _dot(h, w_ref[:, w:2 * w]).astype(BF16)
    for k, (lb_ref, f_ref) in enumerate(((lbf_ref, ff_ref), (lbb_ref, fb_ref))):
        lb = lb_ref[...]
        z = _dot(h, w_ref[:, (2 + k) * w:(3 + k) * w])
        f_ref[...] = lb + (1.0 - lb) * jax.nn.sigmoid(z)
    sg_ref[...] = _silu(_dot(h, w_ref[:, 4 * w:5 * w])).astype(BF16)


def _inproj_rest_kernel(x_ref, g_ref, sh_ref, sc_ref, w_ref, qg_ref, kg_ref, cos_ref, sin_ref,
                        q_ref, k_ref, v_ref, lx_ref, lg_ref, mg_ref):
    h = _norm_mod(x_ref[...], g_ref[...], sh_ref[...], sc_ref[...]).astype(BF16)
    cos = cos_ref[...]
    sin = sin_ref[...]
    lane = lax.broadcasted_iota(jnp.int32, cos.shape, 1)
    quarter = ATT_HD // 4
    first = (lane % (2 * quarter)) < quarter

    def norm_rope(z, gain, scale):
        ms = jnp.mean(z * z, axis=-1, keepdims=True)
        y = z * lax.rsqrt(ms + EPS) * gain
        swapped = jnp.where(first, pltpu.roll(y, ATT_HD - quarter, 1), pltpu.roll(y, quarter, 1))
        return ((y * cos + swapped * sin) * scale).astype(BF16)

    nq = q_ref.shape[-1]
    nkv = k_ref.shape[-1]
    nl = lx_ref.shape[-1]
    z = _dot(h, w_ref[:, 0:nq])
    for i in range(nq // ATT_HD):
        sl = slice(i * ATT_HD, (i + 1) * ATT_HD)
        q_ref[:, sl] = norm_rope(z[:, sl], qg_ref[...], ATT_HD ** -0.5)
    z = _dot(h, w_ref[:, nq:nq + nkv])
    for i in range(nkv // ATT_HD):
        sl = slice(i * ATT_HD, (i + 1) * ATT_HD)
        k_ref[:, sl] = norm_rope(z[:, sl], kg_ref[...], 1.0)
    off = nq + nkv
    v_ref[...] = _dot(h, w_ref[:, off:off + nkv]).astype(BF16)
    off += nkv
    lx_ref[...] = _dot(h, w_ref[:, off:off + nl])
    off += nl
    lg_ref[...] = _gelu_tanh(_dot(h, w_ref[:, off:off + nl])).astype(BF16)
    off += nl
    mg_ref[...] = jax.nn.sigmoid(_dot(h, w_ref[:, off:off + mg_ref.shape[-1]])).astype(BF16)


def _tile_spec(width, dtype_unused=None):
    return pl.BlockSpec((None, TM, width), lambda b, t: (b, t, 0))


def _vec_spec(width):
    return pl.BlockSpec((1, width), lambda b, t: (0, 0))


def _mod_spec(mods, which, n_ctx_tiles, toff=0):
    ctx_row = mods.shape[0] - 1
    d = mods.shape[-1]
    return pl.BlockSpec((None, None, 1, d),
                        lambda b, t: (jnp.where(t + toff < n_ctx_tiles, ctx_row, b), which, 0, 0))


def _inproj(x, mods, g1, w_hg, w_rest, lbf, lbb, qg, kg, cos_t, sin_t, n_ctx_tiles):
    bsz, t_all, d = x.shape
    grid = (bsz, t_all // TM)
    hw = w_hg.shape[1] // 5
    nq = ATT_HEADS * ATT_HD
    nkv = ATT_KV_HEADS * ATT_HD
    common = [_tile_spec(d), _vec_spec(d), _mod_spec(mods, 0, n_ctx_tiles),
              _mod_spec(mods, 1, n_ctx_tiles)]
    sds = lambda w, dt: jax.ShapeDtypeStruct((bsz, t_all, w), dt)
    hg = pl.pallas_call(
        _inproj_hg_kernel,
        grid=grid,
        in_specs=common + [pl.BlockSpec(w_hg.shape, lambda b, t: (0, 0)), _vec_spec(hw), _vec_spec(hw)],
        out_specs=[_tile_spec(hw)] * 5,
        out_shape=[sds(hw, BF16), sds(hw, BF16), sds(hw, F32), sds(hw, F32), sds(hw, BF16)],
        compiler_params=_params("parallel", "parallel"),
        name="inproj_hgrn",
    )(x, g1, mods, mods, w_hg, lbf, lbb)
    nl = hw
    ng = w_rest.shape[1] - nq - 2 * nkv - 2 * nl
    rest = pl.pallas_call(
        _inproj_rest_kernel,
        grid=grid,
        in_specs=common + [pl.BlockSpec(w_rest.shape, lambda b, t: (0, 0)),
                           _vec_spec(ATT_HD), _vec_spec(ATT_HD),
                           pl.BlockSpec((TM, ATT_HD), lambda b, t: (t, 0)),
                           pl.BlockSpec((TM, ATT_HD), lambda b, t: (t, 0))],
        out_specs=[_tile_spec(nq), _tile_spec(nkv), _tile_spec(nkv), _tile_spec(nl), _tile_spec(nl),
                   _tile_spec(ng)],
        out_shape=[sds(nq, BF16), sds(nkv, BF16), sds(nkv, BF16), sds(nl, F32), sds(nl, BF16),
                   sds(ng, BF16)],
        compiler_params=_params("parallel", "parallel"),
        name="inproj_rest",
    )(x, g1, mods, mods, w_rest, qg, kg, cos_t, sin_t)
    return hg, rest


def _split3(g):
    g1 = g.astype(BF16)
    r1 = g - g1.astype(F32)
    g2 = r1.astype(BF16)
    g3 = (r1 - g2.astype(F32)).astype(BF16)
    return jnp.concatenate([g1, g2, g3], axis=-1)


def _block_row(b, blk, ref_row):
    c, kdim = b.shape
    y = b.reshape(c // V7X_SUBLANES, V7X_SUBLANES, kdim)
    pos = lax.broadcasted_iota(jnp.int32, (1, V7X_SUBLANES, 1), 1) % blk
    s = 1
    while ref_row + s < blk:
        y = jnp.where((pos >= ref_row + s) & (pos < ref_row + 2 * s), pltpu.roll(y, s, 1), y)
        s *= 2
    s = 1
    while ref_row - s >= 0:
        y = jnp.where((pos <= ref_row - s) & (pos > ref_row - 2 * s),
                      pltpu.roll(y, V7X_SUBLANES - s, 1), y)
        s *= 2
    return y.reshape(c, kdim)


def _neg_abs(x):
    bits = lax.bitcast_convert_type(x, jnp.uint32) | jnp.uint32(0x80000000)
    return lax.bitcast_convert_type(bits, F32)


def _hg_chunks(probs, pair_xor):
    c, width = probs[0]["q"].shape
    heads = [slice(h * HG_DK, (h + 1) * HG_DK) for h in range(width // HG_DK)]
    for p in probs:
        p["kk"] = (1.0 - p["f"]).astype(BF16)
        b = p["b"]
        qe = p["q"] * jnp.exp2(b).astype(BF16)
        p["o"] = [_dot_nt(qe[:, h], s.astype(BF16)) for h, s in zip(heads, p["st"])]
        b_end = b[0:1, :] if p["reverse"] else b[c - 1:c, :]
        ke = p["kk"] * jnp.exp2(b_end - b).astype(BF16)
        st_scale = jnp.exp2(b_end)
        p["st"] = [s * st_scale[:, h] + _dot_tn(p["v"][:, h], ke[:, h]) for h, s in zip(heads, p["st"])]
        p["a"] = None

    blk = c
    while blk >= 2:
        half = blk // 2
        for p in probs:
            b = p["b"]
            ref_row = half - 1 if p["reverse"] else half
            if blk > V7X_SUBLANES:
                b_ref = b.reshape(c // blk, blk, width)[:, ref_row:ref_row + 1, :]
                b_ref = jnp.broadcast_to(b_ref, (c // blk, blk, width)).reshape(c, width)
            else:
                b_ref = _block_row(b, blk, ref_row)
            e = jnp.exp2(_neg_abs(b - b_ref)).astype(BF16)
            qw = p["q"] * e
            kw = p["kk"] * e
            a_lvl = [_dot_nt(qw[:, h], kw[:, h]) for h in heads]
            p["a"] = a_lvl if p["a"] is None else [jnp.where(pair_xor < blk, al, ah)
                                                   for al, ah in zip(a_lvl, p["a"])]
        blk = half
    for p in probs:
        pairs = []
        for h, ah in zip(heads, p["a"]):
            ah = jnp.where(pair_xor == 0, _dot_nt(p["q"][:, h], p["kk"][:, h]), ah)
            pairs.append(jnp.where(p["causal"], ah, 0.0).astype(BF16))
        p["a"] = pairs
    return ([jnp.concatenate(p["o"], axis=-1) for p in probs], [p["a"] for p in probs],
            [p["st"] for p in probs])


def _hg_running_decay(f, tri):
    width = f.shape[-1]
    parts = _dot(tri, _split3(jnp.log2(f)))
    return parts[:, 0:width] + parts[:, width:2 * width] + parts[:, 2 * width:3 * width]


def _hgrn_kernel(qs_ref, v_ref, ff_ref, fb_ref, sg_ref, gain_ref, y_ref, of_ref, ob_ref, *, n_ctx):
    t_all, width = qs_ref.shape
    kdim = HG_DK
    c = HG_CHUNK
    n_chunks = t_all // c
    n_ctx_chunks = n_ctx // c
    row = lax.broadcasted_iota(jnp.int32, (c, c), 0)
    col = lax.broadcasted_iota(jnp.int32, (c, c), 1)
    pair_xor = row ^ col
    tri_f = jnp.where(col <= row, 1.0, 0.0).astype(BF16)
    tri_b = jnp.where(col >= row, 1.0, 0.0).astype(BF16)

    n_heads = width // kdim
    f_refs = (ff_ref, fb_ref)
    o_refs = (of_ref, ob_ref)
    tris = (tri_f, tri_b)

    def chunk_ids(i):
        i = jnp.minimum(i, n_chunks - 1)
        ib = jnp.where(i < n_ctx_chunks, n_ctx_chunks - 1 - i, n_chunks - 1 - (i - n_ctx_chunks))
        return (i, ib)

    def rows_of(chunk):
        return pl.ds(pl.multiple_of(chunk * c, c), c)

    def finish(chunks, pairs):
        for o_ref, chunk, a in zip(o_refs, chunks, pairs):
            rows = rows_of(chunk)
            add = [_dot(a[h], v_ref[rows, h * kdim:(h + 1) * kdim]) for h in range(n_heads)]
            o_ref[rows, :] += jnp.concatenate(add, axis=-1)

    def body(i, carry):
        sts, decays, prev_chunks, prev_pairs = carry
        finish(prev_chunks, prev_pairs)
        chunks = chunk_ids(i)
        probs = [dict(q=qs_ref[rows_of(ch), :], v=v_ref[rows_of(ch), :], f=f_ref[rows_of(ch), :],
                      b=b, st=st, reverse=d == 1, causal=(col >= row) if d else (col <= row))
                 for d, (ch, f_ref, b, st) in enumerate(zip(chunks, f_refs, decays, sts))]
        outs, pairs, sts = _hg_chunks(probs, pair_xor)
        for o_ref, ch, o in zip(o_refs, chunks, outs):
            o_ref[rows_of(ch), :] = o
        decays = [_hg_running_decay(f_ref[rows_of(ch), :], tri)
                  for ch, f_ref, tri in zip(chunk_ids(i + 1), f_refs, tris)]
        return sts, decays, chunks, pairs

    first = chunk_ids(0)
    for o_ref, ch in zip(o_refs, first):
        o_ref[rows_of(ch), :] = jnp.zeros((c, width), F32)
    zero_st = [[jnp.zeros((kdim, kdim), F32)] * n_heads] * 2
    zero_pairs = [[jnp.zeros((c, c), BF16)] * n_heads] * 2
    decays0 = [_hg_running_decay(f_ref[rows_of(ch), :], tri)
               for ch, f_ref, tri in zip(first, f_refs, tris)]
    _, _, last_chunks, last_pairs = lax.fori_loop(
        0, n_chunks, body, (zero_st, decays0, first, zero_pairs))
    finish(last_chunks, last_pairs)

    def readout(i, carry):
        rows = pl.ds(pl.multiple_of(i * c, c), c)
        for h in range(width // kdim):
            sl = slice(h * kdim, (h + 1) * kdim)
            o = of_ref[rows, sl] + ob_ref[rows, sl]
            ms = jnp.mean(o * o, axis=-1, keepdims=True)
            y = o * lax.rsqrt(ms + EPS) * gain_ref[...]
            y_ref[rows, sl] = (y * sg_ref[rows, sl].astype(F32)).astype(BF16)
        return carry
    lax.fori_loop(0, n_chunks, readout, 0)


def _hgrn(qs, v, ff, fb, sg, gain, n_ctx):
    bsz, t_all, width = qs.shape
    bw = HG_HEADS_PER_STEP * HG_DK
    spec = pl.BlockSpec((None, t_all, bw), lambda b, h: (b, 0, h))
    return pl.pallas_call(
        functools.partial(_hgrn_kernel, n_ctx=n_ctx),
        grid=(bsz, width // bw),
        in_specs=[spec] * 5 + [pl.BlockSpec((1, HG_DK), lambda b, h: (0, 0))],
        out_specs=spec,
        out_shape=jax.ShapeDtypeStruct((bsz, t_all, width), BF16),
        scratch_shapes=[pltpu.VMEM((t_all, bw), F32), pltpu.VMEM((t_all, bw), F32)],
        compiler_params=_params("parallel", "parallel"),
        name="hgrn2_mixer",
    )(qs, v, ff, fb, sg, gain)


def _att_kernel(q_ref, k_ref, v_ref, o_ref, *, n_ctx):
    qi = pl.program_id(2)
    n_ctx_tiles = n_ctx // q_ref.shape[0]

    def run(nk):
        k = k_ref[0:nk, :]
        v = v_ref[0:nk, :]
        for g in range(ATT_GROUP):
            sl = slice(g * ATT_HD, (g + 1) * ATT_HD)
            s = _dot_nt(q_ref[:, sl], k)
            p = jnp.exp(s - jnp.max(s, axis=-1, keepdims=True))
            den = jnp.sum(p, axis=-1, keepdims=True)
            o_ref[:, sl] = (_dot(p.astype(BF16), v) / den).astype(BF16)

    @pl.when(qi < n_ctx_tiles)
    def _():
        run(n_ctx)

    @pl.when(qi >= n_ctx_tiles)
    def _():
        run(k_ref.shape[0])


def _attention(q, k, v, n_ctx):
    bsz, t_all, nq = q.shape
    gw = ATT_GROUP * ATT_HD
    return pl.pallas_call(
        functools.partial(_att_kernel, n_ctx=n_ctx),
        grid=(bsz, ATT_KV_HEADS, t_all // TM),
        in_specs=[pl.BlockSpec((None, TM, gw), lambda b, h, i: (b, i, h)),
                  pl.BlockSpec((None, t_all, ATT_HD), lambda b, h, i: (b, 0, h)),
                  pl.BlockSpec((None, t_all, ATT_HD), lambda b, h, i: (b, 0, h))],
        out_specs=pl.BlockSpec((None, TM, gw), lambda b, h, i: (b, i, h)),
        out_shape=jax.ShapeDtypeStruct((bsz, t_all, nq), BF16),
        compiler_params=_params("parallel", "parallel", "arbitrary"),
        name="gqa_mixer",
    )(q, k, v)


def _lru_kernel(x_ref, gl_ref, cw_ref, cb_ref, wa_ref, ba_ref, wx_ref, bx_ref, lam_ref, y_ref,
                xp_ref, af_ref, uf_ref, ab_ref, ub_ref, *, n_ctx):
    t_all, w = x_ref.shape
    pad = V7X_SUBLANES
    rc = TM
    n_rc = t_all // rc

    xp_ref[0:pad, :] = jnp.zeros((pad, w), F32)
    xp_ref[pad + t_all:pad + t_all + pad, :] = jnp.zeros((pad, w), F32)

    def copy(i, carry):
        rows = pl.multiple_of(i * rc, rc)
        xp_ref[pl.ds(rows + pad, rc), :] = x_ref[pl.ds(rows, rc), :]
        return carry
    lax.fori_loop(0, n_rc, copy, 0)

    a_refs = (af_ref, ab_ref)
    u_refs = (uf_ref, ub_ref)
    sp = [jax.nn.softplus(-lam_ref[d:d + 1, :]) for d in range(2)]

    def gates(i, carry):
        r0 = pl.multiple_of(i * rc, rc)
        win = xp_ref[pl.ds(r0, rc + 2 * pad), :]
        trow = r0 + lax.broadcasted_iota(jnp.int32, (rc, 1), 0)
        xc = jnp.zeros((rc, w), F32) + cb_ref[...]
        for j in range(LRU_CONV):
            o = j - LRU_CONV // 2
            shifted = pltpu.roll(win, (rc + 2 * pad - o) % (rc + 2 * pad), 0)[pad:pad + rc, :]
            same_seg = ((trow + o) >= n_ctx) == (trow >= n_ctx)
            xc = xc + jnp.where(same_seg, shifted, 0.0) * cw_ref[j:j + 1, :]
        xcb = xc.astype(BF16)
        for d in range(2):
            r = jax.nn.sigmoid(_dot(xcb, wa_ref[d]) + ba_ref[d:d + 1, :])
            ig = jax.nn.sigmoid(_dot(xcb, wx_ref[d]) + bx_ref[d:d + 1, :])
            log_a = (-LRU_C) * r * sp[d]
            a = jnp.exp(log_a)
            one_minus_a2 = jnp.tanh(-log_a) * (1.0 + a * a)
            a_refs[d][pl.ds(r0, rc), :] = a
            u_refs[d][pl.ds(r0, rc), :] = jnp.sqrt(one_minus_a2) * (ig * xc)
        return carry
    lax.fori_loop(0, n_rc, gates, 0)

    n_tiles = t_all // SCAN_ROWS
    n_ctx_tiles = n_ctx // SCAN_ROWS
    sub = lax.broadcasted_iota(jnp.int32, (SCAN_ROWS, 1), 0)

    def tile_scan(a, u, h_in, reverse):
        s = 1
        while s < SCAN_ROWS:
            sh = (SCAN_ROWS - s) if reverse else s
            keep = (sub + s < SCAN_ROWS) if reverse else (sub >= s)
            a_s = jnp.where(keep, pltpu.roll(a, sh, 0), 1.0)
            u_s = jnp.where(keep, pltpu.roll(u, sh, 0), 0.0)
            u = u + a * u_s
            a = a * a_s
            s *= 2
        h = u + a * h_in
        edge = h[0:1, :] if reverse else h[SCAN_ROWS - 1:SCAN_ROWS, :]
        return h, jnp.broadcast_to(edge, h.shape)

    def scan_body(i, carry):
        hf, hb = carry
        rows = pl.ds(pl.multiple_of(i * SCAN_ROWS, SCAN_ROWS), SCAN_ROWS)
        h, hf = tile_scan(af_ref[rows, :], uf_ref[rows, :], hf, False)
        uf_ref[rows, :] = h
        ib = jnp.where(i < n_ctx_tiles, n_ctx_tiles - 1 - i, n_tiles - 1 - (i - n_ctx_tiles))
        rows = pl.ds(pl.multiple_of(ib * SCAN_ROWS, SCAN_ROWS), SCAN_ROWS)
        h, hb = tile_scan(ab_ref[rows, :], ub_ref[rows, :], hb, True)
        ub_ref[rows, :] = h
        return hf, hb
    zero = jnp.zeros((SCAN_ROWS, w), F32)
    lax.fori_loop(0, n_tiles, scan_body, (zero, zero))

    def out(i, carry):
        rows = pl.ds(pl.multiple_of(i * rc, rc), rc)
        y_ref[rows, :] = ((uf_ref[rows, :] + ub_ref[rows, :]) * gl_ref[rows, :].astype(F32)).astype(BF16)
        return carry
    lax.fori_loop(0, n_rc, out, 0)


def _rglru(lx, lg, conv_w, conv_b, w_a, b_a, w_x, b_x, lam, n_ctx):
    bsz, t_all, width = lx.shape
    bw = width // LRU_BLOCKS
    seq = pl.BlockSpec((None, t_all, bw), lambda b, j: (b, 0, j))
    vec = lambda rows: pl.BlockSpec((rows, bw), lambda b, j: (0, j))
    wspec = pl.BlockSpec((2, None, bw, bw), lambda b, j: (0, j, 0, 0))
    return pl.pallas_call(
        functools.partial(_lru_kernel, n_ctx=n_ctx),
        grid=(bsz, LRU_BLOCKS),
        in_specs=[seq, seq, vec(LRU_CONV), vec(1), wspec, vec(2), wspec, vec(2), vec(2)],
        out_specs=seq,
        out_shape=jax.ShapeDtypeStruct((bsz, t_all, width), BF16),
        scratch_shapes=[pltpu.VMEM((t_all + 2 * V7X_SUBLANES, bw), F32)]
                       + [pltpu.VMEM((t_all, bw), F32)] * 4,
        compiler_params=_params("parallel", "parallel"),
        name="rglru_mixer",
    )(lx, lg, conv_w, conv_b, w_a, b_a, w_x, b_x, lam)


def _merge_kernel(x_ref, a_ref, b_ref, r_ref, mg_ref, wbr_ref, wo_ref, gate_ref, o_ref):
    d = x_ref.shape[-1]
    y = None
    for i, br in enumerate((a_ref, b_ref, r_ref)):
        term = mg_ref[:, i * d:(i + 1) * d].astype(F32) * _dot(br[...], wbr_ref[i])
        y = term if y is None else y + term
    o_ref[...] = x_ref[...] + gate_ref[...] * _dot(y.astype(BF16), wo_ref[...])


def _ffn_kernel(x_ref, g_ref, sh_ref, sc_ref, gate_ref, wg_ref, wu_ref, wd_ref, *rest, final):
    x = x_ref[...]
    h = _norm_mod(x, g_ref[...], sh_ref[...], sc_ref[...]).astype(BF16)
    act = (_silu(_dot(h, wg_ref[...])) * _dot(h, wu_ref[...])).astype(BF16)
    y = x + gate_ref[...] * _dot(act, wd_ref[...])
    if final:
        gf_ref, o_ref = rest
        ms = jnp.mean(y * y, axis=-1, keepdims=True)
        y = y * lax.rsqrt(ms + EPS) * gf_ref[...]
    else:
        (o_ref,) = rest
    o_ref[...] = y


def _merge_ffn(x, ya, yb, yr, mg, mods, w_br, w_o, g2, w_g, w_u, w_d, n_ctx_tiles, g_final):
    bsz, t_all, d = x.shape
    final = g_final is not None
    toff = n_ctx_tiles if final else 0
    n_t = t_all // TM - toff
    tile = lambda w: pl.BlockSpec((None, TM, w), lambda b, t: (b, t + toff, 0))
    out_tile = pl.BlockSpec((None, TM, d), lambda b, t: (b, t, 0))
    mod = lambda which: _mod_spec(mods, which, n_ctx_tiles, toff)
    full = lambda a: pl.BlockSpec(a.shape, lambda b, t: (0,) * a.ndim)
    out_sds = jax.ShapeDtypeStruct((bsz, n_t * TM, d), F32)
    x1 = pl.pallas_call(
        _merge_kernel,
        grid=(bsz, n_t),
        in_specs=[tile(d), tile(d), tile(d), tile(d), tile(N_BRANCH * d), full(w_br), full(w_o), mod(2)],
        out_specs=out_tile,
        out_shape=out_sds,
        compiler_params=_params("parallel", "parallel"),
        name="merge",
    )(x, ya, yb, yr, mg, w_br, w_o, mods)
    x1_tile = pl.BlockSpec((None, TM, d), lambda b, t: (b, t, 0))
    args = [x1, g2, mods, mods, mods, w_g, w_u, w_d]
    specs = [x1_tile, _vec_spec(d), mod(3), mod(4), mod(5), full(w_g), full(w_u), full(w_d)]
    if final:
        args.append(g_final)
        specs.append(_vec_spec(d))
    return pl.pallas_call(
        functools.partial(_ffn_kernel, final=final),
        grid=(bsz, n_t),
        in_specs=specs,
        out_specs=out_tile,
        out_shape=out_sds,
        compiler_params=_params("parallel", "parallel"),
        name="ffn",
    )(*args)


def _rope_tables(n_ctx, n_lat):
    rows = n_lat // GRID_W
    r = np.repeat(np.arange(rows), GRID_W).astype(np.float32)
    col = np.tile(np.arange(GRID_W), rows).astype(np.float32)
    half = ATT_HD // 2
    inv = np.power(np.float32(ROPE_THETA), -np.arange(0, half, 2, dtype=np.float32) / np.float32(half))
    ar = jnp.asarray(r[:, None] * inv[None, :], F32)
    ac = jnp.asarray(col[:, None] * inv[None, :], F32)
    cr, sr, cc, sc = jnp.cos(ar), jnp.sin(ar), jnp.cos(ac), jnp.sin(ac)
    cos_l = jnp.concatenate([cr, cr, cc, cc], axis=-1)
    sin_l = jnp.concatenate([-sr, sr, -sc, sc], axis=-1)
    cos_t = jnp.concatenate([jnp.ones((n_ctx, ATT_HD), F32), cos_l], axis=0)
    sin_t = jnp.concatenate([jnp.zeros((n_ctx, ATT_HD), F32), sin_l], axis=0)
    return cos_t, sin_t


def kernel(x, c, ctx, c_ctx, w_mod, b_mod, norm1, w_in, hg_lb_logits, hg_out_norm, att_q_norm, att_k_norm, lru_conv_w, lru_conv_b, lru_w_a, lru_b_a, lru_w_x, lru_b_x, lru_lambda, w_branch, w_out, norm2, w_ffn_in, w_ffn_out, norm_final):
    bsz, n_lat, d = x.shape
    n_ctx = ctx.shape[1]
    depth = w_in.shape[0]
    assert n_ctx % TM == 0 and n_lat % TM == 0 and n_lat % GRID_W == 0
    n_ctx_tiles = n_ctx // TM
    hg_w = HG_HEADS * HG_DK
    ffn_h = w_ffn_out.shape[1]

    c_all = jnp.concatenate([c, c_ctx[None, :]], axis=0)
    mods = _mod_vectors(c_all, w_mod, b_mod).reshape(depth, bsz + 1, N_MOD, 1, d)
    lb = _lower_bounds(hg_lb_logits)
    cos_t, sin_t = _rope_tables(n_ctx, n_lat)

    xs = jnp.concatenate([ctx, x], axis=1)
    for l in range(depth):
        last = l == depth - 1
        w_hg = w_in[l, :, :5 * hg_w].astype(BF16)
        w_rest = w_in[l, :, 5 * hg_w:].astype(BF16)
        hg, rest = _inproj(xs, mods[l], norm1[l][None, :], w_hg, w_rest,
                           lb[0, l][None, :], lb[1, l][None, :],
                           att_q_norm[l][None, :], att_k_norm[l][None, :], cos_t, sin_t, n_ctx_tiles)
        qs, hv, ff, fb, sg = hg
        aq, ak, av, lx, lg, mg = rest
        ya = _hgrn(qs, hv, ff, fb, sg, hg_out_norm[l][None, :], n_ctx)
        yb = _attention(aq, ak, av, n_ctx)
        yr = _rglru(lx, lg, lru_conv_w[l], lru_conv_b[l][None, :], lru_w_a[l].astype(BF16), lru_b_a[l],
                    lru_w_x[l].astype(BF16), lru_b_x[l], lru_lambda[l], n_ctx)
        xs = _merge_ffn(xs, ya, yb, yr, mg, mods[l], w_branch[l].astype(BF16), w_out[l].astype(BF16),
                        norm2[l][None, :], w_ffn_in[l, :, :ffn_h].astype(BF16),
                        w_ffn_in[l, :, ffn_h:].astype(BF16), w_ffn_out[l].astype(BF16),
                        n_ctx_tiles, norm_final[None, :] if last else None)
    return xs
```

```python
import functools

import numpy as np
import jax
import jax.numpy as jnp
from jax import lax
from jax.experimental import pallas as pl
from jax.experimental.pallas import tpu as pltpu

F32 = jnp.float32
BF16 = jnp.bfloat16

EPS = 1e-6
GRID_W = 64
HG_HEADS = 8
HG_DK = 128
ATT_HEADS = 8
ATT_KV_HEADS = 2
ATT_GROUP = ATT_HEADS // ATT_KV_HEADS
ATT_HD = 128
ROPE_THETA = 10000.0
LRU_BLOCKS = 4
LRU_CONV = 4
LRU_C = 8.0
N_BRANCH = 3
N_MOD = 6

V7X_VMEM_BYTES = 64 * 1024 * 1024
V7X_LANES = 128
V7X_SUBLANES = 8
VMEM_LIMIT = (V7X_VMEM_BYTES * 3) // 4

TM = 256
SEQ_CHUNK = 128
HG_CHUNK = SEQ_CHUNK
HG_HEADS_PER_STEP = 2


def _silu(x):
    return x * jax.nn.sigmoid(x)


def _gelu_tanh(x):
    return 0.5 * x * (1.0 + jnp.tanh(np.sqrt(2.0 / np.pi) * (x + 0.044715 * (x * x * x))))


def _dot(a, b):
    return jnp.dot(a, b, preferred_element_type=F32)


def _dot_nt(a, b):
    return lax.dot_general(a, b, (((1,), (1,)), ((), ())), preferred_element_type=F32)


def _dot_tn(a, b):
    return lax.dot_general(a, b, (((0,), (0,)), ((), ())), preferred_element_type=F32)


def _params(*sem):
    return pltpu.CompilerParams(dimension_semantics=sem, vmem_limit_bytes=VMEM_LIMIT)


def _mod_kernel(c_ref, w_ref, b_ref, o_ref):
    ca = _silu(c_ref[...]).astype(BF16)
    o_ref[...] = _dot(ca, w_ref[...].astype(BF16)) + b_ref[...]


def _mod_vectors(c_all, w_mod, b_mod):
    depth, d, n = w_mod.shape
    rows = c_all.shape[0]
    tn = n // 4
    return pl.pallas_call(
        _mod_kernel,
        grid=(depth, n // tn),
        in_specs=[pl.BlockSpec((rows, d), lambda l, j: (0, 0)),
                  pl.BlockSpec((None, d, tn), lambda l, j: (l, 0, j)),
                  pl.BlockSpec((None, 1, tn), lambda l, j: (l, 0, j))],
        out_specs=pl.BlockSpec((None, rows, tn), lambda l, j: (l, 0, j)),
        out_shape=jax.ShapeDtypeStruct((depth, rows, n), F32),
        compiler_params=_params("parallel", "parallel"),
        name="mod_vectors",
    )(c_all, w_mod, b_mod.reshape(depth, 1, n))


def _lb_kernel(x_ref, o_ref):
    depth = x_ref.shape[1]
    rows = [x_ref[:, l, :] for l in range(depth)]
    m = functools.reduce(jnp.maximum, rows)
    e = [jnp.exp(r - m) for r in rows]
    inv = 1.0 / functools.reduce(lambda a, b: a + b, e)
    acc = jnp.zeros_like(rows[0])
    o_ref[:, 0, :] = acc
    for l in range(1, depth):
        acc = acc + e[l] * inv
        o_ref[:, l, :] = acc


def _lower_bounds(logits):
    return pl.pallas_call(
        _lb_kernel,
        out_shape=jax.ShapeDtypeStruct(logits.shape, F32),
        name="hg_lower_bounds",
    )(logits.astype(F32))


def _norm_mod(x, g, shift, scale):
    ms = jnp.mean(x * x, axis=-1, keepdims=True)
    y = x * lax.rsqrt(ms + EPS) * g
    return y * (1.0 + scale) + shift


def _inproj_hg_kernel(x_ref, g_ref, sh_ref, sc_ref, w_ref, lbf_ref, lbb_ref,
                      qs_ref, v_ref, ff_ref, fb_ref, sg_ref):
    h = _norm_mod(x_ref[...], g_ref[...], sh_ref[...], sc_ref[...]).astype(BF16)
    w = qs_ref.shape[-1]
    z = _dot(h, w_ref[:, 0:w])
    qs_ref[...] = (_silu(z) * (HG_DK ** -0.5)).astype(BF16)
    v_ref[...] = _dot(h, w_ref[:, w:2 * w]).astype(BF16)
    for k, (lb_ref, f_ref) in enumerate(((lbf_ref, ff_ref), (lbb_ref, fb_ref))):
        lb = lb_ref[...]
        z = _dot(h, w_ref[:, (2 + k) * w:(3 + k) * w])
        f_ref[...] = lb + (1.0 - lb) * jax.nn.sigmoid(z)
    sg_ref[...] = _silu(_dot(h, w_ref[:, 4 * w:5 * w])).astype(BF16)


def _inproj_rest_kernel(x_ref, g_ref, sh_ref, sc_ref, w_ref, qg_ref, kg_ref, cos_ref, sin_ref,
                        q_ref, k_ref, v_ref, lx_ref, lg_ref, mg_ref):
    h = _norm_mod(x_ref[...], g_ref[...], sh_ref[...], sc_ref[...]).astype(BF16)
    cos = cos_ref[...]
    sin = sin_ref[...]
    lane = lax.broadcasted_iota(jnp.int32, cos.shape, 1)
    quarter = ATT_HD // 4
    first = (lane % (2 * quarter)) < quarter

    def norm_rope(z, gain, scale):
        ms = jnp.mean(z * z, axis=-1, keepdims=True)
        y = z * lax.rsqrt(ms + EPS) * gain
        swapped = jnp.where(first, pltpu.roll(y, ATT_HD - quarter, 1), pltpu.roll(y, quarter, 1))
        return ((y * cos + swapped * sin) * scale).astype(BF16)

    nq = q_ref.shape[-1]
    nkv = k_ref.shape[-1]
    nl = lx_ref.shape[-1]
    z = _dot(h, w_ref[:, 0:nq])
    for i in range(nq // ATT_HD):
        sl = slice(i * ATT_HD, (i + 1) * ATT_HD)
        q_ref[:, sl] = norm_rope(z[:, sl], qg_ref[...], ATT_HD ** -0.5)
    z = _dot(h, w_ref[:, nq:nq + nkv])
    for i in range(nkv // ATT_HD):
        sl = slice(i * ATT_HD, (i + 1) * ATT_HD)
        k_ref[:, sl] = norm_rope(z[:, sl], kg_ref[...], 1.0)
    off = nq + nkv
    v_ref[...] = _dot(h, w_ref[:, off:off + nkv]).astype(BF16)
    off += nkv
    lx_ref[...] = _dot(h, w_ref[:, off:off + nl])
    off += nl
    lg_ref[...] = _gelu_tanh(_dot(h, w_ref[:, off:off + nl])).astype(BF16)
    off += nl
    mg_ref[...] = jax.nn.sigmoid(_dot(h, w_ref[:, off:off + mg_ref.shape[-1]])).astype(BF16)


def _tile_spec(width, dtype_unused=None):
    return pl.BlockSpec((None, TM, width), lambda b, t: (b, t, 0))


def _vec_spec(width):
    return pl.BlockSpec((1, width), lambda b, t: (0, 0))


def _mod_spec(mods, which, n_ctx_tiles, toff=0):
    ctx_row = mods.shape[0] - 1
    d = mods.shape[-1]
    return pl.BlockSpec((None, None, 1, d),
                        lambda b, t: (jnp.where(t + toff < n_ctx_tiles, ctx_row, b), which, 0, 0))


def _inproj(x, mods, g1, w_hg, w_rest, lbf, lbb, qg, kg, cos_t, sin_t, n_ctx_tiles):
    bsz, t_all, d = x.shape
    grid = (bsz, t_all // TM)
    hw = w_hg.shape[1] // 5
    nq = ATT_HEADS * ATT_HD
    nkv = ATT_KV_HEADS * ATT_HD
    common = [_tile_spec(d), _vec_spec(d), _mod_spec(mods, 0, n_ctx_tiles),
              _mod_spec(mods, 1, n_ctx_tiles)]
    sds = lambda w, dt: jax.ShapeDtypeStruct((bsz, t_all, w), dt)
    hg = pl.pallas_call(
        _inproj_hg_kernel,
        grid=grid,
        in_specs=common + [pl.BlockSpec(w_hg.shape, lambda b, t: (0, 0)), _vec_spec(hw), _vec_spec(hw)],
        out_specs=[_tile_spec(hw)] * 5,
        out_shape=[sds(hw, BF16), sds(hw, BF16), sds(hw, F32), sds(hw, F32), sds(hw, BF16)],
        compiler_params=_params("parallel", "parallel"),
        name="inproj_hgrn",
    )(x, g1, mods, mods, w_hg, lbf, lbb)
    nl = hw
    ng = w_rest.shape[1] - nq - 2 * nkv - 2 * nl
    rest = pl.pallas_call(
        _inproj_rest_kernel,
        grid=grid,
        in_specs=common + [pl.BlockSpec(w_rest.shape, lambda b, t: (0, 0)),
                           _vec_spec(ATT_HD), _vec_spec(ATT_HD),
                           pl.BlockSpec((TM, ATT_HD), lambda b, t: (t, 0)),
                           pl.BlockSpec((TM, ATT_HD), lambda b, t: (t, 0))],
        out_specs=[_tile_spec(nq), _tile_spec(nkv), _tile_spec(nkv), _tile_spec(nl), _tile_spec(nl),
                   _tile_spec(ng)],
        out_shape=[sds(nq, BF16), sds(nkv, BF16), sds(nkv, BF16), sds(nl, F32), sds(nl, BF16),
                   sds(ng, BF16)],
        compiler_params=_params("parallel", "parallel"),
        name="inproj_rest",
    )(x, g1, mods, mods, w_rest, qg, kg, cos_t, sin_t)
    return hg, rest


def _chunk_token(r):
    nv = SEQ_CHUNK // V7X_SUBLANES
    return (r % V7X_SUBLANES) * nv + r // V7X_SUBLANES


def _to_chunk_layout(a, inverse=False):
    nv = SEQ_CHUNK // V7X_SUBLANES
    inner = (nv, V7X_SUBLANES) if inverse else (V7X_SUBLANES, nv)
    shp = a.shape
    a = a.reshape(shp[0], shp[1] // SEQ_CHUNK, *inner, *shp[2:])
    return jnp.swapaxes(a, 2, 3).reshape(shp)


def _split3(g):
    g1 = g.astype(BF16)
    r1 = g - g1.astype(F32)
    g2 = r1.astype(BF16)
    g3 = (r1 - g2.astype(F32)).astype(BF16)
    return jnp.concatenate([g1, g2, g3], axis=-1)


def _block_row(y, blk, ref_row):
    pos = lax.broadcasted_iota(jnp.int32, (V7X_SUBLANES, 1), 0) % blk
    s = 1
    while ref_row + s < blk:
        y = jnp.where((pos >= ref_row + s) & (pos < ref_row + 2 * s), pltpu.roll(y, s, 0), y)
        s *= 2
    s = 1
    while ref_row - s >= 0:
        y = jnp.where((pos <= ref_row - s) & (pos > ref_row - 2 * s),
                      pltpu.roll(y, V7X_SUBLANES - s, 0), y)
        s *= 2
    return y


def _level_exponent(b, blk, reverse):
    c, width = b.shape
    nv = c // V7X_SUBLANES
    half = blk // 2
    b3 = b.reshape(nv, V7X_SUBLANES, width)
    if blk <= nv:
        ref = half - 1 if reverse else half
        d = [b3[v0:v0 + blk] - b3[v0 + ref:v0 + ref + 1] for v0 in range(0, nv, blk)]
        d = d[0] if len(d) == 1 else jnp.concatenate(d, axis=0)
    else:
        sub_blk = blk // nv
        if reverse:
            ref = _block_row(b3[nv - 1], sub_blk, sub_blk // 2 - 1)
        else:
            ref = _block_row(b3[0], sub_blk, sub_blk // 2)
        d = b3 - ref[None]
    return d.reshape(c, width)


def _neg_abs(x):
    bits = lax.bitcast_convert_type(x, jnp.uint32) | jnp.uint32(0x80000000)
    return lax.bitcast_convert_type(bits, F32)


def _hg_chunks(probs, pair_xor):
    c, width = probs[0]["q"].shape
    heads = [slice(h * HG_DK, (h + 1) * HG_DK) for h in range(width // HG_DK)]
    for p in probs:
        p["kk"] = (1.0 - p["f"]).astype(BF16)
        b = p["b"]
        qe = p["q"] * jnp.exp2(b).astype(BF16)
        p["o"] = [_dot_nt(qe[:, h], s.astype(BF16)) for h, s in zip(heads, p["st"])]
        b_end = b[0:1, :] if p["reverse"] else b[c - 1:c, :]
        ke = p["kk"] * jnp.exp2(b_end - b).astype(BF16)
        st_scale = jnp.exp2(b_end)
        p["st"] = [s * st_scale[:, h] + _dot_tn(p["v"][:, h], ke[:, h]) for h, s in zip(heads, p["st"])]
        p["a"] = None

    blk = c
    while blk >= 2:
        half = blk // 2
        for p in probs:
            e = jnp.exp2(_neg_abs(_level_exponent(p["b"], blk, p["reverse"]))).astype(BF16)
            qw = p["q"] * e
            kw = p["kk"] * e
            a_lvl = [_dot_nt(qw[:, h], kw[:, h]) for h in heads]
            p["a"] = a_lvl if p["a"] is None else [jnp.where(pair_xor < blk, al, ah)
                                                   for al, ah in zip(a_lvl, p["a"])]
        blk = half
    for p in probs:
        pairs = []
        for h, ah in zip(heads, p["a"]):
            ah = jnp.where(pair_xor == 0, _dot_nt(p["q"][:, h], p["kk"][:, h]), ah)
            pairs.append(jnp.where(p["causal"], ah, 0.0).astype(BF16))
        p["a"] = pairs
    return ([jnp.concatenate(p["o"], axis=-1) for p in probs], [p["a"] for p in probs],
            [p["st"] for p in probs])


def _hg_running_decay(f, tri):
    width = f.shape[-1]
    parts = _dot(tri, _split3(jnp.log2(f)))
    return parts[:, 0:width] + parts[:, width:2 * width] + parts[:, 2 * width:3 * width]


def _hgrn_kernel(qs_ref, v_ref, ff_ref, fb_ref, sg_ref, gain_ref, y_ref, of_ref, ob_ref, *, n_ctx):
    t_all, width = qs_ref.shape
    kdim = HG_DK
    c = HG_CHUNK
    n_chunks = t_all // c
    n_ctx_chunks = n_ctx // c
    row = _chunk_token(lax.broadcasted_iota(jnp.int32, (c, c), 0))
    col = _chunk_token(lax.broadcasted_iota(jnp.int32, (c, c), 1))
    pair_xor = row ^ col
    tri_f = jnp.where(col <= row, 1.0, 0.0).astype(BF16)
    tri_b = jnp.where(col >= row, 1.0, 0.0).astype(BF16)

    n_heads = width // kdim
    f_refs = (ff_ref, fb_ref)
    o_refs = (of_ref, ob_ref)
    tris = (tri_f, tri_b)

    def chunk_ids(i):
        i = jnp.minimum(i, n_chunks - 1)
        ib = jnp.where(i < n_ctx_chunks, n_ctx_chunks - 1 - i, n_chunks - 1 - (i - n_ctx_chunks))
        return (i, ib)

    def rows_of(chunk):
        return pl.ds(pl.multiple_of(chunk * c, c), c)

    def finish(chunks, pairs):
        for o_ref, chunk, a in zip(o_refs, chunks, pairs):
            rows = rows_of(chunk)
            add = [_dot(a[h], v_ref[rows, h * kdim:(h + 1) * kdim]) for h in range(n_heads)]
            o_ref[rows, :] += jnp.concatenate(add, axis=-1)

    def body(i, carry):
        sts, decays, prev_chunks, prev_pairs = carry
        finish(prev_chunks, prev_pairs)
        chunks = chunk_ids(i)
        probs = [dict(q=qs_ref[rows_of(ch), :], v=v_ref[rows_of(ch), :], f=f_ref[rows_of(ch), :],
                      b=b, st=st, reverse=d == 1, causal=(col >= row) if d else (col <= row))
                 for d, (ch, f_ref, b, st) in enumerate(zip(chunks, f_refs, decays, sts))]
        outs, pairs, sts = _hg_chunks(probs, pair_xor)
        for o_ref, ch, o in zip(o_refs, chunks, outs):
            o_ref[rows_of(ch), :] = o
        decays = [_hg_running_decay(f_ref[rows_of(ch), :], tri)
                  for ch, f_ref, tri in zip(chunk_ids(i + 1), f_refs, tris)]
        return sts, decays, chunks, pairs

    first = chunk_ids(0)
    for o_ref, ch in zip(o_refs, first):
        o_ref[rows_of(ch), :] = jnp.zeros((c, width), F32)
    zero_st = [[jnp.zeros((kdim, kdim), F32)] * n_heads] * 2
    zero_pairs = [[jnp.zeros((c, c), BF16)] * n_heads] * 2
    decays0 = [_hg_running_decay(f_ref[rows_of(ch), :], tri)
               for ch, f_ref, tri in zip(first, f_refs, tris)]
    _, _, last_chunks, last_pairs = lax.fori_loop(
        0, n_chunks, body, (zero_st, decays0, first, zero_pairs))
    finish(last_chunks, last_pairs)

    def readout(i, carry):
        rows = pl.ds(pl.multiple_of(i * c, c), c)
        for h in range(width // kdim):
            sl = slice(h * kdim, (h + 1) * kdim)
            o = of_ref[rows, sl] + ob_ref[rows, sl]
            ms = jnp.mean(o * o, axis=-1, keepdims=True)
            y = o * lax.rsqrt(ms + EPS) * gain_ref[...]
            y_ref[rows, sl] = (y * sg_ref[rows, sl].astype(F32)).astype(BF16)
        return carry
    lax.fori_loop(0, n_chunks, readout, 0)


def _hgrn(qs, v, ff, fb, sg, gain, n_ctx):
    bsz, t_all, width = qs.shape
    bw = HG_HEADS_PER_STEP * HG_DK
    spec = pl.BlockSpec((None, t_all, bw), lambda b, h: (b, 0, h))
    return pl.pallas_call(
        functools.partial(_hgrn_kernel, n_ctx=n_ctx),
        grid=(bsz, width // bw),
        in_specs=[spec] * 5 + [pl.BlockSpec((1, HG_DK), lambda b, h: (0, 0))],
        out_specs=spec,
        out_shape=jax.ShapeDtypeStruct((bsz, t_all, width), BF16),
        scratch_shapes=[pltpu.VMEM((t_all, bw), F32), pltpu.VMEM((t_all, bw), F32)],
        compiler_params=_params("parallel", "parallel"),
        name="hgrn2_mixer",
    )(qs, v, ff, fb, sg, gain)


def _att_kernel(q_ref, k_ref, v_ref, o_ref, *, n_ctx):
    qi = pl.program_id(2)
    n_ctx_tiles = n_ctx // q_ref.shape[0]

    def run(nk):
        k = k_ref[0:nk, :]
        v = v_ref[0:nk, :]
        for g in range(ATT_GROUP):
            sl = slice(g * ATT_HD, (g + 1) * ATT_HD)
            s = _dot_nt(q_ref[:, sl], k)
            p = jnp.exp(s - jnp.max(s, axis=-1, keepdims=True))
            den = jnp.sum(p, axis=-1, keepdims=True)
            o_ref[:, sl] = (_dot(p.astype(BF16), v) / den).astype(BF16)

    @pl.when(qi < n_ctx_tiles)
    def _():
        run(n_ctx)

    @pl.when(qi >= n_ctx_tiles)
    def _():
        run(k_ref.shape[0])


def _attention(q, k, v, n_ctx):
    bsz, t_all, nq = q.shape
    gw = ATT_GROUP * ATT_HD
    return pl.pallas_call(
        functools.partial(_att_kernel, n_ctx=n_ctx),
        grid=(bsz, ATT_KV_HEADS, t_all // TM),
        in_specs=[pl.BlockSpec((None, TM, gw), lambda b, h, i: (b, i, h)),
                  pl.BlockSpec((None, t_all, ATT_HD), lambda b, h, i: (b, 0, h)),
                  pl.BlockSpec((None, t_all, ATT_HD), lambda b, h, i: (b, 0, h))],
        out_specs=pl.BlockSpec((None, TM, gw), lambda b, h, i: (b, i, h)),
        out_shape=jax.ShapeDtypeStruct((bsz, t_all, nq), BF16),
        compiler_params=_params("parallel", "parallel", "arbitrary"),
        name="gqa_mixer",
    )(q, k, v)


def _sigmoid(x):
    return 0.5 * jnp.tanh(0.5 * x) + 0.5


def _lru_kernel(x_ref, gl_ref, cw_ref, cb_ref, wa_ref, ba_ref, wx_ref, bx_ref, lam_ref, y_ref,
                af_ref, uf_ref, ab_ref, ub_ref, *, n_ctx):
    t_all, w = x_ref.shape
    c = SEQ_CHUNK
    ns = V7X_SUBLANES
    nv = c // ns
    n_chunks = t_all // c
    n_ctx_chunks = n_ctx // c
    sub = lax.broadcasted_iota(jnp.int32, (ns, 1), 0)

    def rows_of(chunk):
        return pl.ds(pl.multiple_of(chunk * c, c), c)

    def vreg_rows(chunk, v):
        return pl.ds(pl.multiple_of(chunk * c + v * ns, ns), ns)

    a_refs = (af_ref, ab_ref)
    u_refs = (uf_ref, ub_ref)
    sp = [jax.nn.softplus(-lam_ref[d:d + 1, :]) for d in range(2)]

    def one_later(cur, prev):
        return pltpu.roll(jnp.where(sub == ns - 1, prev, cur), 1, 0)

    def one_earlier(cur, nxt):
        return pltpu.roll(jnp.where(sub == 0, nxt, cur), ns - 1, 0)

    def gates(i, carry):
        x3 = x_ref[rows_of(i), :].reshape(nv, ns, w)
        has_prev = jnp.logical_and(i > 0, i != n_ctx_chunks)
        has_next = jnp.logical_and(i < n_chunks - 1, i != n_ctx_chunks - 1)
        ip = jnp.maximum(i - 1, 0)
        inx = jnp.minimum(i + 1, n_chunks - 1)
        prev_1 = jnp.where(has_prev, x_ref[vreg_rows(ip, nv - 1), :], 0.0)
        prev_2 = jnp.where(has_prev, x_ref[vreg_rows(ip, nv - 2), :], 0.0)
        next_0 = jnp.where(has_next, x_ref[vreg_rows(inx, 0), :], 0.0)
        ext = jnp.concatenate([one_later(x3[nv - 2], prev_2)[None], one_later(x3[nv - 1], prev_1)[None],
                               x3, one_earlier(x3[0], next_0)[None]], axis=0)
        xc = cb_ref[...] + ext[0:nv] * cw_ref[0:1, :]
        for j in range(1, LRU_CONV):
            xc = xc + ext[j:j + nv] * cw_ref[j:j + 1, :]
        xc = xc.reshape(c, w)
        xcb = xc.astype(BF16)
        for d in range(2):
            r = _sigmoid(_dot(xcb, wa_ref[d]) + ba_ref[d:d + 1, :])
            ig = _sigmoid(_dot(xcb, wx_ref[d]) + bx_ref[d:d + 1, :])
            log_a = (-LRU_C) * r * sp[d]
            a = jnp.exp(log_a)
            one_minus_a2 = jnp.tanh(-log_a) * (1.0 + a * a)
            a_refs[d][rows_of(i), :] = a
            u_refs[d][rows_of(i), :] = jnp.sqrt(one_minus_a2) * (ig * xc)
        return carry
    lax.fori_loop(0, n_chunks, gates, 0)

    def chunk_scan(a_ref, u_ref, chunk, h_in, reverse):
        a3 = a_ref[rows_of(chunk), :].reshape(nv, ns, w)
        u3 = u_ref[rows_of(chunk), :].reshape(nv, ns, w)
        order = list(range(nv - 1, -1, -1) if reverse else range(nv))
        h, p = u3[order[0]], a3[order[0]]
        hs, ps = {order[0]: h}, {order[0]: p}
        for v in order[1:]:
            h = a3[v] * h + u3[v]
            p = a3[v] * p
            hs[v], ps[v] = h, p
        s = 1
        while s < ns:
            sh = (ns - s) if reverse else s
            keep = (sub + s < ns) if reverse else (sub >= s)
            h = h + p * jnp.where(keep, pltpu.roll(h, sh, 0), 0.0)
            p = p * jnp.where(keep, pltpu.roll(p, sh, 0), 1.0)
            s *= 2
        end = h + p * h_in
        if reverse:
            enter = jnp.where(sub == ns - 1, h_in, pltpu.roll(end, ns - 1, 0))
            h_out = end[0:1, :]
        else:
            enter = jnp.where(sub == 0, h_in, pltpu.roll(end, 1, 0))
            h_out = end[ns - 1:ns, :]
        full = jnp.concatenate([(hs[v] + ps[v] * enter)[None] for v in range(nv)], axis=0)
        u_ref[rows_of(chunk), :] = full.reshape(c, w)
        return h_out

    def scan_body(i, carry):
        hf, hb = carry
        hf = chunk_scan(af_ref, uf_ref, i, hf, False)
        ib = jnp.where(i < n_ctx_chunks, n_ctx_chunks - 1 - i, n_chunks - 1 - (i - n_ctx_chunks))
        hb = chunk_scan(ab_ref, ub_ref, ib, hb, True)
        return hf, hb
    zero = jnp.zeros((1, w), F32)
    lax.fori_loop(0, n_chunks, scan_body, (zero, zero))

    def out(i, carry):
        rows = rows_of(i)
        y_ref[rows, :] = ((uf_ref[rows, :] + ub_ref[rows, :]) * gl_ref[rows, :].astype(F32)).astype(BF16)
        return carry
    lax.fori_loop(0, n_chunks, out, 0)


def _rglru(lx, lg, conv_w, conv_b, w_a, b_a, w_x, b_x, lam, n_ctx):
    bsz, t_all, width = lx.shape
    bw = width // LRU_BLOCKS
    seq = pl.BlockSpec((None, t_all, bw), lambda b, j: (b, 0, j))
    vec = lambda rows: pl.BlockSpec((rows, bw), lambda b, j: (0, j))
    wspec = pl.BlockSpec((2, None, bw, bw), lambda b, j: (0, j, 0, 0))
    return pl.pallas_call(
        functools.partial(_lru_kernel, n_ctx=n_ctx),
        grid=(bsz, LRU_BLOCKS),
        in_specs=[seq, seq, vec(LRU_CONV), vec(1), wspec, vec(2), wspec, vec(2), vec(2)],
        out_specs=seq,
        out_shape=jax.ShapeDtypeStruct((bsz, t_all, width), BF16),
        scratch_shapes=[pltpu.VMEM((t_all, bw), F32)] * 4,
        compiler_params=_params("parallel", "parallel"),
        name="rglru_mixer",
    )(lx, lg, conv_w, conv_b, w_a, b_a, w_x, b_x, lam)


def _merge_kernel(x_ref, a_ref, b_ref, r_ref, mg_ref, wbr_ref, wo_ref, gate_ref, o_ref):
    d = x_ref.shape[-1]
    y = None
    for i, br in enumerate((a_ref, b_ref, r_ref)):
        term = mg_ref[:, i * d:(i + 1) * d].astype(F32) * _dot(br[...], wbr_ref[i])
        y = term if y is None else y + term
    o_ref[...] = x_ref[...] + gate_ref[...] * _dot(y.astype(BF16), wo_ref[...])


def _ffn_kernel(x_ref, g_ref, sh_ref, sc_ref, gate_ref, wg_ref, wu_ref, wd_ref, *rest, final):
    x = x_ref[...]
    h = _norm_mod(x, g_ref[...], sh_ref[...], sc_ref[...]).astype(BF16)
    act = (_silu(_dot(h, wg_ref[...])) * _dot(h, wu_ref[...])).astype(BF16)
    y = x + gate_ref[...] * _dot(act, wd_ref[...])
    if final:
        gf_ref, o_ref = rest
        ms = jnp.mean(y * y, axis=-1, keepdims=True)
        y = y * lax.rsqrt(ms + EPS) * gf_ref[...]
    else:
        (o_ref,) = rest
    o_ref[...] = y


def _merge_ffn(x, ya, yb, yr, mg, mods, w_br, w_o, g2, w_g, w_u, w_d, n_ctx_tiles, g_final):
    bsz, t_all, d = x.shape
    final = g_final is not None
    toff = n_ctx_tiles if final else 0
    n_t = t_all // TM - toff
    tile = lambda w: pl.BlockSpec((None, TM, w), lambda b, t: (b, t + toff, 0))
    out_tile = pl.BlockSpec((None, TM, d), lambda b, t: (b, t, 0))
    mod = lambda which: _mod_spec(mods, which, n_ctx_tiles, toff)
    full = lambda a: pl.BlockSpec(a.shape, lambda b, t: (0,) * a.ndim)
    out_sds = jax.ShapeDtypeStruct((bsz, n_t * TM, d), F32)
    x1 = pl.pallas_call(
        _merge_kernel,
        grid=(bsz, n_t),
        in_specs=[tile(d), tile(d), tile(d), tile(d), tile(N_BRANCH * d), full(w_br), full(w_o), mod(2)],
        out_specs=out_tile,
        out_shape=out_sds,
        compiler_params=_params("parallel", "parallel"),
        name="merge",
    )(x, ya, yb, yr, mg, w_br, w_o, mods)
    x1_tile = pl.BlockSpec((None, TM, d), lambda b, t: (b, t, 0))
    args = [x1, g2, mods, mods, mods, w_g, w_u, w_d]
    specs = [x1_tile, _vec_spec(d), mod(3), mod(4), mod(5), full(w_g), full(w_u), full(w_d)]
    if final:
        args.append(g_final)
        specs.append(_vec_spec(d))
    return pl.pallas_call(
        functools.partial(_ffn_kernel, final=final),
        grid=(bsz, n_t),
        in_specs=specs,
        out_specs=out_tile,
        out_shape=out_sds,
        compiler_params=_params("parallel", "parallel"),
        name="ffn",
    )(*args)


def _rope_tables(n_ctx, n_lat):
    rows = n_lat // GRID_W
    r = np.repeat(np.arange(rows), GRID_W).astype(np.float32)
    col = np.tile(np.arange(GRID_W), rows).astype(np.float32)
    half = ATT_HD // 2
    inv = np.power(np.float32(ROPE_THETA), -np.arange(0, half, 2, dtype=np.float32) / np.float32(half))
    ar = jnp.asarray(r[:, None] * inv[None, :], F32)
    ac = jnp.asarray(col[:, None] * inv[None, :], F32)
    cr, sr, cc, sc = jnp.cos(ar), jnp.sin(ar), jnp.cos(ac), jnp.sin(ac)
    cos_l = jnp.concatenate([cr, cr, cc, cc], axis=-1)
    sin_l = jnp.concatenate([-sr, sr, -sc, sc], axis=-1)
    cos_t = jnp.concatenate([jnp.ones((n_ctx, ATT_HD), F32), cos_l], axis=0)
    sin_t = jnp.concatenate([jnp.zeros((n_ctx, ATT_HD), F32), sin_l], axis=0)
    return _to_chunk_layout(cos_t[None])[0], _to_chunk_layout(sin_t[None])[0]


def kernel(x, c, ctx, c_ctx, w_mod, b_mod, norm1, w_in, hg_lb_logits, hg_out_norm, att_q_norm, att_k_norm, lru_conv_w, lru_conv_b, lru_w_a, lru_b_a, lru_w_x, lru_b_x, lru_lambda, w_branch, w_out, norm2, w_ffn_in, w_ffn_out, norm_final):
    bsz, n_lat, d = x.shape
    n_ctx = ctx.shape[1]
    depth = w_in.shape[0]
    assert n_ctx % TM == 0 and n_lat % TM == 0 and n_lat % GRID_W == 0
    n_ctx_tiles = n_ctx // TM
    hg_w = HG_HEADS * HG_DK
    ffn_h = w_ffn_out.shape[1]

    c_all = jnp.concatenate([c, c_ctx[None, :]], axis=0)
    mods = _mod_vectors(c_all, w_mod, b_mod).reshape(depth, bsz + 1, N_MOD, 1, d)
    lb = _lower_bounds(hg_lb_logits)
    cos_t, sin_t = _rope_tables(n_ctx, n_lat)

    xs = _to_chunk_layout(jnp.concatenate([ctx, x], axis=1))
    for l in range(depth):
        last = l == depth - 1
        w_hg = w_in[l, :, :5 * hg_w].astype(BF16)
        w_rest = w_in[l, :, 5 * hg_w:].astype(BF16)
        hg, rest = _inproj(xs, mods[l], norm1[l][None, :], w_hg, w_rest,
                           lb[0, l][None, :], lb[1, l][None, :],
                           att_q_norm[l][None, :], att_k_norm[l][None, :], cos_t, sin_t, n_ctx_tiles)
        qs, hv, ff, fb, sg = hg
        aq, ak, av, lx, lg, mg = rest
        ya = _hgrn(qs, hv, ff, fb, sg, hg_out_norm[l][None, :], n_ctx)
        yb = _attention(aq, ak, av, n_ctx)
        yr = _rglru(lx, lg, lru_conv_w[l], lru_conv_b[l][None, :], lru_w_a[l].astype(BF16), lru_b_a[l],
                    lru_w_x[l].astype(BF16), lru_b_x[l], lru_lambda[l], n_ctx)
        xs = _merge_ffn(xs, ya, yb, yr, mg, mods[l], w_branch[l].astype(BF16), w_out[l].astype(BF16),
                        norm2[l][None, :], w_ffn_in[l, :, :ffn_h].astype(BF16),
                        w_ffn_in[l, :, ffn_h:].astype(BF16), w_ffn_out[l].astype(BF16),
                        n_ctx_tiles, norm_final[None, :] if last else None)
    return _to_chunk_layout(xs, inverse=True)
```

```python
import functools

import numpy as np
import jax
import jax.numpy as jnp
from jax import lax
from jax.experimental import pallas as pl
from jax.experimental.pallas import tpu as pltpu

F32 = jnp.float32
BF16 = jnp.bfloat16

EPS = 1e-6
GRID_W = 64
HG_HEADS = 8
HG_DK = 128
ATT_HEADS = 8
ATT_KV_HEADS = 2
ATT_GROUP = ATT_HEADS // ATT_KV_HEADS
ATT_HD = 128
ROPE_THETA = 10000.0
LRU_BLOCKS = 4
LRU_CONV = 4
LRU_C = 8.0
N_BRANCH = 3
N_MOD = 6

V7X_VMEM_BYTES = 64 * 1024 * 1024
V7X_LANES = 128
V7X_SUBLANES = 8
VMEM_LIMIT = (V7X_VMEM_BYTES * 3) // 4

TM = 256
SEQ_CHUNK = 128
HG_CHUNK = SEQ_CHUNK
HG_HEADS_PER_STEP = 2


def _silu(x):
    return x * jax.nn.sigmoid(x)


def _gelu_tanh(x):
    return 0.5 * x * (1.0 + jnp.tanh(np.sqrt(2.0 / np.pi) * (x + 0.044715 * (x * x * x))))


def _dot(a, b):
    return jnp.dot(a, b, preferred_element_type=F32)


def _dot_nt(a, b):
    return lax.dot_general(a, b, (((1,), (1,)), ((), ())), preferred_element_type=F32)


def _dot_tn(a, b):
    return lax.dot_general(a, b, (((0,), (0,)), ((), ())), preferred_element_type=F32)


def _params(*sem):
    return pltpu.CompilerParams(dimension_semantics=sem, vmem_limit_bytes=VMEM_LIMIT)


def _mod_kernel(c_ref, w_ref, b_ref, o_ref):
    ca = _silu(c_ref[...]).astype(BF16)
    o_ref[...] = _dot(ca, w_ref[...].astype(BF16)) + b_ref[...]


def _mod_vectors(c_all, w_mod, b_mod):
    depth, d, n = w_mod.shape
    rows = c_all.shape[0]
    tn = n // 4
    return pl.pallas_call(
        _mod_kernel,
        grid=(depth, n // tn),
        in_specs=[pl.BlockSpec((rows, d), lambda l, j: (0, 0)),
                  pl.BlockSpec((None, d, tn), lambda l, j: (l, 0, j)),
                  pl.BlockSpec((None, 1, tn), lambda l, j: (l, 0, j))],
        out_specs=pl.BlockSpec((None, rows, tn), lambda l, j: (l, 0, j)),
        out_shape=jax.ShapeDtypeStruct((depth, rows, n), F32),
        compiler_params=_params("parallel", "parallel"),
        name="mod_vectors",
    )(c_all, w_mod, b_mod.reshape(depth, 1, n))


def _lb_kernel(x_ref, o_ref):
    depth = x_ref.shape[1]
    rows = [x_ref[:, l, :] for l in range(depth)]
    m = functools.reduce(jnp.maximum, rows)
    e = [jnp.exp(r - m) for r in rows]
    inv = 1.0 / functools.reduce(lambda a, b: a + b, e)
    acc = jnp.zeros_like(rows[0])
    o_ref[:, 0, :] = acc
    for l in range(1, depth):
        acc = acc + e[l] * inv
        o_ref[:, l, :] = acc


def _lower_bounds(logits):
    return pl.pallas_call(
        _lb_kernel,
        out_shape=jax.ShapeDtypeStruct(logits.shape, F32),
        name="hg_lower_bounds",
    )(logits.astype(F32))


def _norm_mod(x, g, shift, scale):
    ms = jnp.mean(x * x, axis=-1, keepdims=True)
    y = x * lax.rsqrt(ms + EPS) * g
    return y * (1.0 + scale) + shift


def _inproj_hg_kernel(x_ref, g_ref, sh_ref, sc_ref, w_ref, lbf_ref, lbb_ref,
                      qs_ref, v_ref, ff_ref, fb_ref, sg_ref):
    h = _norm_mod(x_ref[...], g_ref[...], sh_ref[...], sc_ref[...]).astype(BF16)
    w = qs_ref.shape[-1]
    z = _dot(h, w_ref[:, 0:w])
    qs_ref[...] = (_silu(z) * (HG_DK ** -0.5)).astype(BF16)
    v_ref[...] = _dot(h, w_ref[:, w:2 * w]).astype(BF16)
    for k, (lb_ref, f_ref) in enumerate(((lbf_ref, ff_ref), (lbb_ref, fb_ref))):
        lb = lb_ref[...]
        z = _dot(h, w_ref[:, (2 + k) * w:(3 + k) * w])
        f_ref[...] = lb + (1.0 - lb) * jax.nn.sigmoid(z)
    sg_ref[...] = _silu(_dot(h, w_ref[:, 4 * w:5 * w])).astype(BF16)


def _inproj_rest_kernel(x_ref, g_ref, sh_ref, sc_ref, w_ref, qg_ref, kg_ref, cos_ref, sin_ref,
                        q_ref, k_ref, v_ref, lx_ref, lg_ref, mg_ref):
    h = _norm_mod(x_ref[...], g_ref[...], sh_ref[...], sc_ref[...]).astype(BF16)
    cos = cos_ref[...]
    sin = sin_ref[...]
    lane = lax.broadcasted_iota(jnp.int32, cos.shape, 1)
    quarter = ATT_HD // 4
    first = (lane % (2 * quarter)) < quarter

    def norm_rope(z, gain, scale):
        ms = jnp.mean(z * z, axis=-1, keepdims=True)
        y = z * lax.rsqrt(ms + EPS) * gain
        swapped = jnp.where(first, pltpu.roll(y, ATT_HD - quarter, 1), pltpu.roll(y, quarter, 1))
        return ((y * cos + swapped * sin) * scale).astype(BF16)

    nq = q_ref.shape[-1]
    nkv = k_ref.shape[-1]
    nl = lx_ref.shape[-1]
    z = _dot(h, w_ref[:, 0:nq])
    for i in range(nq // ATT_HD):
        sl = slice(i * ATT_HD, (i + 1) * ATT_HD)
        q_ref[:, sl] = norm_rope(z[:, sl], qg_ref[...], ATT_HD ** -0.5)
    z = _dot(h, w_ref[:, nq:nq + nkv])
    for i in range(nkv // ATT_HD):
        sl = slice(i * ATT_HD, (i + 1) * ATT_HD)
        k_ref[:, sl] = norm_rope(z[:, sl], kg_ref[...], 1.0)
    off = nq + nkv
    v_ref[...] = _dot(h, w_ref[:, off:off + nkv]).astype(BF16)
    off += nkv
    lx_ref[...] = _dot(h, w_ref[:, off:off + nl])
    off += nl
    lg_ref[...] = _gelu_tanh(_dot(h, w_ref[:, off:off + nl])).astype(BF16)
    off += nl
    mg_ref[...] = jax.nn.sigmoid(_dot(h, w_ref[:, off:off + mg_ref.shape[-1]])).astype(BF16)


def _tile_spec(width, dtype_unused=None):
    return pl.BlockSpec((None, TM, width), lambda b, t: (b, t, 0))


def _vec_spec(width):
    return pl.BlockSpec((1, width), lambda b, t: (0, 0))


def _mod_spec(mods, which, n_ctx_tiles, toff=0):
    ctx_row = mods.shape[0] - 1
    d = mods.shape[-1]
    return pl.BlockSpec((None, None, 1, d),
                        lambda b, t: (jnp.where(t + toff < n_ctx_tiles, ctx_row, b), which, 0, 0))


def _inproj(x, mods, g1, w_hg, w_rest, lbf, lbb, qg, kg, cos_t, sin_t, n_ctx_tiles):
    bsz, t_all, d = x.shape
    grid = (bsz, t_all // TM)
    hw = w_hg.shape[1] // 5
    nq = ATT_HEADS * ATT_HD
    nkv = ATT_KV_HEADS * ATT_HD
    common = [_tile_spec(d), _vec_spec(d), _mod_spec(mods, 0, n_ctx_tiles),
              _mod_spec(mods, 1, n_ctx_tiles)]
    sds = lambda w, dt: jax.ShapeDtypeStruct((bsz, t_all, w), dt)
    hg = pl.pallas_call(
        _inproj_hg_kernel,
        grid=grid,
        in_specs=common + [pl.BlockSpec(w_hg.shape, lambda b, t: (0, 0)), _vec_spec(hw), _vec_spec(hw)],
        out_specs=[_tile_spec(hw)] * 5,
        out_shape=[sds(hw, BF16), sds(hw, BF16), sds(hw, F32), sds(hw, F32), sds(hw, BF16)],
        compiler_params=_params("parallel", "parallel"),
        name="inproj_hgrn",
    )(x, g1, mods, mods, w_hg, lbf, lbb)
    nl = hw
    ng = w_rest.shape[1] - nq - 2 * nkv - 2 * nl
    rest = pl.pallas_call(
        _inproj_rest_kernel,
        grid=grid,
        in_specs=common + [pl.BlockSpec(w_rest.shape, lambda b, t: (0, 0)),
                           _vec_spec(ATT_HD), _vec_spec(ATT_HD),
                           pl.BlockSpec((TM, ATT_HD), lambda b, t: (t, 0)),
                           pl.BlockSpec((TM, ATT_HD), lambda b, t: (t, 0))],
        out_specs=[_tile_spec(nq), _tile_spec(nkv), _tile_spec(nkv), _tile_spec(nl), _tile_spec(nl),
                   _tile_spec(ng)],
        out_shape=[sds(nq, BF16), sds(nkv, BF16), sds(nkv, BF16), sds(nl, F32), sds(nl, BF16),
                   sds(ng, BF16)],
        compiler_params=_params("parallel", "parallel"),
        name="inproj_rest",
    )(x, g1, mods, mods, w_rest, qg, kg, cos_t, sin_t)
    return hg, rest


def _chunk_token(r):
    nv = SEQ_CHUNK // V7X_SUBLANES
    return (r % V7X_SUBLANES) * nv + r // V7X_SUBLANES


def _to_chunk_layout(a):
    nv = SEQ_CHUNK // V7X_SUBLANES
    shp = a.shape
    a = a.reshape(shp[0], shp[1] // SEQ_CHUNK, V7X_SUBLANES, nv, *shp[2:])
    return jnp.swapaxes(a, 2, 3).reshape(shp)


def _move_tokens(src_ref, dst_ref, to_chunk):
    for r in range(dst_ref.shape[0]):
        c0, i = r - r % SEQ_CHUNK, r % SEQ_CHUNK
        if to_chunk:
            dst_ref[r] = src_ref[c0 + _chunk_token(i)]
        else:
            dst_ref[c0 + _chunk_token(i)] = src_ref[r]


def _stream_in_kernel(ctx_ref, x_ref, o_ref, *, n_ctx_tiles):
    @pl.when(pl.program_id(1) < n_ctx_tiles)
    def _():
        _move_tokens(ctx_ref, o_ref, True)

    @pl.when(pl.program_id(1) >= n_ctx_tiles)
    def _():
        _move_tokens(x_ref, o_ref, True)


def _stream_out_kernel(x_ref, o_ref):
    _move_tokens(x_ref, o_ref, False)


def _token_tiles(a):
    bsz, n, d = a.shape
    return a.reshape(bsz, n, d // V7X_LANES, V7X_LANES)


def _stream_in(ctx, x):
    bsz, n_ctx, d = ctx.shape
    n_ctx_tiles = n_ctx // TM
    n_tiles = n_ctx_tiles + x.shape[1] // TM
    blk = (None, TM, d // V7X_LANES, V7X_LANES)
    out = pl.pallas_call(
        functools.partial(_stream_in_kernel, n_ctx_tiles=n_ctx_tiles),
        grid=(bsz, n_tiles),
        in_specs=[pl.BlockSpec(blk, lambda b, t: (b, jnp.minimum(t, n_ctx_tiles - 1), 0, 0)),
                  pl.BlockSpec(blk, lambda b, t: (b, jnp.maximum(t - n_ctx_tiles, 0), 0, 0))],
        out_specs=pl.BlockSpec(blk, lambda b, t: (b, t, 0, 0)),
        out_shape=jax.ShapeDtypeStruct((bsz, n_tiles * TM, d // V7X_LANES, V7X_LANES), x.dtype),
        compiler_params=_params("parallel", "arbitrary"),
        name="stream_in",
    )(_token_tiles(ctx), _token_tiles(x))
    return out.reshape(bsz, n_tiles * TM, d)


def _stream_out(xs):
    bsz, n, d = xs.shape
    blk = pl.BlockSpec((None, TM, d // V7X_LANES, V7X_LANES), lambda b, t: (b, t, 0, 0))
    out = pl.pallas_call(
        _stream_out_kernel,
        grid=(bsz, n // TM),
        in_specs=[blk],
        out_specs=blk,
        out_shape=jax.ShapeDtypeStruct((bsz, n, d // V7X_LANES, V7X_LANES), xs.dtype),
        compiler_params=_params("parallel", "parallel"),
        name="stream_out",
    )(_token_tiles(xs))
    return out.reshape(bsz, n, d)


def _split3(g):
    g1 = g.astype(BF16)
    r1 = g - g1.astype(F32)
    g2 = r1.astype(BF16)
    g3 = (r1 - g2.astype(F32)).astype(BF16)
    return jnp.concatenate([g1, g2, g3], axis=-1)


def _block_row(y, blk, ref_row):
    pos = lax.broadcasted_iota(jnp.int32, (V7X_SUBLANES, 1), 0) % blk
    s = 1
    while ref_row + s < blk:
        y = jnp.where((pos >= ref_row + s) & (pos < ref_row + 2 * s), pltpu.roll(y, s, 0), y)
        s *= 2
    s = 1
    while ref_row - s >= 0:
        y = jnp.where((pos <= ref_row - s) & (pos > ref_row - 2 * s),
                      pltpu.roll(y, V7X_SUBLANES - s, 0), y)
        s *= 2
    return y


def _level_exponent(b, blk, reverse):
    c, width = b.shape
    nv = c // V7X_SUBLANES
    half = blk // 2
    b3 = b.reshape(nv, V7X_SUBLANES, width)
    if blk <= nv:
        ref = half - 1 if reverse else half
        d = []
        for v0 in range(0, nv, blk):
            r = b3[v0 + ref:v0 + ref + 1]
            lo, hi = b3[v0:v0 + half], b3[v0 + half:v0 + blk]
            d += [lo - r, r - hi] if reverse else [r - lo, hi - r]
        d = jnp.concatenate(d, axis=0)
    else:
        sub_blk = blk // nv
        if reverse:
            ref = _block_row(b3[nv - 1], sub_blk, sub_blk // 2 - 1)
        else:
            ref = _block_row(b3[0], sub_blk, sub_blk // 2)
        d = _neg_abs(b3 - ref[None])
    return d.reshape(c, width)


def _neg_abs(x):
    bits = lax.bitcast_convert_type(x, jnp.uint32) | jnp.uint32(0x80000000)
    return lax.bitcast_convert_type(bits, F32)


def _hg_chunks(probs, pair_xor):
    c, width = probs[0]["q"].shape
    heads = [slice(h * HG_DK, (h + 1) * HG_DK) for h in range(width // HG_DK)]
    for p in probs:
        p["kk"] = (1.0 - p["f"]).astype(BF16)
        b = p["b"]
        qe = p["q"] * jnp.exp2(b).astype(BF16)
        p["o"] = [_dot_nt(qe[:, h], s.astype(BF16)) for h, s in zip(heads, p["st"])]
        b_end = b[0:1, :] if p["reverse"] else b[c - 1:c, :]
        ke = p["kk"] * jnp.exp2(b_end - b).astype(BF16)
        st_scale = jnp.exp2(b_end)
        p["st"] = [s * st_scale[:, h] + _dot_tn(p["v"][:, h], ke[:, h]) for h, s in zip(heads, p["st"])]
        p["a"] = None

    blk = c
    while blk >= 2:
        half = blk // 2
        for p in probs:
            e = jnp.exp2(_level_exponent(p["b"], blk, p["reverse"])).astype(BF16)
            qw = p["q"] * e
            kw = p["kk"] * e
            a_lvl = [_dot_nt(qw[:, h], kw[:, h]) for h in heads]
            p["a"] = a_lvl if p["a"] is None else [jnp.where(pair_xor < blk, al, ah)
                                                   for al, ah in zip(a_lvl, p["a"])]
        blk = half
    for p in probs:
        pairs = []
        for h, ah in zip(heads, p["a"]):
            ah = jnp.where(pair_xor == 0, _dot_nt(p["q"][:, h], p["kk"][:, h]), ah)
            pairs.append(jnp.where(p["causal"], ah, 0.0).astype(BF16))
        p["a"] = pairs
    return ([jnp.concatenate(p["o"], axis=-1) for p in probs], [p["a"] for p in probs],
            [p["st"] for p in probs])


def _hg_running_decay(f, tri):
    width = f.shape[-1]
    parts = _dot(tri, _split3(jnp.log2(f)))
    return parts[:, 0:width] + parts[:, width:2 * width] + parts[:, 2 * width:3 * width]


def _hgrn_kernel(qs_ref, v_ref, ff_ref, fb_ref, sg_ref, gain_ref, y_ref, of_ref, ob_ref, *, n_ctx):
    t_all, width = qs_ref.shape
    kdim = HG_DK
    c = HG_CHUNK
    n_chunks = t_all // c
    n_ctx_chunks = n_ctx // c
    row = _chunk_token(lax.broadcasted_iota(jnp.int32, (c, c), 0))
    col = _chunk_token(lax.broadcasted_iota(jnp.int32, (c, c), 1))
    pair_xor = row ^ col
    tri_f = jnp.where(col <= row, 1.0, 0.0).astype(BF16)
    tri_b = jnp.where(col >= row, 1.0, 0.0).astype(BF16)

    n_heads = width // kdim
    f_refs = (ff_ref, fb_ref)
    o_refs = (of_ref, ob_ref)
    tris = (tri_f, tri_b)

    def chunk_ids(i):
        i = jnp.minimum(i, n_chunks - 1)
        ib = jnp.where(i < n_ctx_chunks, n_ctx_chunks - 1 - i, n_chunks - 1 - (i - n_ctx_chunks))
        return (i, ib)

    def rows_of(chunk):
        return pl.ds(pl.multiple_of(chunk * c, c), c)

    def finish(chunks, pairs):
        for o_ref, chunk, a in zip(o_refs, chunks, pairs):
            rows = rows_of(chunk)
            add = [_dot(a[h], v_ref[rows, h * kdim:(h + 1) * kdim]) for h in range(n_heads)]
            o_ref[rows, :] += jnp.concatenate(add, axis=-1)

    def body(i, carry):
        sts, decays, prev_chunks, prev_pairs = carry
        finish(prev_chunks, prev_pairs)
        chunks = chunk_ids(i)
        probs = [dict(q=qs_ref[rows_of(ch), :], v=v_ref[rows_of(ch), :], f=f_ref[rows_of(ch), :],
                      b=b, st=st, reverse=d == 1, causal=(col >= row) if d else (col <= row))
                 for d, (ch, f_ref, b, st) in enumerate(zip(chunks, f_refs, decays, sts))]
        outs, pairs, sts = _hg_chunks(probs, pair_xor)
        for o_ref, ch, o in zip(o_refs, chunks, outs):
            o_ref[rows_of(ch), :] = o
        decays = [_hg_running_decay(f_ref[rows_of(ch), :], tri)
                  for ch, f_ref, tri in zip(chunk_ids(i + 1), f_refs, tris)]
        return sts, decays, chunks, pairs

    first = chunk_ids(0)
    for o_ref, ch in zip(o_refs, first):
        o_ref[rows_of(ch), :] = jnp.zeros((c, width), F32)
    zero_st = [[jnp.zeros((kdim, kdim), F32)] * n_heads] * 2
    zero_pairs = [[jnp.zeros((c, c), BF16)] * n_heads] * 2
    decays0 = [_hg_running_decay(f_ref[rows_of(ch), :], tri)
               for ch, f_ref, tri in zip(first, f_refs, tris)]
    _, _, last_chunks, last_pairs = lax.fori_loop(
        0, n_chunks, body, (zero_st, decays0, first, zero_pairs))
    finish(last_chunks, last_pairs)

    def readout(i, carry):
        rows = pl.ds(pl.multiple_of(i * c, c), c)
        for h in range(width // kdim):
            sl = slice(h * kdim, (h + 1) * kdim)
            o = of_ref[rows, sl] + ob_ref[rows, sl]
            ms = jnp.mean(o * o, axis=-1, keepdims=True)
            y = o * lax.rsqrt(ms + EPS) * gain_ref[...]
            y_ref[rows, sl] = (y * sg_ref[rows, sl].astype(F32)).astype(BF16)
        return carry
    lax.fori_loop(0, n_chunks, readout, 0, unroll=2)


def _hgrn(qs, v, ff, fb, sg, gain, n_ctx):
    bsz, t_all, width = qs.shape
    bw = HG_HEADS_PER_STEP * HG_DK
    spec = pl.BlockSpec((None, t_all, bw), lambda b, h: (b, 0, h))
    return pl.pallas_call(
        functools.partial(_hgrn_kernel, n_ctx=n_ctx),
        grid=(bsz, width // bw),
        in_specs=[spec] * 5 + [pl.BlockSpec((1, HG_DK), lambda b, h: (0, 0))],
        out_specs=spec,
        out_shape=jax.ShapeDtypeStruct((bsz, t_all, width), BF16),
        scratch_shapes=[pltpu.VMEM((t_all, bw), F32), pltpu.VMEM((t_all, bw), F32)],
        compiler_params=_params("parallel", "parallel"),
        name="hgrn2_mixer",
    )(qs, v, ff, fb, sg, gain)


def _att_kernel(q_ref, k_ref, v_ref, o_ref, *, n_ctx):
    qi = pl.program_id(2)
    n_ctx_tiles = n_ctx // q_ref.shape[0]

    def run(nk):
        k = k_ref[0:nk, :]
        v = v_ref[0:nk, :]
        for g in range(ATT_GROUP):
            sl = slice(g * ATT_HD, (g + 1) * ATT_HD)
            s = _dot_nt(q_ref[:, sl], k)
            p = jnp.exp(s - jnp.max(s, axis=-1, keepdims=True))
            den = jnp.sum(p, axis=-1, keepdims=True)
            o_ref[:, sl] = (_dot(p.astype(BF16), v) / den).astype(BF16)

    @pl.when(qi < n_ctx_tiles)
    def _():
        run(n_ctx)

    @pl.when(qi >= n_ctx_tiles)
    def _():
        run(k_ref.shape[0])


def _attention(q, k, v, n_ctx):
    bsz, t_all, nq = q.shape
    gw = ATT_GROUP * ATT_HD
    return pl.pallas_call(
        functools.partial(_att_kernel, n_ctx=n_ctx),
        grid=(bsz, ATT_KV_HEADS, t_all // TM),
        in_specs=[pl.BlockSpec((None, TM, gw), lambda b, h, i: (b, i, h)),
                  pl.BlockSpec((None, t_all, ATT_HD), lambda b, h, i: (b, 0, h)),
                  pl.BlockSpec((None, t_all, ATT_HD), lambda b, h, i: (b, 0, h))],
        out_specs=pl.BlockSpec((None, TM, gw), lambda b, h, i: (b, i, h)),
        out_shape=jax.ShapeDtypeStruct((bsz, t_all, nq), BF16),
        compiler_params=_params("parallel", "parallel", "arbitrary"),
        name="gqa_mixer",
    )(q, k, v)


def _lru_kernel(x_ref, gl_ref, cw_ref, cb_ref, wa_ref, ba_ref, wx_ref, bx_ref, lam_ref, y_ref,
                af_ref, uf_ref, ab_ref, ub_ref, *, n_ctx):
    t_all, w = x_ref.shape
    c = SEQ_CHUNK
    ns = V7X_SUBLANES
    nv = c // ns
    n_chunks = t_all // c
    n_ctx_chunks = n_ctx // c
    sub = lax.broadcasted_iota(jnp.int32, (ns, 1), 0)

    def rows_of(chunk):
        return pl.ds(pl.multiple_of(chunk * c, c), c)

    def vreg_rows(chunk, v):
        return pl.ds(pl.multiple_of(chunk * c + v * ns, ns), ns)

    a_refs = (af_ref, ab_ref)
    u_refs = (uf_ref, ub_ref)
    rate = [(-0.5 * LRU_C) * jax.nn.softplus(-lam_ref[d:d + 1, :]) for d in range(2)]
    half_ba = [0.5 * ba_ref[d:d + 1, :] for d in range(2)]
    half_bx = [0.5 * bx_ref[d:d + 1, :] for d in range(2)]

    def one_later(cur, prev):
        return pltpu.roll(jnp.where(sub == ns - 1, prev, cur), 1, 0)

    def one_earlier(cur, nxt):
        return pltpu.roll(jnp.where(sub == 0, nxt, cur), ns - 1, 0)

    def gates(i, carry):
        x3 = x_ref[rows_of(i), :].reshape(nv, ns, w)
        has_prev = jnp.logical_and(i > 0, i != n_ctx_chunks)
        has_next = jnp.logical_and(i < n_chunks - 1, i != n_ctx_chunks - 1)
        ip = jnp.maximum(i - 1, 0)
        inx = jnp.minimum(i + 1, n_chunks - 1)
        prev_1 = jnp.where(has_prev, x_ref[vreg_rows(ip, nv - 1), :], 0.0)
        prev_2 = jnp.where(has_prev, x_ref[vreg_rows(ip, nv - 2), :], 0.0)
        next_0 = jnp.where(has_next, x_ref[vreg_rows(inx, 0), :], 0.0)
        ext = jnp.concatenate([one_later(x3[nv - 2], prev_2)[None], one_later(x3[nv - 1], prev_1)[None],
                               x3, one_earlier(x3[0], next_0)[None]], axis=0)
        xc = cb_ref[...] + ext[0:nv] * cw_ref[0:1, :]
        for j in range(1, LRU_CONV):
            xc = xc + ext[j:j + nv] * cw_ref[j:j + 1, :]
        xc = xc.reshape(c, w)
        xcb = xc.astype(BF16)
        half_xc = 0.5 * xc
        for d in range(2):
            th_r = jnp.tanh(_dot(xcb, wa_ref[d]) + half_ba[d])
            th_i = jnp.tanh(_dot(xcb, wx_ref[d]) + half_bx[d])
            log_a = rate[d] * (th_r + 1.0)
            a = jnp.exp(log_a)
            one_minus_a2 = jnp.tanh(log_a) * (-1.0 - a * a)
            a_refs[d][rows_of(i), :] = a
            u_refs[d][rows_of(i), :] = jnp.sqrt(one_minus_a2) * ((th_i + 1.0) * half_xc)
        return carry
    lax.fori_loop(0, n_chunks, gates, 0, unroll=2)

    def chunk_scan(a_ref, u_ref, chunk, h_in, reverse):
        a3 = a_ref[rows_of(chunk), :].reshape(nv, ns, w)
        u3 = u_ref[rows_of(chunk), :].reshape(nv, ns, w)
        order = list(range(nv - 1, -1, -1) if reverse else range(nv))
        h, p = u3[order[0]], a3[order[0]]
        hs, ps = {order[0]: h}, {order[0]: p}
        for v in order[1:]:
            h = a3[v] * h + u3[v]
            p = a3[v] * p
            hs[v], ps[v] = h, p
        s = 1
        while s < ns:
            sh = (ns - s) if reverse else s
            keep = (sub + s < ns) if reverse else (sub >= s)
            h = h + p * jnp.where(keep, pltpu.roll(h, sh, 0), 0.0)
            p = p * jnp.where(keep, pltpu.roll(p, sh, 0), 1.0)
            s *= 2
        end = h + p * h_in
        if reverse:
            enter = jnp.where(sub == ns - 1, h_in, pltpu.roll(end, ns - 1, 0))
            h_out = end[0:1, :]
        else:
            enter = jnp.where(sub == 0, h_in, pltpu.roll(end, 1, 0))
            h_out = end[ns - 1:ns, :]
        full = jnp.concatenate([(hs[v] + ps[v] * enter)[None] for v in range(nv)], axis=0)
        u_ref[rows_of(chunk), :] = full.reshape(c, w)
        return h_out

    def scan_body(i, carry):
        hf, hb = carry
        hf = chunk_scan(af_ref, uf_ref, i, hf, False)
        ib = jnp.where(i < n_ctx_chunks, n_ctx_chunks - 1 - i, n_chunks - 1 - (i - n_ctx_chunks))
        hb = chunk_scan(ab_ref, ub_ref, ib, hb, True)
        return hf, hb
    zero = jnp.zeros((1, w), F32)
    lax.fori_loop(0, n_chunks, scan_body, (zero, zero))

    def out(i, carry):
        rows = rows_of(i)
        y_ref[rows, :] = ((uf_ref[rows, :] + ub_ref[rows, :]) * gl_ref[rows, :].astype(F32)).astype(BF16)
        return carry
    lax.fori_loop(0, n_chunks, out, 0)


def _rglru(lx, lg, conv_w, conv_b, w_a, b_a, w_x, b_x, lam, n_ctx):
    bsz, t_all, width = lx.shape
    bw = width // LRU_BLOCKS
    seq = pl.BlockSpec((None, t_all, bw), lambda b, j: (b, 0, j))
    vec = lambda rows: pl.BlockSpec((rows, bw), lambda b, j: (0, j))
    wspec = pl.BlockSpec((2, None, bw, bw), lambda b, j: (0, j, 0, 0))
    return pl.pallas_call(
        functools.partial(_lru_kernel, n_ctx=n_ctx),
        grid=(bsz, LRU_BLOCKS),
        in_specs=[seq, seq, vec(LRU_CONV), vec(1), wspec, vec(2), wspec, vec(2), vec(2)],
        out_specs=seq,
        out_shape=jax.ShapeDtypeStruct((bsz, t_all, width), BF16),
        scratch_shapes=[pltpu.VMEM((t_all, bw), F32)] * 4,
        compiler_params=_params("parallel", "parallel"),
        name="rglru_mixer",
    )(lx, lg, conv_w, conv_b, w_a, b_a, w_x, b_x, lam)


def _merge_kernel(x_ref, a_ref, b_ref, r_ref, mg_ref, wbr_ref, wo_ref, gate_ref, o_ref):
    d = x_ref.shape[-1]
    y = None
    for i, br in enumerate((a_ref, b_ref, r_ref)):
        term = mg_ref[:, i * d:(i + 1) * d].astype(F32) * _dot(br[...], wbr_ref[i])
        y = term if y is None else y + term
    o_ref[...] = x_ref[...] + gate_ref[...] * _dot(y.astype(BF16), wo_ref[...])


def _ffn_kernel(x_ref, g_ref, sh_ref, sc_ref, gate_ref, wg_ref, wu_ref, wd_ref, *rest, final):
    x = x_ref[...]
    h = _norm_mod(x, g_ref[...], sh_ref[...], sc_ref[...]).astype(BF16)
    act = (_silu(_dot(h, wg_ref[...])) * _dot(h, wu_ref[...])).astype(BF16)
    y = x + gate_ref[...] * _dot(act, wd_ref[...])
    if final:
        gf_ref, o_ref = rest
        ms = jnp.mean(y * y, axis=-1, keepdims=True)
        y = y * lax.rsqrt(ms + EPS) * gf_ref[...]
    else:
        (o_ref,) = rest
    o_ref[...] = y


def _merge_ffn(x, ya, yb, yr, mg, mods, w_br, w_o, g2, w_g, w_u, w_d, n_ctx_tiles, g_final):
    bsz, t_all, d = x.shape
    final = g_final is not None
    toff = n_ctx_tiles if final else 0
    n_t = t_all // TM - toff
    tile = lambda w: pl.BlockSpec((None, TM, w), lambda b, t: (b, t + toff, 0))
    out_tile = pl.BlockSpec((None, TM, d), lambda b, t: (b, t, 0))
    mod = lambda which: _mod_spec(mods, which, n_ctx_tiles, toff)
    full = lambda a: pl.BlockSpec(a.shape, lambda b, t: (0,) * a.ndim)
    out_sds = jax.ShapeDtypeStruct((bsz, n_t * TM, d), F32)
    x1 = pl.pallas_call(
        _merge_kernel,
        grid=(bsz, n_t),
        in_specs=[tile(d), tile(d), tile(d), tile(d), tile(N_BRANCH * d), full(w_br), full(w_o), mod(2)],
        out_specs=out_tile,
        out_shape=out_sds,
        compiler_params=_params("parallel", "parallel"),
        name="merge",
    )(x, ya, yb, yr, mg, w_br, w_o, mods)
    x1_tile = pl.BlockSpec((None, TM, d), lambda b, t: (b, t, 0))
    args = [x1, g2, mods, mods, mods, w_g, w_u, w_d]
    specs = [x1_tile, _vec_spec(d), mod(3), mod(4), mod(5), full(w_g), full(w_u), full(w_d)]
    if final:
        args.append(g_final)
        specs.append(_vec_spec(d))
    return pl.pallas_call(
        functools.partial(_ffn_kernel, final=final),
        grid=(bsz, n_t),
        in_specs=specs,
        out_specs=out_tile,
        out_shape=out_sds,
        compiler_params=_params("parallel", "parallel"),
        name="ffn",
    )(*args)


def _rope_tables(n_ctx, n_lat):
    rows = n_lat // GRID_W
    r = np.repeat(np.arange(rows), GRID_W).astype(np.float32)
    col = np.tile(np.arange(GRID_W), rows).astype(np.float32)
    half = ATT_HD // 2
    inv = np.power(np.float32(ROPE_THETA), -np.arange(0, half, 2, dtype=np.float32) / np.float32(half))
    ar = jnp.asarray(r[:, None] * inv[None, :], F32)
    ac = jnp.asarray(col[:, None] * inv[None, :], F32)
    cr, sr, cc, sc = jnp.cos(ar), jnp.sin(ar), jnp.cos(ac), jnp.sin(ac)
    cos_l = jnp.concatenate([cr, cr, cc, cc], axis=-1)
    sin_l = jnp.concatenate([-sr, sr, -sc, sc], axis=-1)
    cos_t = jnp.concatenate([jnp.ones((n_ctx, ATT_HD), F32), cos_l], axis=0)
    sin_t = jnp.concatenate([jnp.zeros((n_ctx, ATT_HD), F32), sin_l], axis=0)
    return _to_chunk_layout(cos_t[None])[0], _to_chunk_layout(sin_t[None])[0]


def kernel(x, c, ctx, c_ctx, w_mod, b_mod, norm1, w_in, hg_lb_logits, hg_out_norm, att_q_norm, att_k_norm, lru_conv_w, lru_conv_b, lru_w_a, lru_b_a, lru_w_x, lru_b_x, lru_lambda, w_branch, w_out, norm2, w_ffn_in, w_ffn_out, norm_final):
    bsz, n_lat, d = x.shape
    n_ctx = ctx.shape[1]
    depth = w_in.shape[0]
    assert n_ctx % TM == 0 and n_lat % TM == 0 and n_lat % GRID_W == 0
    n_ctx_tiles = n_ctx // TM
    hg_w = HG_HEADS * HG_DK
    ffn_h = w_ffn_out.shape[1]

    c_all = jnp.concatenate([c, c_ctx[None, :]], axis=0)
    mods = _mod_vectors(c_all, w_mod, b_mod).reshape(depth, bsz + 1, N_MOD, 1, d)
    lb = _lower_bounds(hg_lb_logits)
    cos_t, sin_t = _rope_tables(n_ctx, n_lat)

    xs = _stream_in(ctx, x)
    for l in range(depth):
        last = l == depth - 1
        w_hg = w_in[l, :, :5 * hg_w].astype(BF16)
        w_rest = w_in[l, :, 5 * hg_w:].astype(BF16)
        hg, rest = _inproj(xs, mods[l], norm1[l][None, :], w_hg, w_rest,
                           lb[0, l][None, :], lb[1, l][None, :],
                           att_q_norm[l][None, :], att_k_norm[l][None, :], cos_t, sin_t, n_ctx_tiles)
        qs, hv, ff, fb, sg = hg
        aq, ak, av, lx, lg, mg = rest
        ya = _hgrn(qs, hv, ff, fb, sg, hg_out_norm[l][None, :], n_ctx)
        yb = _attention(aq, ak, av, n_ctx)
        yr = _rglru(lx, lg, lru_conv_w[l], lru_conv_b[l][None, :], (0.5 * lru_w_a[l]).astype(BF16),
                    lru_b_a[l], (0.5 * lru_w_x[l]).astype(BF16), lru_b_x[l], lru_lambda[l], n_ctx)
        xs = _merge_ffn(xs, ya, yb, yr, mg, mods[l], w_branch[l].astype(BF16), w_out[l].astype(BF16),
                        norm2[l][None, :], w_ffn_in[l, :, :ffn_h].astype(BF16),
                        w_ffn_in[l, :, ffn_h:].astype(BF16), w_ffn_out[l].astype(BF16),
                        n_ctx_tiles, norm_final[None, :] if last else None)
    return _stream_out(xs)
```

```python
import functools

import numpy as np
import jax
import jax.numpy as jnp
from jax import lax
from jax.experimental import pallas as pl
from jax.experimental.pallas import tpu as pltpu

F32 = jnp.float32
BF16 = jnp.bfloat16

EPS = 1e-6
GRID_W = 64
HG_HEADS = 8
HG_DK = 128
ATT_HEADS = 8
ATT_KV_HEADS = 2
ATT_GROUP = ATT_HEADS // ATT_KV_HEADS
ATT_HD = 128
ROPE_THETA = 10000.0
LRU_BLOCKS = 4
LRU_CONV = 4
LRU_C = 8.0
N_BRANCH = 3
N_MOD = 6

V7X_VMEM_BYTES = 64 * 1024 * 1024
V7X_LANES = 128
V7X_SUBLANES = 8
VMEM_LIMIT = (V7X_VMEM_BYTES * 3) // 4

TM = 256
SEQ_CHUNK = 128
HG_CHUNK = SEQ_CHUNK
HG_HEADS_PER_STEP = 2


def _silu(x):
    return x * jax.nn.sigmoid(x)


def _gelu_tanh(x):
    return 0.5 * x * (1.0 + jnp.tanh(np.sqrt(2.0 / np.pi) * (x + 0.044715 * (x * x * x))))


def _dot(a, b):
    return jnp.dot(a, b, preferred_element_type=F32)


def _dot_nt(a, b):
    return lax.dot_general(a, b, (((1,), (1,)), ((), ())), preferred_element_type=F32)


def _dot_tn(a, b):
    return lax.dot_general(a, b, (((0,), (0,)), ((), ())), preferred_element_type=F32)


def _params(*sem):
    return pltpu.CompilerParams(dimension_semantics=sem, vmem_limit_bytes=VMEM_LIMIT)


def _mod_kernel(c_ref, w_ref, b_ref, o_ref):
    ca = _silu(c_ref[...]).astype(BF16)
    o_ref[...] = _dot(ca, w_ref[...].astype(BF16)) + b_ref[...]


def _mod_vectors(c_all, w_mod, b_mod):
    depth, d, n = w_mod.shape
    rows = c_all.shape[0]
    tn = n // 4
    return pl.pallas_call(
        _mod_kernel,
        grid=(depth, n // tn),
        in_specs=[pl.BlockSpec((rows, d), lambda l, j: (0, 0)),
                  pl.BlockSpec((None, d, tn), lambda l, j: (l, 0, j)),
                  pl.BlockSpec((None, 1, tn), lambda l, j: (l, 0, j))],
        out_specs=pl.BlockSpec((None, rows, tn), lambda l, j: (l, 0, j)),
        out_shape=jax.ShapeDtypeStruct((depth, rows, n), F32),
        compiler_params=_params("parallel", "parallel"),
        name="mod_vectors",
    )(c_all, w_mod, b_mod.reshape(depth, 1, n))


def _lb_kernel(x_ref, o_ref):
    depth = x_ref.shape[1]
    rows = [x_ref[:, l, :] for l in range(depth)]
    m = functools.reduce(jnp.maximum, rows)
    e = [jnp.exp(r - m) for r in rows]
    inv = 1.0 / functools.reduce(lambda a, b: a + b, e)
    acc = jnp.zeros_like(rows[0])
    o_ref[:, 0, :] = acc
    for l in range(1, depth):
        acc = acc + e[l] * inv
        o_ref[:, l, :] = acc


def _lower_bounds(logits):
    return pl.pallas_call(
        _lb_kernel,
        out_shape=jax.ShapeDtypeStruct(logits.shape, F32),
        name="hg_lower_bounds",
    )(logits.astype(F32))


def _norm_mod(x, g, shift, scale):
    ms = jnp.mean(x * x, axis=-1, keepdims=True)
    y = x * lax.rsqrt(ms + EPS) * g
    return y * (1.0 + scale) + shift


def _inproj_hg_kernel(x_ref, g_ref, sh_ref, sc_ref, w_ref, lbf_ref, lbb_ref,
                      qs_ref, v_ref, ff_ref, fb_ref, sg_ref):
    h = _norm_mod(x_ref[...], g_ref[...], sh_ref[...], sc_ref[...]).astype(BF16)
    w = qs_ref.shape[-1]
    z = _dot(h, w_ref[:, 0:w])
    qs_ref[...] = (_silu(z) * (HG_DK ** -0.5)).astype(BF16)
    v_ref[...] = _dot(h, w_ref[:, w:2 * w]).astype(BF16)
    for k, (lb_ref, f_ref) in enumerate(((lbf_ref, ff_ref), (lbb_ref, fb_ref))):
        lb = lb_ref[...]
        z = _dot(h, w_ref[:, (2 + k) * w:(3 + k) * w])
        f_ref[...] = lb + (1.0 - lb) * jax.nn.sigmoid(z)
    sg_ref[...] = _silu(_dot(h, w_ref[:, 4 * w:5 * w])).astype(BF16)


def _inproj_rest_kernel(x_ref, g_ref, sh_ref, sc_ref, w_ref, qg_ref, kg_ref, cos_ref, sin_ref,
                        q_ref, k_ref, v_ref, lx_ref, lg_ref, mg_ref):
    h = _norm_mod(x_ref[...], g_ref[...], sh_ref[...], sc_ref[...]).astype(BF16)
    cos = cos_ref[...]
    sin = sin_ref[...]
    lane = lax.broadcasted_iota(jnp.int32, cos.shape, 1)
    quarter = ATT_HD // 4
    first = (lane % (2 * quarter)) < quarter

    def norm_rope(z, gain, scale):
        ms = jnp.mean(z * z, axis=-1, keepdims=True)
        y = z * lax.rsqrt(ms + EPS) * gain
        swapped = jnp.where(first, pltpu.roll(y, ATT_HD - quarter, 1), pltpu.roll(y, quarter, 1))
        return ((y * cos + swapped * sin) * scale).astype(BF16)

    nq = q_ref.shape[-1]
    nkv = k_ref.shape[-1]
    nl = lx_ref.shape[-1]
    z = _dot(h, w_ref[:, 0:nq])
    for i in range(nq // ATT_HD):
        sl = slice(i * ATT_HD, (i + 1) * ATT_HD)
        q_ref[:, sl] = norm_rope(z[:, sl], qg_ref[...], ATT_HD ** -0.5)
    z = _dot(h, w_ref[:, nq:nq + nkv])
    for i in range(nkv // ATT_HD):
        sl = slice(i * ATT_HD, (i + 1) * ATT_HD)
        k_ref[:, sl] = norm_rope(z[:, sl], kg_ref[...], 1.0)
    off = nq + nkv
    v_ref[...] = _dot(h, w_ref[:, off:off + nkv]).astype(BF16)
    off += nkv
    lx_ref[...] = _dot(h, w_ref[:, off:off + nl])
    off += nl
    lg_ref[...] = _gelu_tanh(_dot(h, w_ref[:, off:off + nl])).astype(BF16)
    off += nl
    mg_ref[...] = jax.nn.sigmoid(_dot(h, w_ref[:, off:off + mg_ref.shape[-1]])).astype(BF16)


def _tile_spec(width, dtype_unused=None):
    return pl.BlockSpec((None, TM, width), lambda b, t: (b, t, 0))


def _vec_spec(width):
    return pl.BlockSpec((1, width), lambda b, t: (0, 0))


def _mod_spec(mods, which, n_ctx_tiles, toff=0):
    ctx_row = mods.shape[0] - 1
    d = mods.shape[-1]
    return pl.BlockSpec((None, None, 1, d),
                        lambda b, t: (jnp.where(t + toff < n_ctx_tiles, ctx_row, b), which, 0, 0))


def _inproj(x, mods, g1, w_hg, w_rest, lbf, lbb, qg, kg, cos_t, sin_t, n_ctx_tiles):
    bsz, t_all, d = x.shape
    grid = (bsz, t_all // TM)
    hw = w_hg.shape[1] // 5
    nq = ATT_HEADS * ATT_HD
    nkv = ATT_KV_HEADS * ATT_HD
    common = [_tile_spec(d), _vec_spec(d), _mod_spec(mods, 0, n_ctx_tiles),
              _mod_spec(mods, 1, n_ctx_tiles)]
    sds = lambda w, dt: jax.ShapeDtypeStruct((bsz, t_all, w), dt)
    hg = pl.pallas_call(
        _inproj_hg_kernel,
        grid=grid,
        in_specs=common + [pl.BlockSpec(w_hg.shape, lambda b, t: (0, 0)), _vec_spec(hw), _vec_spec(hw)],
        out_specs=[_tile_spec(hw)] * 5,
        out_shape=[sds(hw, BF16), sds(hw, BF16), sds(hw, F32), sds(hw, F32), sds(hw, BF16)],
        compiler_params=_params("parallel", "parallel"),
        name="inproj_hgrn",
    )(x, g1, mods, mods, w_hg, lbf, lbb)
    nl = hw
    ng = w_rest.shape[1] - nq - 2 * nkv - 2 * nl
    rest = pl.pallas_call(
        _inproj_rest_kernel,
        grid=grid,
        in_specs=common + [pl.BlockSpec(w_rest.shape, lambda b, t: (0, 0)),
                           _vec_spec(ATT_HD), _vec_spec(ATT_HD),
                           pl.BlockSpec((TM, ATT_HD), lambda b, t: (t, 0)),
                           pl.BlockSpec((TM, ATT_HD), lambda b, t: (t, 0))],
        out_specs=[_tile_spec(nq), _tile_spec(nkv), _tile_spec(nkv), _tile_spec(nl), _tile_spec(nl),
                   _tile_spec(ng)],
        out_shape=[sds(nq, BF16), sds(nkv, BF16), sds(nkv, BF16), sds(nl, F32), sds(nl, BF16),
                   sds(ng, BF16)],
        compiler_params=_params("parallel", "parallel"),
        name="inproj_rest",
    )(x, g1, mods, mods, w_rest, qg, kg, cos_t, sin_t)
    return hg, rest


def _chunk_token(r):
    nv = SEQ_CHUNK // V7X_SUBLANES
    return (r % V7X_SUBLANES) * nv + r // V7X_SUBLANES


def _to_chunk_layout(a):
    nv = SEQ_CHUNK // V7X_SUBLANES
    shp = a.shape
    a = a.reshape(shp[0], shp[1] // SEQ_CHUNK, V7X_SUBLANES, nv, *shp[2:])
    return jnp.swapaxes(a, 2, 3).reshape(shp)


def _split3(g):
    g1 = g.astype(BF16)
    r1 = g - g1.astype(F32)
    g2 = r1.astype(BF16)
    g3 = (r1 - g2.astype(F32)).astype(BF16)
    return jnp.concatenate([g1, g2, g3], axis=-1)


def _move_tokens(x, to_chunk):
    rows, d = x.shape
    nv = SEQ_CHUNK // V7X_SUBLANES
    r = lax.broadcasted_iota(jnp.int32, (rows, rows), 0)
    i = r % SEQ_CHUNK
    src = _chunk_token(i) if to_chunk else (i % nv) * V7X_SUBLANES + i // nv
    pick = lax.broadcasted_iota(jnp.int32, (rows, rows), 1) == (r - i) + src
    parts = _dot(jnp.where(pick, 1.0, 0.0).astype(BF16), _split3(x))
    return (parts[:, 0:d] + parts[:, d:2 * d]) + parts[:, 2 * d:3 * d]


def _stream_in_kernel(ctx_ref, x_ref, o_ref, *, n_ctx_tiles):
    @pl.when(pl.program_id(1) < n_ctx_tiles)
    def _():
        o_ref[...] = _move_tokens(ctx_ref[...], True)

    @pl.when(pl.program_id(1) >= n_ctx_tiles)
    def _():
        o_ref[...] = _move_tokens(x_ref[...], True)


def _stream_out_kernel(x_ref, o_ref):
    o_ref[...] = _move_tokens(x_ref[...], False)


def _stream_in(ctx, x):
    bsz, n_ctx, d = ctx.shape
    n_ctx_tiles = n_ctx // TM
    n_tiles = n_ctx_tiles + x.shape[1] // TM
    return pl.pallas_call(
        functools.partial(_stream_in_kernel, n_ctx_tiles=n_ctx_tiles),
        grid=(bsz, n_tiles),
        in_specs=[pl.BlockSpec((None, TM, d), lambda b, t: (b, jnp.minimum(t, n_ctx_tiles - 1), 0)),
                  pl.BlockSpec((None, TM, d), lambda b, t: (b, jnp.maximum(t - n_ctx_tiles, 0), 0))],
        out_specs=pl.BlockSpec((None, TM, d), lambda b, t: (b, t, 0)),
        out_shape=jax.ShapeDtypeStruct((bsz, n_tiles * TM, d), x.dtype),
        compiler_params=_params("parallel", "arbitrary"),
        name="stream_in",
    )(ctx, x)


def _stream_out(xs):
    bsz, n, d = xs.shape
    blk = pl.BlockSpec((None, TM, d), lambda b, t: (b, t, 0))
    return pl.pallas_call(
        _stream_out_kernel,
        grid=(bsz, n // TM),
        in_specs=[blk],
        out_specs=blk,
        out_shape=jax.ShapeDtypeStruct((bsz, n, d), xs.dtype),
        compiler_params=_params("parallel", "parallel"),
        name="stream_out",
    )(xs)


def _block_row(y, blk, ref_row):
    pos = lax.broadcasted_iota(jnp.int32, (V7X_SUBLANES, 1), 0) % blk
    s = 1
    while ref_row + s < blk:
        y = jnp.where((pos >= ref_row + s) & (pos < ref_row + 2 * s), pltpu.roll(y, s, 0), y)
        s *= 2
    s = 1
    while ref_row - s >= 0:
        y = jnp.where((pos <= ref_row - s) & (pos > ref_row - 2 * s),
                      pltpu.roll(y, V7X_SUBLANES - s, 0), y)
        s *= 2
    return y


def _level_exponent(b, blk, reverse):
    c, width = b.shape
    nv = c // V7X_SUBLANES
    half = blk // 2
    b3 = b.reshape(nv, V7X_SUBLANES, width)
    if blk <= nv:
        ref = half - 1 if reverse else half
        d = []
        for v0 in range(0, nv, blk):
            r = b3[v0 + ref:v0 + ref + 1]
            lo, hi = b3[v0:v0 + half], b3[v0 + half:v0 + blk]
            d += [lo - r, r - hi] if reverse else [r - lo, hi - r]
        d = jnp.concatenate(d, axis=0)
    else:
        sub_blk = blk // nv
        if reverse:
            ref = _block_row(b3[nv - 1], sub_blk, sub_blk // 2 - 1)
        else:
            ref = _block_row(b3[0], sub_blk, sub_blk // 2)
        d = _neg_abs(b3 - ref[None])
    return d.reshape(c, width)


def _neg_abs(x):
    bits = lax.bitcast_convert_type(x, jnp.uint32) | jnp.uint32(0x80000000)
    return lax.bitcast_convert_type(bits, F32)


def _hg_chunks(probs, pair_xor):
    c, width = probs[0]["q"].shape
    heads = [slice(h * HG_DK, (h + 1) * HG_DK) for h in range(width // HG_DK)]
    for p in probs:
        p["kk"] = (1.0 - p["f"]).astype(BF16)
        b = p["b"]
        qe = p["q"] * jnp.exp2(b).astype(BF16)
        p["o"] = [_dot_nt(qe[:, h], s.astype(BF16)) for h, s in zip(heads, p["st"])]
        b_end = b[0:1, :] if p["reverse"] else b[c - 1:c, :]
        ke = p["kk"] * jnp.exp2(b_end - b).astype(BF16)
        st_scale = jnp.exp2(b_end)
        p["st"] = [s * st_scale[:, h] + _dot_tn(p["v"][:, h], ke[:, h]) for h, s in zip(heads, p["st"])]

    blk = c
    while blk >= 2:
        for p in probs:
            e = jnp.exp2(_level_exponent(p["b"], blk, p["reverse"])).astype(BF16)
            qw = p["q"] * e
            kw = p["kk"] * e
            for n, h in enumerate(heads):
                pltpu.store(p["pairs"].at[n], _dot_nt(qw[:, h], kw[:, h]),
                            mask=None if blk == c else pair_xor < blk)
        blk //= 2
    for p in probs:
        pairs = []
        for n, h in enumerate(heads):
            pltpu.store(p["pairs"].at[n], _dot_nt(p["q"][:, h], p["kk"][:, h]), mask=pair_xor == 0)
            pairs.append(jnp.where(p["causal"], p["pairs"][n], 0.0).astype(BF16))
        p["a"] = pairs
    return ([jnp.concatenate(p["o"], axis=-1) for p in probs], [p["a"] for p in probs],
            [p["st"] for p in probs])


def _hg_running_decay(f, tri):
    width = f.shape[-1]
    parts = _dot(tri, _split3(jnp.log2(f)))
    return parts[:, 0:width] + parts[:, width:2 * width] + parts[:, 2 * width:3 * width]


def _hgrn_kernel(qs_ref, v_ref, ff_ref, fb_ref, sg_ref, gain_ref, y_ref, of_ref, ob_ref, pair_ref,
                 *, n_ctx):
    t_all, width = qs_ref.shape
    kdim = HG_DK
    c = HG_CHUNK
    n_chunks = t_all // c
    n_ctx_chunks = n_ctx // c
    row = _chunk_token(lax.broadcasted_iota(jnp.int32, (c, c), 0))
    col = _chunk_token(lax.broadcasted_iota(jnp.int32, (c, c), 1))
    pair_xor = row ^ col
    tri_f = jnp.where(col <= row, 1.0, 0.0).astype(BF16)
    tri_b = jnp.where(col >= row, 1.0, 0.0).astype(BF16)

    n_heads = width // kdim
    f_refs = (ff_ref, fb_ref)
    o_refs = (of_ref, ob_ref)
    tris = (tri_f, tri_b)

    def chunk_ids(i):
        i = jnp.minimum(i, n_chunks - 1)
        ib = jnp.where(i < n_ctx_chunks, n_ctx_chunks - 1 - i, n_chunks - 1 - (i - n_ctx_chunks))
        return (i, ib)

    def rows_of(chunk):
        return pl.ds(pl.multiple_of(chunk * c, c), c)

    def finish(chunks, pairs):
        for o_ref, chunk, a in zip(o_refs, chunks, pairs):
            rows = rows_of(chunk)
            add = [_dot(a[h], v_ref[rows, h * kdim:(h + 1) * kdim]) for h in range(n_heads)]
            o_ref[rows, :] += jnp.concatenate(add, axis=-1)

    def body(i, carry):
        sts, decays, prev_chunks, prev_pairs = carry
        finish(prev_chunks, prev_pairs)
        chunks = chunk_ids(i)
        probs = [dict(q=qs_ref[rows_of(ch), :], v=v_ref[rows_of(ch), :], f=f_ref[rows_of(ch), :],
                      b=b, st=st, reverse=d == 1, causal=(col >= row) if d else (col <= row),
                      pairs=pair_ref.at[d])
                 for d, (ch, f_ref, b, st) in enumerate(zip(chunks, f_refs, decays, sts))]
        outs, pairs, sts = _hg_chunks(probs, pair_xor)
        for o_ref, ch, o in zip(o_refs, chunks, outs):
            o_ref[rows_of(ch), :] = o
        decays = [_hg_running_decay(f_ref[rows_of(ch), :], tri)
                  for ch, f_ref, tri in zip(chunk_ids(i + 1), f_refs, tris)]
        return sts, decays, chunks, pairs

    first = chunk_ids(0)
    for o_ref, ch in zip(o_refs, first):
        o_ref[rows_of(ch), :] = jnp.zeros((c, width), F32)
    zero_st = [[jnp.zeros((kdim, kdim), F32)] * n_heads] * 2
    zero_pairs = [[jnp.zeros((c, c), BF16)] * n_heads] * 2
    decays0 = [_hg_running_decay(f_ref[rows_of(ch), :], tri)
               for ch, f_ref, tri in zip(first, f_refs, tris)]
    _, _, last_chunks, last_pairs = lax.fori_loop(
        0, n_chunks, body, (zero_st, decays0, first, zero_pairs))
    finish(last_chunks, last_pairs)

    def readout(i, carry):
        rows = pl.ds(pl.multiple_of(i * c, c), c)
        for h in range(width // kdim):
            sl = slice(h * kdim, (h + 1) * kdim)
            o = of_ref[rows, sl] + ob_ref[rows, sl]
            ms = jnp.mean(o * o, axis=-1, keepdims=True)
            y = o * lax.rsqrt(ms + EPS) * gain_ref[...]
            y_ref[rows, sl] = (y * sg_ref[rows, sl].astype(F32)).astype(BF16)
        return carry
    lax.fori_loop(0, n_chunks, readout, 0, unroll=2)


def _hgrn(qs, v, ff, fb, sg, gain, n_ctx):
    bsz, t_all, width = qs.shape
    bw = HG_HEADS_PER_STEP * HG_DK
    spec = pl.BlockSpec((None, t_all, bw), lambda b, h: (b, 0, h))
    return pl.pallas_call(
        functools.partial(_hgrn_kernel, n_ctx=n_ctx),
        grid=(bsz, width // bw),
        in_specs=[spec] * 5 + [pl.BlockSpec((1, HG_DK), lambda b, h: (0, 0))],
        out_specs=spec,
        out_shape=jax.ShapeDtypeStruct((bsz, t_all, width), BF16),
        scratch_shapes=[pltpu.VMEM((t_all, bw), F32), pltpu.VMEM((t_all, bw), F32),
                        pltpu.VMEM((2, HG_HEADS_PER_STEP, HG_CHUNK, HG_CHUNK), F32)],
        compiler_params=_params("parallel", "parallel"),
        name="hgrn2_mixer",
    )(qs, v, ff, fb, sg, gain)


def _att_kernel(q_ref, k_ref, v_ref, o_ref, *, n_ctx):
    qi = pl.program_id(2)
    n_ctx_tiles = n_ctx // q_ref.shape[0]

    def run(nk):
        k = k_ref[0:nk, :]
        v = v_ref[0:nk, :]
        for g in range(ATT_GROUP):
            sl = slice(g * ATT_HD, (g + 1) * ATT_HD)
            s = _dot_nt(q_ref[:, sl], k)
            p = jnp.exp(s - jnp.max(s, axis=-1, keepdims=True))
            den = jnp.sum(p, axis=-1, keepdims=True)
            o_ref[:, sl] = (_dot(p.astype(BF16), v) / den).astype(BF16)

    @pl.when(qi < n_ctx_tiles)
    def _():
        run(n_ctx)

    @pl.when(qi >= n_ctx_tiles)
    def _():
        run(k_ref.shape[0])


def _attention(q, k, v, n_ctx):
    bsz, t_all, nq = q.shape
    gw = ATT_GROUP * ATT_HD
    return pl.pallas_call(
        functools.partial(_att_kernel, n_ctx=n_ctx),
        grid=(bsz, ATT_KV_HEADS, t_all // TM),
        in_specs=[pl.BlockSpec((None, TM, gw), lambda b, h, i: (b, i, h)),
                  pl.BlockSpec((None, t_all, ATT_HD), lambda b, h, i: (b, 0, h)),
                  pl.BlockSpec((None, t_all, ATT_HD), lambda b, h, i: (b, 0, h))],
        out_specs=pl.BlockSpec((None, TM, gw), lambda b, h, i: (b, i, h)),
        out_shape=jax.ShapeDtypeStruct((bsz, t_all, nq), BF16),
        compiler_params=_params("parallel", "parallel", "arbitrary"),
        name="gqa_mixer",
    )(q, k, v)


def _lru_kernel(x_ref, gl_ref, cw_ref, cb_ref, wa_ref, ba_ref, wx_ref, bx_ref, lam_ref, y_ref,
                af_ref, uf_ref, ab_ref, ub_ref, *, n_ctx):
    t_all, w = x_ref.shape
    c = SEQ_CHUNK
    ns = V7X_SUBLANES
    nv = c // ns
    n_chunks = t_all // c
    n_ctx_chunks = n_ctx // c
    sub = lax.broadcasted_iota(jnp.int32, (ns, 1), 0)

    def rows_of(chunk):
        return pl.ds(pl.multiple_of(chunk * c, c), c)

    def vreg_rows(chunk, v):
        return pl.ds(pl.multiple_of(chunk * c + v * ns, ns), ns)

    a_refs = (af_ref, ab_ref)
    u_refs = (uf_ref, ub_ref)
    rate = [(-0.5 * LRU_C) * jax.nn.softplus(-lam_ref[d:d + 1, :]) for d in range(2)]
    half_ba = [0.5 * ba_ref[d:d + 1, :] for d in range(2)]
    half_bx = [0.5 * bx_ref[d:d + 1, :] for d in range(2)]

    def one_later(cur, prev):
        return pltpu.roll(jnp.where(sub == ns - 1, prev, cur), 1, 0)

    def one_earlier(cur, nxt):
        return pltpu.roll(jnp.where(sub == 0, nxt, cur), ns - 1, 0)

    def gates(i, carry):
        x3 = x_ref[rows_of(i), :].reshape(nv, ns, w)
        has_prev = jnp.logical_and(i > 0, i != n_ctx_chunks)
        has_next = jnp.logical_and(i < n_chunks - 1, i != n_ctx_chunks - 1)
        ip = jnp.maximum(i - 1, 0)
        inx = jnp.minimum(i + 1, n_chunks - 1)
        prev_1 = jnp.where(has_prev, x_ref[vreg_rows(ip, nv - 1), :], 0.0)
        prev_2 = jnp.where(has_prev, x_ref[vreg_rows(ip, nv - 2), :], 0.0)
        next_0 = jnp.where(has_next, x_ref[vreg_rows(inx, 0), :], 0.0)
        ext = jnp.concatenate([one_later(x3[nv - 2], prev_2)[None], one_later(x3[nv - 1], prev_1)[None],
                               x3, one_earlier(x3[0], next_0)[None]], axis=0)
        xc = cb_ref[...] + ext[0:nv] * cw_ref[0:1, :]
        for j in range(1, LRU_CONV):
            xc = xc + ext[j:j + nv] * cw_ref[j:j + 1, :]
        xc = xc.reshape(c, w)
        xcb = xc.astype(BF16)
        half_xc = 0.5 * xc
        for d in range(2):
            th_r = jnp.tanh(_dot(xcb, wa_ref[d]) + half_ba[d])
            th_i = jnp.tanh(_dot(xcb, wx_ref[d]) + half_bx[d])
            log_a = rate[d] * (th_r + 1.0)
            a = jnp.exp(log_a)
            one_minus_a2 = jnp.tanh(log_a) * (-1.0 - a * a)
            a_refs[d][rows_of(i), :] = a
            u_refs[d][rows_of(i), :] = jnp.sqrt(one_minus_a2) * ((th_i + 1.0) * half_xc)
        return carry
    lax.fori_loop(0, n_chunks, gates, 0, unroll=2)

    def chunk_scan(a_ref, u_ref, chunk, h_in, reverse):
        a3 = a_ref[rows_of(chunk), :].reshape(nv, ns, w)
        u3 = u_ref[rows_of(chunk), :].reshape(nv, ns, w)
        order = list(range(nv - 1, -1, -1) if reverse else range(nv))
        h, p = u3[order[0]], a3[order[0]]
        hs, ps = {order[0]: h}, {order[0]: p}
        for v in order[1:]:
            h = a3[v] * h + u3[v]
            p = a3[v] * p
            hs[v], ps[v] = h, p
        s = 1
        while s < ns:
            sh = (ns - s) if reverse else s
            keep = (sub + s < ns) if reverse else (sub >= s)
            h = h + p * jnp.where(keep, pltpu.roll(h, sh, 0), 0.0)
            p = p * jnp.where(keep, pltpu.roll(p, sh, 0), 1.0)
            s *= 2
        end = h + p * h_in
        if reverse:
            enter = jnp.where(sub == ns - 1, h_in, pltpu.roll(end, ns - 1, 0))
            h_out = end[0:1, :]
        else:
            enter = jnp.where(sub == 0, h_in, pltpu.roll(end, 1, 0))
            h_out = end[ns - 1:ns, :]
        full = jnp.concatenate([(hs[v] + ps[v] * enter)[None] for v in range(nv)], axis=0)
        u_ref[rows_of(chunk), :] = full.reshape(c, w)
        return h_out

    def scan_body(i, carry):
        hf, hb = carry
        hf = chunk_scan(af_ref, uf_ref, i, hf, False)
        ib = jnp.where(i < n_ctx_chunks, n_ctx_chunks - 1 - i, n_chunks - 1 - (i - n_ctx_chunks))
        hb = chunk_scan(ab_ref, ub_ref, ib, hb, True)
        return hf, hb
    zero = jnp.zeros((1, w), F32)
    lax.fori_loop(0, n_chunks, scan_body, (zero, zero))

    def out(i, carry):
        rows = rows_of(i)
        y_ref[rows, :] = ((uf_ref[rows, :] + ub_ref[rows, :]) * gl_ref[rows, :].astype(F32)).astype(BF16)
        return carry
    lax.fori_loop(0, n_chunks, out, 0)


def _rglru(lx, lg, conv_w, conv_b, w_a, b_a, w_x, b_x, lam, n_ctx):
    bsz, t_all, width = lx.shape
    bw = width // LRU_BLOCKS
    seq = pl.BlockSpec((None, t_all, bw), lambda b, j: (b, 0, j))
    vec = lambda rows: pl.BlockSpec((rows, bw), lambda b, j: (0, j))
    wspec = pl.BlockSpec((2, None, bw, bw), lambda b, j: (0, j, 0, 0))
    return pl.pallas_call(
        functools.partial(_lru_kernel, n_ctx=n_ctx),
        grid=(bsz, LRU_BLOCKS),
        in_specs=[seq, seq, vec(LRU_CONV), vec(1), wspec, vec(2), wspec, vec(2), vec(2)],
        out_specs=seq,
        out_shape=jax.ShapeDtypeStruct((bsz, t_all, width), BF16),
        scratch_shapes=[pltpu.VMEM((t_all, bw), F32)] * 4,
        compiler_params=_params("parallel", "parallel"),
        name="rglru_mixer",
    )(lx, lg, conv_w, conv_b, w_a, b_a, w_x, b_x, lam)


def _merge_kernel(x_ref, a_ref, b_ref, r_ref, mg_ref, wbr_ref, wo_ref, gate_ref, o_ref):
    d = x_ref.shape[-1]
    y = None
    for i, br in enumerate((a_ref, b_ref, r_ref)):
        term = mg_ref[:, i * d:(i + 1) * d].astype(F32) * _dot(br[...], wbr_ref[i])
        y = term if y is None else y + term
    o_ref[...] = x_ref[...] + gate_ref[...] * _dot(y.astype(BF16), wo_ref[...])


def _ffn_kernel(x_ref, g_ref, sh_ref, sc_ref, gate_ref, wg_ref, wu_ref, wd_ref, *rest, final):
    x = x_ref[...]
    h = _norm_mod(x, g_ref[...], sh_ref[...], sc_ref[...]).astype(BF16)
    act = (_silu(_dot(h, wg_ref[...])) * _dot(h, wu_ref[...])).astype(BF16)
    y = x + gate_ref[...] * _dot(act, wd_ref[...])
    if final:
        gf_ref, o_ref = rest
        ms = jnp.mean(y * y, axis=-1, keepdims=True)
        y = y * lax.rsqrt(ms + EPS) * gf_ref[...]
    else:
        (o_ref,) = rest
    o_ref[...] = y


def _merge_ffn(x, ya, yb, yr, mg, mods, w_br, w_o, g2, w_g, w_u, w_d, n_ctx_tiles, g_final):
    bsz, t_all, d = x.shape
    final = g_final is not None
    toff = n_ctx_tiles if final else 0
    n_t = t_all // TM - toff
    tile = lambda w: pl.BlockSpec((None, TM, w), lambda b, t: (b, t + toff, 0))
    out_tile = pl.BlockSpec((None, TM, d), lambda b, t: (b, t, 0))
    mod = lambda which: _mod_spec(mods, which, n_ctx_tiles, toff)
    full = lambda a: pl.BlockSpec(a.shape, lambda b, t: (0,) * a.ndim)
    out_sds = jax.ShapeDtypeStruct((bsz, n_t * TM, d), F32)
    x1 = pl.pallas_call(
        _merge_kernel,
        grid=(bsz, n_t),
        in_specs=[tile(d), tile(d), tile(d), tile(d), tile(N_BRANCH * d), full(w_br), full(w_o), mod(2)],
        out_specs=out_tile,
        out_shape=out_sds,
        compiler_params=_params("parallel", "parallel"),
        name="merge",
    )(x, ya, yb, yr, mg, w_br, w_o, mods)
    x1_tile = pl.BlockSpec((None, TM, d), lambda b, t: (b, t, 0))
    args = [x1, g2, mods, mods, mods, w_g, w_u, w_d]
    specs = [x1_tile, _vec_spec(d), mod(3), mod(4), mod(5), full(w_g), full(w_u), full(w_d)]
    if final:
        args.append(g_final)
        specs.append(_vec_spec(d))
    return pl.pallas_call(
        functools.partial(_ffn_kernel, final=final),
        grid=(bsz, n_t),
        in_specs=specs,
        out_specs=out_tile,
        out_shape=out_sds,
        compiler_params=_params("parallel", "parallel"),
        name="ffn",
    )(*args)


def _rope_tables(n_ctx, n_lat):
    rows = n_lat // GRID_W
    r = np.repeat(np.arange(rows), GRID_W).astype(np.float32)
    col = np.tile(np.arange(GRID_W), rows).astype(np.float32)
    half = ATT_HD // 2
    inv = np.power(np.float32(ROPE_THETA), -np.arange(0, half, 2, dtype=np.float32) / np.float32(half))
    ar = jnp.asarray(r[:, None] * inv[None, :], F32)
    ac = jnp.asarray(col[:, None] * inv[None, :], F32)
    cr, sr, cc, sc = jnp.cos(ar), jnp.sin(ar), jnp.cos(ac), jnp.sin(ac)
    cos_l = jnp.concatenate([cr, cr, cc, cc], axis=-1)
    sin_l = jnp.concatenate([-sr, sr, -sc, sc], axis=-1)
    cos_t = jnp.concatenate([jnp.ones((n_ctx, ATT_HD), F32), cos_l], axis=0)
    sin_t = jnp.concatenate([jnp.zeros((n_ctx, ATT_HD), F32), sin_l], axis=0)
    return _to_chunk_layout(cos_t[None])[0], _to_chunk_layout(sin_t[None])[0]


def kernel(x, c, ctx, c_ctx, w_mod, b_mod, norm1, w_in, hg_lb_logits, hg_out_norm, att_q_norm, att_k_norm, lru_conv_w, lru_conv_b, lru_w_a, lru_b_a, lru_w_x, lru_b_x, lru_lambda, w_branch, w_out, norm2, w_ffn_in, w_ffn_out, norm_final):
    bsz, n_lat, d = x.shape
    n_ctx = ctx.shape[1]
    depth = w_in.shape[0]
    assert n_ctx % TM == 0 and n_lat % TM == 0 and n_lat % GRID_W == 0
    n_ctx_tiles = n_ctx // TM
    hg_w = HG_HEADS * HG_DK
    ffn_h = w_ffn_out.shape[1]

    c_all = jnp.concatenate([c, c_ctx[None, :]], axis=0)
    mods = _mod_vectors(c_all, w_mod, b_mod).reshape(depth, bsz + 1, N_MOD, 1, d)
    lb = _lower_bounds(hg_lb_logits)
    cos_t, sin_t = _rope_tables(n_ctx, n_lat)

    xs = _stream_in(ctx, x)
    for l in range(depth):
        last = l == depth - 1
        w_hg = w_in[l, :, :5 * hg_w].astype(BF16)
        w_rest = w_in[l, :, 5 * hg_w:].astype(BF16)
        hg, rest = _inproj(xs, mods[l], norm1[l][None, :], w_hg, w_rest,
                           lb[0, l][None, :], lb[1, l][None, :],
                           att_q_norm[l][None, :], att_k_norm[l][None, :], cos_t, sin_t, n_ctx_tiles)
        qs, hv, ff, fb, sg = hg
        aq, ak, av, lx, lg, mg = rest
        ya = _hgrn(qs, hv, ff, fb, sg, hg_out_norm[l][None, :], n_ctx)
        yb = _attention(aq, ak, av, n_ctx)
        yr = _rglru(lx, lg, lru_conv_w[l], lru_conv_b[l][None, :], (0.5 * lru_w_a[l]).astype(BF16),
                    lru_b_a[l], (0.5 * lru_w_x[l]).astype(BF16), lru_b_x[l], lru_lambda[l], n_ctx)
        xs = _merge_ffn(xs, ya, yb, yr, mg, mods[l], w_branch[l].astype(BF16), w_out[l].astype(BF16),
                        norm2[l][None, :], w_ffn_in[l, :, :ffn_h].astype(BF16),
                        w_ffn_in[l, :, ffn_h:].astype(BF16), w_ffn_out[l].astype(BF16),
                        n_ctx_tiles, norm_final[None, :] if last else None)
    return _stream_out(xs)
```

```python
import functools

import numpy as np
import jax
import jax.numpy as jnp
from jax import lax
from jax.experimental import pallas as pl
from jax.experimental.pallas import tpu as pltpu

F32 = jnp.float32
BF16 = jnp.bfloat16

EPS = 1e-6
GRID_W = 64
HG_HEADS = 8
HG_DK = 128
ATT_HEADS = 8
ATT_KV_HEADS = 2
ATT_GROUP = ATT_HEADS // ATT_KV_HEADS
ATT_HD = 128
ROPE_THETA = 10000.0
LRU_BLOCKS = 4
LRU_CONV = 4
LRU_C = 8.0
N_BRANCH = 3
N_MOD = 6

V7X_VMEM_BYTES = 64 * 1024 * 1024
V7X_LANES = 128
V7X_SUBLANES = 8
VMEM_LIMIT = (V7X_VMEM_BYTES * 3) // 4

TM = 256
SEQ_CHUNK = 128
HG_CHUNK = SEQ_CHUNK
HG_HEADS_PER_STEP = 2


def _silu(x):
    return x * jax.nn.sigmoid(x)


def _gelu_tanh(x):
    return 0.5 * x * (1.0 + jnp.tanh(np.sqrt(2.0 / np.pi) * (x + 0.044715 * (x * x * x))))


def _dot(a, b):
    return jnp.dot(a, b, preferred_element_type=F32)


def _dot_nt(a, b):
    return lax.dot_general(a, b, (((1,), (1,)), ((), ())), preferred_element_type=F32)


def _dot_tn(a, b):
    return lax.dot_general(a, b, (((0,), (0,)), ((), ())), preferred_element_type=F32)


def _params(*sem):
    return pltpu.CompilerParams(dimension_semantics=sem, vmem_limit_bytes=VMEM_LIMIT)


def _mod_kernel(c_ref, w_ref, b_ref, o_ref):
    ca = _silu(c_ref[...]).astype(BF16)
    o_ref[...] = _dot(ca, w_ref[...].astype(BF16)) + b_ref[...]


def _mod_vectors(c_all, w_mod, b_mod):
    depth, d, n = w_mod.shape
    rows = c_all.shape[0]
    tn = n // 4
    return pl.pallas_call(
        _mod_kernel,
        grid=(depth, n // tn),
        in_specs=[pl.BlockSpec((rows, d), lambda l, j: (0, 0)),
                  pl.BlockSpec((None, d, tn), lambda l, j: (l, 0, j)),
                  pl.BlockSpec((None, 1, tn), lambda l, j: (l, 0, j))],
        out_specs=pl.BlockSpec((None, rows, tn), lambda l, j: (l, 0, j)),
        out_shape=jax.ShapeDtypeStruct((depth, rows, n), F32),
        compiler_params=_params("parallel", "parallel"),
        name="mod_vectors",
    )(c_all, w_mod, b_mod.reshape(depth, 1, n))


def _lb_kernel(x_ref, o_ref):
    depth = x_ref.shape[1]
    rows = [x_ref[:, l, :] for l in range(depth)]
    m = functools.reduce(jnp.maximum, rows)
    e = [jnp.exp(r - m) for r in rows]
    inv = 1.0 / functools.reduce(lambda a, b: a + b, e)
    acc = jnp.zeros_like(rows[0])
    o_ref[:, 0, :] = acc
    for l in range(1, depth):
        acc = acc + e[l] * inv
        o_ref[:, l, :] = acc


def _lower_bounds(logits):
    return pl.pallas_call(
        _lb_kernel,
        out_shape=jax.ShapeDtypeStruct(logits.shape, F32),
        name="hg_lower_bounds",
    )(logits.astype(F32))


def _norm_mod(x, g, shift, scale):
    ms = jnp.mean(x * x, axis=-1, keepdims=True)
    y = x * lax.rsqrt(ms + EPS) * g
    return y * (1.0 + scale) + shift


def _inproj_hg_kernel(x_ref, g_ref, sh_ref, sc_ref, w_ref, lbf_ref, lbb_ref,
                      qs_ref, v_ref, ff_ref, fb_ref, sg_ref):
    h = _norm_mod(x_ref[...], g_ref[...], sh_ref[...], sc_ref[...]).astype(BF16)
    w = qs_ref.shape[-1]
    z = _dot(h, w_ref[:, 0:w])
    qs_ref[...] = (_silu(z) * (HG_DK ** -0.5)).astype(BF16)
    v_ref[...] = _dot(h, w_ref[:, w:2 * w]).astype(BF16)
    for k, (lb_ref, f_ref) in enumerate(((lbf_ref, ff_ref), (lbb_ref, fb_ref))):
        lb = lb_ref[...]
        z = _dot(h, w_ref[:, (2 + k) * w:(3 + k) * w])
        f_ref[...] = lb + (1.0 - lb) * jax.nn.sigmoid(z)
    sg_ref[...] = _silu(_dot(h, w_ref[:, 4 * w:5 * w])).astype(BF16)


def _inproj_rest_kernel(x_ref, g_ref, sh_ref, sc_ref, w_ref, qg_ref, kg_ref, cos_ref, sin_ref,
                        q_ref, k_ref, v_ref, lx_ref, lg_ref, mg_ref):
    h = _norm_mod(x_ref[...], g_ref[...], sh_ref[...], sc_ref[...]).astype(BF16)
    cos = cos_ref[...]
    sin = sin_ref[...]
    lane = lax.broadcasted_iota(jnp.int32, cos.shape, 1)
    quarter = ATT_HD // 4
    first = (lane % (2 * quarter)) < quarter

    def norm_rope(z, gain, scale):
        ms = jnp.mean(z * z, axis=-1, keepdims=True)
        y = z * lax.rsqrt(ms + EPS) * gain
        swapped = jnp.where(first, pltpu.roll(y, ATT_HD - quarter, 1), pltpu.roll(y, quarter, 1))
        return ((y * cos + swapped * sin) * scale).astype(BF16)

    nq = q_ref.shape[-1]
    nkv = k_ref.shape[-1]
    nl = lx_ref.shape[-1]
    z = _dot(h, w_ref[:, 0:nq])
    for i in range(nq // ATT_HD):
        sl = slice(i * ATT_HD, (i + 1) * ATT_HD)
        q_ref[:, sl] = norm_rope(z[:, sl], qg_ref[...], ATT_HD ** -0.5)
    z = _dot(h, w_ref[:, nq:nq + nkv])
    for i in range(nkv // ATT_HD):
        sl = slice(i * ATT_HD, (i + 1) * ATT_HD)
        k_ref[:, sl] = norm_rope(z[:, sl], kg_ref[...], 1.0)
    off = nq + nkv
    v_ref[...] = _dot(h, w_ref[:, off:off + nkv]).astype(BF16)
    off += nkv
    lx_ref[...] = _dot(h, w_ref[:, off:off + nl])
    off += nl
    lg_ref[...] = _gelu_tanh(_dot(h, w_ref[:, off:off + nl])).astype(BF16)
    off += nl
    mg_ref[...] = jax.nn.sigmoid(_dot(h, w_ref[:, off:off + mg_ref.shape[-1]])).astype(BF16)


def _tile_spec(width, dtype_unused=None):
    return pl.BlockSpec((None, TM, width), lambda b, t: (b, t, 0))


def _vec_spec(width):
    return pl.BlockSpec((1, width), lambda b, t: (0, 0))


def _mod_spec(mods, which, n_ctx_tiles, toff=0):
    ctx_row = mods.shape[0] - 1
    d = mods.shape[-1]
    return pl.BlockSpec((None, None, 1, d),
                        lambda b, t: (jnp.where(t + toff < n_ctx_tiles, ctx_row, b), which, 0, 0))


def _inproj(x, mods, g1, w_hg, w_rest, lbf, lbb, qg, kg, cos_t, sin_t, n_ctx_tiles):
    bsz, t_all, d = x.shape
    grid = (bsz, t_all // TM)
    hw = w_hg.shape[1] // 5
    nq = ATT_HEADS * ATT_HD
    nkv = ATT_KV_HEADS * ATT_HD
    common = [_tile_spec(d), _vec_spec(d), _mod_spec(mods, 0, n_ctx_tiles),
              _mod_spec(mods, 1, n_ctx_tiles)]
    sds = lambda w, dt: jax.ShapeDtypeStruct((bsz, t_all, w), dt)
    hg = pl.pallas_call(
        _inproj_hg_kernel,
        grid=grid,
        in_specs=common + [pl.BlockSpec(w_hg.shape, lambda b, t: (0, 0)), _vec_spec(hw), _vec_spec(hw)],
        out_specs=[_tile_spec(hw)] * 5,
        out_shape=[sds(hw, BF16), sds(hw, BF16), sds(hw, F32), sds(hw, F32), sds(hw, BF16)],
        compiler_params=_params("parallel", "parallel"),
        name="inproj_hgrn",
    )(x, g1, mods, mods, w_hg, lbf, lbb)
    nl = hw
    ng = w_rest.shape[1] - nq - 2 * nkv - 2 * nl
    rest = pl.pallas_call(
        _inproj_rest_kernel,
        grid=grid,
        in_specs=common + [pl.BlockSpec(w_rest.shape, lambda b, t: (0, 0)),
                           _vec_spec(ATT_HD), _vec_spec(ATT_HD),
                           pl.BlockSpec((TM, ATT_HD), lambda b, t: (t, 0)),
                           pl.BlockSpec((TM, ATT_HD), lambda b, t: (t, 0))],
        out_specs=[_tile_spec(nq), _tile_spec(nkv), _tile_spec(nkv), _tile_spec(nl), _tile_spec(nl),
                   _tile_spec(ng)],
        out_shape=[sds(nq, BF16), sds(nkv, BF16), sds(nkv, BF16), sds(nl, F32), sds(nl, BF16),
                   sds(ng, BF16)],
        compiler_params=_params("parallel", "parallel"),
        name="inproj_rest",
    )(x, g1, mods, mods, w_rest, qg, kg, cos_t, sin_t)
    return hg, rest


def _chunk_token(r):
    nv = SEQ_CHUNK // V7X_SUBLANES
    return (r % V7X_SUBLANES) * nv + r // V7X_SUBLANES


def _to_chunk_layout(a):
    nv = SEQ_CHUNK // V7X_SUBLANES
    shp = a.shape
    a = a.reshape(shp[0], shp[1] // SEQ_CHUNK, V7X_SUBLANES, nv, *shp[2:])
    return jnp.swapaxes(a, 2, 3).reshape(shp)


def _split3(g):
    g1 = g.astype(BF16)
    r1 = g - g1.astype(F32)
    g2 = r1.astype(BF16)
    g3 = (r1 - g2.astype(F32)).astype(BF16)
    return jnp.concatenate([g1, g2, g3], axis=-1)


def _move_tokens(x, to_chunk):
    rows, d = x.shape
    nv = SEQ_CHUNK // V7X_SUBLANES
    r = lax.broadcasted_iota(jnp.int32, (rows, rows), 0)
    i = r % SEQ_CHUNK
    src = _chunk_token(i) if to_chunk else (i % nv) * V7X_SUBLANES + i // nv
    pick = lax.broadcasted_iota(jnp.int32, (rows, rows), 1) == (r - i) + src
    parts = _dot(jnp.where(pick, 1.0, 0.0).astype(BF16), _split3(x))
    return (parts[:, 0:d] + parts[:, d:2 * d]) + parts[:, 2 * d:3 * d]


def _stream_in_kernel(ctx_ref, x_ref, o_ref, *, n_ctx_tiles):
    @pl.when(pl.program_id(1) < n_ctx_tiles)
    def _():
        o_ref[...] = _move_tokens(ctx_ref[...], True)

    @pl.when(pl.program_id(1) >= n_ctx_tiles)
    def _():
        o_ref[...] = _move_tokens(x_ref[...], True)


def _stream_out_kernel(x_ref, o_ref):
    o_ref[...] = _move_tokens(x_ref[...], False)


def _stream_in(ctx, x):
    bsz, n_ctx, d = ctx.shape
    n_ctx_tiles = n_ctx // TM
    n_tiles = n_ctx_tiles + x.shape[1] // TM
    return pl.pallas_call(
        functools.partial(_stream_in_kernel, n_ctx_tiles=n_ctx_tiles),
        grid=(bsz, n_tiles),
        in_specs=[pl.BlockSpec((None, TM, d), lambda b, t: (b, jnp.minimum(t, n_ctx_tiles - 1), 0)),
                  pl.BlockSpec((None, TM, d), lambda b, t: (b, jnp.maximum(t - n_ctx_tiles, 0), 0))],
        out_specs=pl.BlockSpec((None, TM, d), lambda b, t: (b, t, 0)),
        out_shape=jax.ShapeDtypeStruct((bsz, n_tiles * TM, d), x.dtype),
        compiler_params=_params("parallel", "arbitrary"),
        name="stream_in",
    )(ctx, x)


def _stream_out(xs):
    bsz, n, d = xs.shape
    blk = pl.BlockSpec((None, TM, d), lambda b, t: (b, t, 0))
    return pl.pallas_call(
        _stream_out_kernel,
        grid=(bsz, n // TM),
        in_specs=[blk],
        out_specs=blk,
        out_shape=jax.ShapeDtypeStruct((bsz, n, d), xs.dtype),
        compiler_params=_params("parallel", "parallel"),
        name="stream_out",
    )(xs)


def _block_row(y, blk, ref_row):
    pos = lax.broadcasted_iota(jnp.int32, (V7X_SUBLANES, 1), 0) % blk
    s = 1
    while ref_row + s < blk:
        y = jnp.where((pos >= ref_row + s) & (pos < ref_row + 2 * s), pltpu.roll(y, s, 0), y)
        s *= 2
    s = 1
    while ref_row - s >= 0:
        y = jnp.where((pos <= ref_row - s) & (pos > ref_row - 2 * s),
                      pltpu.roll(y, V7X_SUBLANES - s, 0), y)
        s *= 2
    return y


def _level_exponent(b, blk, reverse):
    c, width = b.shape
    nv = c // V7X_SUBLANES
    half = blk // 2
    b3 = b.reshape(nv, V7X_SUBLANES, width)
    if blk <= nv:
        ref = half - 1 if reverse else half
        d = []
        for v0 in range(0, nv, blk):
            r = b3[v0 + ref:v0 + ref + 1]
            lo, hi = b3[v0:v0 + half], b3[v0 + half:v0 + blk]
            d += [lo - r, r - hi] if reverse else [r - lo, hi - r]
        d = jnp.concatenate(d, axis=0)
    else:
        sub_blk = blk // nv
        if reverse:
            ref = _block_row(b3[nv - 1], sub_blk, sub_blk // 2 - 1)
        else:
            ref = _block_row(b3[0], sub_blk, sub_blk // 2)
        d = _neg_abs(b3 - ref[None])
    return d.reshape(c, width)


def _neg_abs(x):
    bits = lax.bitcast_convert_type(x, jnp.uint32) | jnp.uint32(0x80000000)
    return lax.bitcast_convert_type(bits, F32)


def _hg_chunks(probs, pair_xor):
    c, width = probs[0]["q"].shape
    heads = [slice(h * HG_DK, (h + 1) * HG_DK) for h in range(width // HG_DK)]
    for p in probs:
        p["kk"] = (1.0 - p["f"]).astype(BF16)
        b = p["b"]
        qe = p["q"] * jnp.exp2(b).astype(BF16)
        p["o"] = [_dot_nt(qe[:, h], s.astype(BF16)) for h, s in zip(heads, p["st"])]
        b_end = b[0:1, :] if p["reverse"] else b[c - 1:c, :]
        ke = p["kk"] * jnp.exp2(b_end - b).astype(BF16)
        st_scale = jnp.exp2(b_end)
        p["st"] = [s * st_scale[:, h] + _dot_tn(p["v"][:, h], ke[:, h]) for h, s in zip(heads, p["st"])]

    blk = c
    while blk >= 2:
        for p in probs:
            e = jnp.exp2(_level_exponent(p["b"], blk, p["reverse"])).astype(BF16)
            qw = p["q"] * e
            kw = p["kk"] * e
            for n, h in enumerate(heads):
                pltpu.store(p["pairs"].at[n], _dot_nt(qw[:, h], kw[:, h]),
                            mask=None if blk == c else pair_xor < blk)
        blk //= 2
    for p in probs:
        pairs = []
        for n, h in enumerate(heads):
            pltpu.store(p["pairs"].at[n], _dot_nt(p["q"][:, h], p["kk"][:, h]), mask=pair_xor == 0)
            pairs.append(jnp.where(p["causal"], p["pairs"][n], 0.0).astype(BF16))
        p["a"] = pairs
    return ([jnp.concatenate(p["o"], axis=-1) for p in probs], [p["a"] for p in probs],
            [p["st"] for p in probs])


def _hg_running_decay(f, tri):
    width = f.shape[-1]
    parts = _dot(tri, _split3(jnp.log2(f)))
    return parts[:, 0:width] + parts[:, width:2 * width] + parts[:, 2 * width:3 * width]


def _hgrn_kernel(qs_ref, v_ref, ff_ref, fb_ref, sg_ref, gain_ref, y_ref, of_ref, ob_ref, pair_ref,
                 *, n_ctx):
    t_all, width = qs_ref.shape
    kdim = HG_DK
    c = HG_CHUNK
    n_chunks = t_all // c
    n_ctx_chunks = n_ctx // c
    row = _chunk_token(lax.broadcasted_iota(jnp.int32, (c, c), 0))
    col = _chunk_token(lax.broadcasted_iota(jnp.int32, (c, c), 1))
    pair_xor = row ^ col
    tri_f = jnp.where(col <= row, 1.0, 0.0).astype(BF16)
    tri_b = jnp.where(col >= row, 1.0, 0.0).astype(BF16)

    n_heads = width // kdim
    f_refs = (ff_ref, fb_ref)
    o_refs = (of_ref, ob_ref)
    tris = (tri_f, tri_b)

    def chunk_ids(i):
        i = jnp.minimum(i, n_chunks - 1)
        ib = jnp.where(i < n_ctx_chunks, n_ctx_chunks - 1 - i, n_chunks - 1 - (i - n_ctx_chunks))
        return (i, ib)

    def rows_of(chunk):
        return pl.ds(pl.multiple_of(chunk * c, c), c)

    def finish(chunks, pairs):
        for o_ref, chunk, a in zip(o_refs, chunks, pairs):
            rows = rows_of(chunk)
            add = [_dot(a[h], v_ref[rows, h * kdim:(h + 1) * kdim]) for h in range(n_heads)]
            o_ref[rows, :] += jnp.concatenate(add, axis=-1)

    def body(i, carry):
        sts, decays, prev_chunks, prev_pairs = carry
        finish(prev_chunks, prev_pairs)
        chunks = chunk_ids(i)
        probs = [dict(q=qs_ref[rows_of(ch), :], v=v_ref[rows_of(ch), :], f=f_ref[rows_of(ch), :],
                      b=b, st=st, reverse=d == 1, causal=(col >= row) if d else (col <= row),
                      pairs=pair_ref.at[d])
                 for d, (ch, f_ref, b, st) in enumerate(zip(chunks, f_refs, decays, sts))]
        outs, pairs, sts = _hg_chunks(probs, pair_xor)
        for o_ref, ch, o in zip(o_refs, chunks, outs):
            o_ref[rows_of(ch), :] = o
        decays = [_hg_running_decay(f_ref[rows_of(ch), :], tri)
                  for ch, f_ref, tri in zip(chunk_ids(i + 1), f_refs, tris)]
        return sts, decays, chunks, pairs

    first = chunk_ids(0)
    for o_ref, ch in zip(o_refs, first):
        o_ref[rows_of(ch), :] = jnp.zeros((c, width), F32)
    zero_st = [[jnp.zeros((kdim, kdim), F32)] * n_heads] * 2
    zero_pairs = [[jnp.zeros((c, c), BF16)] * n_heads] * 2
    decays0 = [_hg_running_decay(f_ref[rows_of(ch), :], tri)
               for ch, f_ref, tri in zip(first, f_refs, tris)]
    _, _, last_chunks, last_pairs = lax.fori_loop(
        0, n_chunks, body, (zero_st, decays0, first, zero_pairs))
    finish(last_chunks, last_pairs)

    def readout(i, carry):
        rows = pl.ds(pl.multiple_of(i * c, c), c)
        for h in range(width // kdim):
            sl = slice(h * kdim, (h + 1) * kdim)
            o = of_ref[rows, sl] + ob_ref[rows, sl]
            ms = jnp.mean(o * o, axis=-1, keepdims=True)
            y = o * lax.rsqrt(ms + EPS) * gain_ref[...]
            y_ref[rows, sl] = (y * sg_ref[rows, sl].astype(F32)).astype(BF16)
        return carry
    lax.fori_loop(0, n_chunks, readout, 0, unroll=2)


def _hgrn(qs, v, ff, fb, sg, gain, n_ctx):
    bsz, t_all, width = qs.shape
    bw = HG_HEADS_PER_STEP * HG_DK
    spec = pl.BlockSpec((None, t_all, bw), lambda b, h: (b, 0, h))
    return pl.pallas_call(
        functools.partial(_hgrn_kernel, n_ctx=n_ctx),
        grid=(bsz, width // bw),
        in_specs=[spec] * 5 + [pl.BlockSpec((1, HG_DK), lambda b, h: (0, 0))],
        out_specs=spec,
        out_shape=jax.ShapeDtypeStruct((bsz, t_all, width), BF16),
        scratch_shapes=[pltpu.VMEM((t_all, bw), F32), pltpu.VMEM((t_all, bw), F32),
                        pltpu.VMEM((2, HG_HEADS_PER_STEP, HG_CHUNK, HG_CHUNK), F32)],
        compiler_params=_params("parallel", "parallel"),
        name="hgrn2_mixer",
    )(qs, v, ff, fb, sg, gain)


def _att_kernel(q_ref, k_ref, v_ref, o_ref, *, n_ctx):
    qi = pl.program_id(2)
    n_ctx_tiles = n_ctx // q_ref.shape[0]

    def run(nk):
        k = k_ref[0:nk, :]
        v = v_ref[0:nk, :]
        heads = [slice(g * ATT_HD, (g + 1) * ATT_HD) for g in range(ATT_GROUP)]
        s_next = _dot_nt(q_ref[:, heads[0]], k)
        for g, sl in enumerate(heads):
            s = s_next
            if g + 1 < ATT_GROUP:
                s_next = _dot_nt(q_ref[:, heads[g + 1]], k)
            p = jnp.exp(s - jnp.max(s, axis=-1, keepdims=True))
            den = jnp.sum(p, axis=-1, keepdims=True)
            o_ref[:, sl] = (_dot(p.astype(BF16), v) / den).astype(BF16)

    @pl.when(qi < n_ctx_tiles)
    def _():
        run(n_ctx)

    @pl.when(qi >= n_ctx_tiles)
    def _():
        run(k_ref.shape[0])


def _attention(q, k, v, n_ctx):
    bsz, t_all, nq = q.shape
    gw = ATT_GROUP * ATT_HD
    return pl.pallas_call(
        functools.partial(_att_kernel, n_ctx=n_ctx),
        grid=(bsz, ATT_KV_HEADS, t_all // TM),
        in_specs=[pl.BlockSpec((None, TM, gw), lambda b, h, i: (b, i, h)),
                  pl.BlockSpec((None, t_all, ATT_HD), lambda b, h, i: (b, 0, h)),
                  pl.BlockSpec((None, t_all, ATT_HD), lambda b, h, i: (b, 0, h))],
        out_specs=pl.BlockSpec((None, TM, gw), lambda b, h, i: (b, i, h)),
        out_shape=jax.ShapeDtypeStruct((bsz, t_all, nq), BF16),
        compiler_params=_params("parallel", "parallel", "arbitrary"),
        name="gqa_mixer",
    )(q, k, v)


def _lru_kernel(x_ref, gl_ref, cw_ref, cb_ref, wa_ref, ba_ref, wx_ref, bx_ref, lam_ref, y_ref,
                af_ref, uf_ref, ab_ref, ub_ref, *, n_ctx):
    t_all, w = x_ref.shape
    c = SEQ_CHUNK
    ns = V7X_SUBLANES
    nv = c // ns
    n_chunks = t_all // c
    n_ctx_chunks = n_ctx // c
    sub = lax.broadcasted_iota(jnp.int32, (ns, 1), 0)

    def rows_of(chunk):
        return pl.ds(pl.multiple_of(chunk * c, c), c)

    def vreg_rows(chunk, v):
        return pl.ds(pl.multiple_of(chunk * c + v * ns, ns), ns)

    a_refs = (af_ref, ab_ref)
    u_refs = (uf_ref, ub_ref)
    rate = [(-0.5 * LRU_C) * jax.nn.softplus(-lam_ref[d:d + 1, :]) for d in range(2)]
    half_ba = [0.5 * ba_ref[d:d + 1, :] for d in range(2)]
    half_bx = [0.5 * bx_ref[d:d + 1, :] for d in range(2)]

    def one_later(cur, prev):
        return pltpu.roll(jnp.where(sub == ns - 1, prev, cur), 1, 0)

    def one_earlier(cur, nxt):
        return pltpu.roll(jnp.where(sub == 0, nxt, cur), ns - 1, 0)

    def gates(i, carry):
        x3 = x_ref[rows_of(i), :].reshape(nv, ns, w)
        has_prev = jnp.logical_and(i > 0, i != n_ctx_chunks)
        has_next = jnp.logical_and(i < n_chunks - 1, i != n_ctx_chunks - 1)
        ip = jnp.maximum(i - 1, 0)
        inx = jnp.minimum(i + 1, n_chunks - 1)
        prev_1 = jnp.where(has_prev, x_ref[vreg_rows(ip, nv - 1), :], 0.0)
        prev_2 = jnp.where(has_prev, x_ref[vreg_rows(ip, nv - 2), :], 0.0)
        next_0 = jnp.where(has_next, x_ref[vreg_rows(inx, 0), :], 0.0)
        ext = jnp.concatenate([one_later(x3[nv - 2], prev_2)[None], one_later(x3[nv - 1], prev_1)[None],
                               x3, one_earlier(x3[0], next_0)[None]], axis=0)
        xc = cb_ref[...] + ext[0:nv] * cw_ref[0:1, :]
        for j in range(1, LRU_CONV):
            xc = xc + ext[j:j + nv] * cw_ref[j:j + 1, :]
        xc = xc.reshape(c, w)
        xcb = xc.astype(BF16)
        half_xc = 0.5 * xc
        for d in range(2):
            th_r = jnp.tanh(_dot(xcb, wa_ref[d]) + half_ba[d])
            th_i = jnp.tanh(_dot(xcb, wx_ref[d]) + half_bx[d])
            log_a = rate[d] * (th_r + 1.0)
            a = jnp.exp(log_a)
            one_minus_a2 = jnp.tanh(log_a) * (-1.0 - a * a)
            a_refs[d][rows_of(i), :] = a
            u_refs[d][rows_of(i), :] = jnp.sqrt(one_minus_a2) * ((th_i + 1.0) * half_xc)
        return carry
    lax.fori_loop(0, n_chunks, gates, 0, unroll=2)

    def chunk_scan(a_ref, u_ref, chunk, h_in, reverse):
        a3 = a_ref[rows_of(chunk), :].reshape(nv, ns, w)
        u3 = u_ref[rows_of(chunk), :].reshape(nv, ns, w)
        order = list(range(nv - 1, -1, -1) if reverse else range(nv))
        h, p = u3[order[0]], a3[order[0]]
        hs, ps = {order[0]: h}, {order[0]: p}
        for v in order[1:]:
            h = a3[v] * h + u3[v]
            p = a3[v] * p
            hs[v], ps[v] = h, p
        s = 1
        while s < ns:
            sh = (ns - s) if reverse else s
            keep = (sub + s < ns) if reverse else (sub >= s)
            h = h + p * jnp.where(keep, pltpu.roll(h, sh, 0), 0.0)
            p = p * jnp.where(keep, pltpu.roll(p, sh, 0), 1.0)
            s *= 2
        end = h + p * h_in
        if reverse:
            enter = jnp.where(sub == ns - 1, h_in, pltpu.roll(end, ns - 1, 0))
            h_out = end[0:1, :]
        else:
            enter = jnp.where(sub == 0, h_in, pltpu.roll(end, 1, 0))
            h_out = end[ns - 1:ns, :]
        full = jnp.concatenate([(hs[v] + ps[v] * enter)[None] for v in range(nv)], axis=0)
        u_ref[rows_of(chunk), :] = full.reshape(c, w)
        return h_out

    def scan_body(i, carry):
        hf, hb = carry
        hf = chunk_scan(af_ref, uf_ref, i, hf, False)
        ib = jnp.where(i < n_ctx_chunks, n_ctx_chunks - 1 - i, n_chunks - 1 - (i - n_ctx_chunks))
        hb = chunk_scan(ab_ref, ub_ref, ib, hb, True)
        return hf, hb
    zero = jnp.zeros((1, w), F32)
    lax.fori_loop(0, n_chunks, scan_body, (zero, zero))

    def out(i, carry):
        rows = rows_of(i)
        y_ref[rows, :] = ((uf_ref[rows, :] + ub_ref[rows, :]) * gl_ref[rows, :].astype(F32)).astype(BF16)
        return carry
    lax.fori_loop(0, n_chunks, out, 0)


def _rglru(lx, lg, conv_w, conv_b, w_a, b_a, w_x, b_x, lam, n_ctx):
    bsz, t_all, width = lx.shape
    bw = width // LRU_BLOCKS
    seq = pl.BlockSpec((None, t_all, bw), lambda b, j: (b, 0, j))
    vec = lambda rows: pl.BlockSpec((rows, bw), lambda b, j: (0, j))
    wspec = pl.BlockSpec((2, None, bw, bw), lambda b, j: (0, j, 0, 0))
    return pl.pallas_call(
        functools.partial(_lru_kernel, n_ctx=n_ctx),
        grid=(bsz, LRU_BLOCKS),
        in_specs=[seq, seq, vec(LRU_CONV), vec(1), wspec, vec(2), wspec, vec(2), vec(2)],
        out_specs=seq,
        out_shape=jax.ShapeDtypeStruct((bsz, t_all, width), BF16),
        scratch_shapes=[pltpu.VMEM((t_all, bw), F32)] * 4,
        compiler_params=_params("parallel", "parallel"),
        name="rglru_mixer",
    )(lx, lg, conv_w, conv_b, w_a, b_a, w_x, b_x, lam)


def _merge_kernel(x_ref, a_ref, b_ref, r_ref, mg_ref, wbr_ref, wo_ref, gate_ref, o_ref):
    d = x_ref.shape[-1]
    y = None
    for i, br in enumerate((a_ref, b_ref, r_ref)):
        term = mg_ref[:, i * d:(i + 1) * d].astype(F32) * _dot(br[...], wbr_ref[i])
        y = term if y is None else y + term
    o_ref[...] = x_ref[...] + gate_ref[...] * _dot(y.astype(BF16), wo_ref[...])


def _ffn_kernel(x_ref, g_ref, sh_ref, sc_ref, gate_ref, wg_ref, wu_ref, wd_ref, *rest, final):
    x = x_ref[...]
    h = _norm_mod(x, g_ref[...], sh_ref[...], sc_ref[...]).astype(BF16)
    act = (_silu(_dot(h, wg_ref[...])) * _dot(h, wu_ref[...])).astype(BF16)
    y = x + gate_ref[...] * _dot(act, wd_ref[...])
    if final:
        gf_ref, o_ref = rest
        ms = jnp.mean(y * y, axis=-1, keepdims=True)
        y = y * lax.rsqrt(ms + EPS) * gf_ref[...]
    else:
        (o_ref,) = rest
    o_ref[...] = y


def _merge_ffn(x, ya, yb, yr, mg, mods, w_br, w_o, g2, w_g, w_u, w_d, n_ctx_tiles, g_final):
    bsz, t_all, d = x.shape
    final = g_final is not None
    toff = n_ctx_tiles if final else 0
    n_t = t_all // TM - toff
    tile = lambda w: pl.BlockSpec((None, TM, w), lambda b, t: (b, t + toff, 0))
    out_tile = pl.BlockSpec((None, TM, d), lambda b, t: (b, t, 0))
    mod = lambda which: _mod_spec(mods, which, n_ctx_tiles, toff)
    full = lambda a: pl.BlockSpec(a.shape, lambda b, t: (0,) * a.ndim, pipeline_mode=pl.Buffered(1))
    out_sds = jax.ShapeDtypeStruct((bsz, n_t * TM, d), F32)
    x1 = pl.pallas_call(
        _merge_kernel,
        grid=(bsz, n_t),
        in_specs=[tile(d), tile(d), tile(d), tile(d), tile(N_BRANCH * d), full(w_br), full(w_o), mod(2)],
        out_specs=out_tile,
        out_shape=out_sds,
        compiler_params=_params("parallel", "parallel"),
        name="merge",
    )(x, ya, yb, yr, mg, w_br, w_o, mods)
    x1_tile = pl.BlockSpec((None, TM, d), lambda b, t: (b, t, 0))
    args = [x1, g2, mods, mods, mods, w_g, w_u, w_d]
    specs = [x1_tile, _vec_spec(d), mod(3), mod(4), mod(5), full(w_g), full(w_u), full(w_d)]
    if final:
        args.append(g_final)
        specs.append(_vec_spec(d))
    return pl.pallas_call(
        functools.partial(_ffn_kernel, final=final),
        grid=(bsz, n_t),
        in_specs=specs,
        out_specs=out_tile,
        out_shape=out_sds,
        compiler_params=_params("parallel", "parallel"),
        name="ffn",
    )(*args)


def _rope_tables(n_ctx, n_lat):
    rows = n_lat // GRID_W
    r = np.repeat(np.arange(rows), GRID_W).astype(np.float32)
    col = np.tile(np.arange(GRID_W), rows).astype(np.float32)
    half = ATT_HD // 2
    inv = np.power(np.float32(ROPE_THETA), -np.arange(0, half, 2, dtype=np.float32) / np.float32(half))
    ar = jnp.asarray(r[:, None] * inv[None, :], F32)
    ac = jnp.asarray(col[:, None] * inv[None, :], F32)
    cr, sr, cc, sc = jnp.cos(ar), jnp.sin(ar), jnp.cos(ac), jnp.sin(ac)
    cos_l = jnp.concatenate([cr, cr, cc, cc], axis=-1)
    sin_l = jnp.concatenate([-sr, sr, -sc, sc], axis=-1)
    cos_t = jnp.concatenate([jnp.ones((n_ctx, ATT_HD), F32), cos_l], axis=0)
    sin_t = jnp.concatenate([jnp.zeros((n_ctx, ATT_HD), F32), sin_l], axis=0)
    return _to_chunk_layout(cos_t[None])[0], _to_chunk_layout(sin_t[None])[0]


def kernel(x, c, ctx, c_ctx, w_mod, b_mod, norm1, w_in, hg_lb_logits, hg_out_norm, att_q_norm, att_k_norm, lru_conv_w, lru_conv_b, lru_w_a, lru_b_a, lru_w_x, lru_b_x, lru_lambda, w_branch, w_out, norm2, w_ffn_in, w_ffn_out, norm_final):
    bsz, n_lat, d = x.shape
    n_ctx = ctx.shape[1]
    depth = w_in.shape[0]
    assert n_ctx % TM == 0 and n_lat % TM == 0 and n_lat % GRID_W == 0
    n_ctx_tiles = n_ctx // TM
    hg_w = HG_HEADS * HG_DK
    ffn_h = w_ffn_out.shape[1]

    c_all = jnp.concatenate([c, c_ctx[None, :]], axis=0)
    mods = _mod_vectors(c_all, w_mod, b_mod).reshape(depth, bsz + 1, N_MOD, 1, d)
    lb = _lower_bounds(hg_lb_logits)
    cos_t, sin_t = _rope_tables(n_ctx, n_lat)

    xs = _stream_in(ctx, x)
    for l in range(depth):
        last = l == depth - 1
        w_hg = w_in[l, :, :5 * hg_w].astype(BF16)
        w_rest = w_in[l, :, 5 * hg_w:].astype(BF16)
        hg, rest = _inproj(xs, mods[l], norm1[l][None, :], w_hg, w_rest,
                           lb[0, l][None, :], lb[1, l][None, :],
                           att_q_norm[l][None, :], att_k_norm[l][None, :], cos_t, sin_t, n_ctx_tiles)
        qs, hv, ff, fb, sg = hg
        aq, ak, av, lx, lg, mg = rest
        ya = _hgrn(qs, hv, ff, fb, sg, hg_out_norm[l][None, :], n_ctx)
        yb = _attention(aq, ak, av, n_ctx)
        yr = _rglru(lx, lg, lru_conv_w[l], lru_conv_b[l][None, :], (0.5 * lru_w_a[l]).astype(BF16),
                    lru_b_a[l], (0.5 * lru_w_x[l]).astype(BF16), lru_b_x[l], lru_lambda[l], n_ctx)
        xs = _merge_ffn(xs, ya, yb, yr, mg, mods[l], w_branch[l].astype(BF16), w_out[l].astype(BF16),
                        norm2[l][None, :], w_ffn_in[l, :, :ffn_h].astype(BF16),
                        w_ffn_in[l, :, ffn_h:].astype(BF16), w_ffn_out[l].astype(BF16),
                        n_ctx_tiles, norm_final[None, :] if last else None)
    return _stream_out(xs)
```

```python
import functools

import numpy as np
import jax
import jax.numpy as jnp
from jax import lax
from jax.experimental import pallas as pl
from jax.experimental.pallas import tpu as pltpu

F32 = jnp.float32
BF16 = jnp.bfloat16

EPS = 1e-6
GRID_W = 64
HG_HEADS = 8
HG_DK = 128
ATT_HEADS = 8
ATT_KV_HEADS = 2
ATT_GROUP = ATT_HEADS // ATT_KV_HEADS
ATT_HD = 128
ROPE_THETA = 10000.0
LRU_BLOCKS = 4
LRU_CONV = 4
LRU_C = 8.0
N_BRANCH = 3
N_MOD = 6

V7X_VMEM_BYTES = 64 * 1024 * 1024
V7X_LANES = 128
V7X_SUBLANES = 8
VMEM_LIMIT = (V7X_VMEM_BYTES * 3) // 4

TM = 256
SEQ_CHUNK = 128
HG_CHUNK = SEQ_CHUNK
HG_HEADS_PER_STEP = 2


def _silu(x):
    return x * jax.nn.sigmoid(x)


def _gelu_tanh(x):
    return 0.5 * x * (1.0 + jnp.tanh(np.sqrt(2.0 / np.pi) * (x + 0.044715 * (x * x * x))))


def _dot(a, b):
    return jnp.dot(a, b, preferred_element_type=F32)


def _dot_nt(a, b):
    return lax.dot_general(a, b, (((1,), (1,)), ((), ())), preferred_element_type=F32)


def _dot_tn(a, b):
    return lax.dot_general(a, b, (((0,), (0,)), ((), ())), preferred_element_type=F32)


def _params(*sem):
    return pltpu.CompilerParams(dimension_semantics=sem, vmem_limit_bytes=VMEM_LIMIT)


def _mod_kernel(c_ref, w_ref, b_ref, o_ref):
    ca = _silu(c_ref[...]).astype(BF16)
    o_ref[...] = _dot(ca, w_ref[...].astype(BF16)) + b_ref[...]


def _mod_vectors(c_all, w_mod, b_mod):
    depth, d, n = w_mod.shape
    rows = c_all.shape[0]
    tn = n // 4
    return pl.pallas_call(
        _mod_kernel,
        grid=(depth, n // tn),
        in_specs=[pl.BlockSpec((rows, d), lambda l, j: (0, 0)),
                  pl.BlockSpec((None, d, tn), lambda l, j: (l, 0, j)),
                  pl.BlockSpec((None, 1, tn), lambda l, j: (l, 0, j))],
        out_specs=pl.BlockSpec((None, rows, tn), lambda l, j: (l, 0, j)),
        out_shape=jax.ShapeDtypeStruct((depth, rows, n), F32),
        compiler_params=_params("parallel", "parallel"),
        name="mod_vectors",
    )(c_all, w_mod, b_mod.reshape(depth, 1, n))


def _lb_kernel(x_ref, o_ref):
    depth = x_ref.shape[1]
    rows = [x_ref[:, l, :] for l in range(depth)]
    m = functools.reduce(jnp.maximum, rows)
    e = [jnp.exp(r - m) for r in rows]
    inv = 1.0 / functools.reduce(lambda a, b: a + b, e)
    acc = jnp.zeros_like(rows[0])
    o_ref[:, 0, :] = acc
    for l in range(1, depth):
        acc = acc + e[l] * inv
        o_ref[:, l, :] = acc


def _lower_bounds(logits):
    return pl.pallas_call(
        _lb_kernel,
        out_shape=jax.ShapeDtypeStruct(logits.shape, F32),
        name="hg_lower_bounds",
    )(logits.astype(F32))


def _norm_mod(x, g, shift, scale):
    ms = jnp.mean(x * x, axis=-1, keepdims=True)
    y = x * lax.rsqrt(ms + EPS) * g
    return y * (1.0 + scale) + shift


def _inproj_hg_kernel(x_ref, g_ref, sh_ref, sc_ref, w_ref, lbf_ref, lbb_ref,
                      qs_ref, v_ref, ff_ref, fb_ref, sg_ref):
    h = _norm_mod(x_ref[...], g_ref[...], sh_ref[...], sc_ref[...]).astype(BF16)
    w = qs_ref.shape[-1]
    z = _dot(h, w_ref[:, 0:w])
    qs_ref[...] = (_silu(z) * (HG_DK ** -0.5)).astype(BF16)
    v_ref[...] = _dot(h, w_ref[:, w:2 * w]).astype(BF16)
    for k, (lb_ref, f_ref) in enumerate(((lbf_ref, ff_ref), (lbb_ref, fb_ref))):
        lb = lb_ref[...]
        z = _dot(h, w_ref[:, (2 + k) * w:(3 + k) * w])
        f_ref[...] = lb + (1.0 - lb) * jax.nn.sigmoid(z)
    sg_ref[...] = _silu(_dot(h, w_ref[:, 4 * w:5 * w])).astype(BF16)


def _inproj_rest_kernel(x_ref, g_ref, sh_ref, sc_ref, w_ref, qg_ref, kg_ref, cos_ref, sin_ref,
                        q_ref, k_ref, v_ref, lx_ref, lg_ref, mg_ref):
    h = _norm_mod(x_ref[...], g_ref[...], sh_ref[...], sc_ref[...]).astype(BF16)
    cos = cos_ref[...]
    sin = sin_ref[...]
    lane = lax.broadcasted_iota(jnp.int32, cos.shape, 1)
    quarter = ATT_HD // 4
    first = (lane % (2 * quarter)) < quarter

    def norm_rope(z, gain, scale):
        ms = jnp.mean(z * z, axis=-1, keepdims=True)
        y = z * lax.rsqrt(ms + EPS) * gain
        swapped = jnp.where(first, pltpu.roll(y, ATT_HD - quarter, 1), pltpu.roll(y, quarter, 1))
        return ((y * cos + swapped * sin) * scale).astype(BF16)

    nq = q_ref.shape[-1]
    nkv = k_ref.shape[-1]
    nl = lx_ref.shape[-1]
    z = _dot(h, w_ref[:, 0:nq])
    for i in range(nq // ATT_HD):
        sl = slice(i * ATT_HD, (i + 1) * ATT_HD)
        q_ref[:, sl] = norm_rope(z[:, sl], qg_ref[...], ATT_HD ** -0.5 * np.log2(np.e))
    z = _dot(h, w_ref[:, nq:nq + nkv])
    for i in range(nkv // ATT_HD):
        sl = slice(i * ATT_HD, (i + 1) * ATT_HD)
        k_ref[:, sl] = norm_rope(z[:, sl], kg_ref[...], 1.0)
    off = nq + nkv
    v_ref[...] = _dot(h, w_ref[:, off:off + nkv]).astype(BF16)
    off += nkv
    lx_ref[...] = _dot(h, w_ref[:, off:off + nl])
    off += nl
    lg_ref[...] = _gelu_tanh(_dot(h, w_ref[:, off:off + nl])).astype(BF16)
    off += nl
    mg_ref[...] = jax.nn.sigmoid(_dot(h, w_ref[:, off:off + mg_ref.shape[-1]])).astype(BF16)


def _tile_spec(width, dtype_unused=None):
    return pl.BlockSpec((None, TM, width), lambda b, t: (b, t, 0))


def _vec_spec(width):
    return pl.BlockSpec((1, width), lambda b, t: (0, 0))


def _mod_spec(mods, which, n_ctx_tiles, toff=0):
    ctx_row = mods.shape[0] - 1
    d = mods.shape[-1]
    return pl.BlockSpec((None, None, 1, d),
                        lambda b, t: (jnp.where(t + toff < n_ctx_tiles, ctx_row, b), which, 0, 0))


def _inproj(x, mods, g1, w_hg, w_rest, lbf, lbb, qg, kg, cos_t, sin_t, n_ctx_tiles):
    bsz, t_all, d = x.shape
    grid = (bsz, t_all // TM)
    hw = w_hg.shape[1] // 5
    nq = ATT_HEADS * ATT_HD
    nkv = ATT_KV_HEADS * ATT_HD
    common = [_tile_spec(d), _vec_spec(d), _mod_spec(mods, 0, n_ctx_tiles),
              _mod_spec(mods, 1, n_ctx_tiles)]
    sds = lambda w, dt: jax.ShapeDtypeStruct((bsz, t_all, w), dt)
    hg = pl.pallas_call(
        _inproj_hg_kernel,
        grid=grid,
        in_specs=common + [pl.BlockSpec(w_hg.shape, lambda b, t: (0, 0)), _vec_spec(hw), _vec_spec(hw)],
        out_specs=[_tile_spec(hw)] * 5,
        out_shape=[sds(hw, BF16), sds(hw, BF16), sds(hw, F32), sds(hw, F32), sds(hw, BF16)],
        compiler_params=_params("parallel", "parallel"),
        name="inproj_hgrn",
    )(x, g1, mods, mods, w_hg, lbf, lbb)
    nl = hw
    ng = w_rest.shape[1] - nq - 2 * nkv - 2 * nl
    rest = pl.pallas_call(
        _inproj_rest_kernel,
        grid=grid,
        in_specs=common + [pl.BlockSpec(w_rest.shape, lambda b, t: (0, 0)),
                           _vec_spec(ATT_HD), _vec_spec(ATT_HD),
                           pl.BlockSpec((TM, ATT_HD), lambda b, t: (t, 0)),
                           pl.BlockSpec((TM, ATT_HD), lambda b, t: (t, 0))],
        out_specs=[_tile_spec(nq), _tile_spec(nkv), _tile_spec(nkv), _tile_spec(nl), _tile_spec(nl),
                   _tile_spec(ng)],
        out_shape=[sds(nq, BF16), sds(nkv, BF16), sds(nkv, BF16), sds(nl, F32), sds(nl, BF16),
                   sds(ng, BF16)],
        compiler_params=_params("parallel", "parallel"),
        name="inproj_rest",
    )(x, g1, mods, mods, w_rest, qg, kg, cos_t, sin_t)
    return hg, rest


def _chunk_token(r):
    nv = SEQ_CHUNK // V7X_SUBLANES
    return (r % V7X_SUBLANES) * nv + r // V7X_SUBLANES


def _to_chunk_layout(a):
    nv = SEQ_CHUNK // V7X_SUBLANES
    shp = a.shape
    a = a.reshape(shp[0], shp[1] // SEQ_CHUNK, V7X_SUBLANES, nv, *shp[2:])
    return jnp.swapaxes(a, 2, 3).reshape(shp)


def _split3(g):
    g1 = g.astype(BF16)
    r1 = g - g1.astype(F32)
    g2 = r1.astype(BF16)
    g3 = (r1 - g2.astype(F32)).astype(BF16)
    return jnp.concatenate([g1, g2, g3], axis=-1)


def _move_tokens(x, to_chunk):
    rows, d = x.shape
    nv = SEQ_CHUNK // V7X_SUBLANES
    r = lax.broadcasted_iota(jnp.int32, (rows, rows), 0)
    i = r % SEQ_CHUNK
    src = _chunk_token(i) if to_chunk else (i % nv) * V7X_SUBLANES + i // nv
    pick = lax.broadcasted_iota(jnp.int32, (rows, rows), 1) == (r - i) + src
    parts = _dot(jnp.where(pick, 1.0, 0.0).astype(BF16), _split3(x))
    return (parts[:, 0:d] + parts[:, d:2 * d]) + parts[:, 2 * d:3 * d]


def _stream_in_kernel(ctx_ref, x_ref, o_ref, *, n_ctx_tiles):
    @pl.when(pl.program_id(1) < n_ctx_tiles)
    def _():
        o_ref[...] = _move_tokens(ctx_ref[...], True)

    @pl.when(pl.program_id(1) >= n_ctx_tiles)
    def _():
        o_ref[...] = _move_tokens(x_ref[...], True)


def _stream_out_kernel(x_ref, o_ref):
    o_ref[...] = _move_tokens(x_ref[...], False)


def _stream_in(ctx, x):
    bsz, n_ctx, d = ctx.shape
    n_ctx_tiles = n_ctx // TM
    n_tiles = n_ctx_tiles + x.shape[1] // TM
    return pl.pallas_call(
        functools.partial(_stream_in_kernel, n_ctx_tiles=n_ctx_tiles),
        grid=(bsz, n_tiles),
        in_specs=[pl.BlockSpec((None, TM, d), lambda b, t: (b, jnp.minimum(t, n_ctx_tiles - 1), 0)),
                  pl.BlockSpec((None, TM, d), lambda b, t: (b, jnp.maximum(t - n_ctx_tiles, 0), 0))],
        out_specs=pl.BlockSpec((None, TM, d), lambda b, t: (b, t, 0)),
        out_shape=jax.ShapeDtypeStruct((bsz, n_tiles * TM, d), x.dtype),
        compiler_params=_params("parallel", "arbitrary"),
        name="stream_in",
    )(ctx, x)


def _stream_out(xs):
    bsz, n, d = xs.shape
    blk = pl.BlockSpec((None, TM, d), lambda b, t: (b, t, 0))
    return pl.pallas_call(
        _stream_out_kernel,
        grid=(bsz, n // TM),
        in_specs=[blk],
        out_specs=blk,
        out_shape=jax.ShapeDtypeStruct((bsz, n, d), xs.dtype),
        compiler_params=_params("parallel", "parallel"),
        name="stream_out",
    )(xs)


def _block_row(y, blk, ref_row):
    pos = lax.broadcasted_iota(jnp.int32, (V7X_SUBLANES, 1), 0) % blk
    s = 1
    while ref_row + s < blk:
        y = jnp.where((pos >= ref_row + s) & (pos < ref_row + 2 * s), pltpu.roll(y, s, 0), y)
        s *= 2
    s = 1
    while ref_row - s >= 0:
        y = jnp.where((pos <= ref_row - s) & (pos > ref_row - 2 * s),
                      pltpu.roll(y, V7X_SUBLANES - s, 0), y)
        s *= 2
    return y


def _level_exponent(b, blk, reverse):
    c, width = b.shape
    nv = c // V7X_SUBLANES
    half = blk // 2
    b3 = b.reshape(nv, V7X_SUBLANES, width)
    if blk <= nv:
        ref = half - 1 if reverse else half
        d = []
        for v0 in range(0, nv, blk):
            r = b3[v0 + ref:v0 + ref + 1]
            lo, hi = b3[v0:v0 + half], b3[v0 + half:v0 + blk]
            d += [lo - r, r - hi] if reverse else [r - lo, hi - r]
        d = jnp.concatenate(d, axis=0)
    else:
        sub_blk = blk // nv
        if reverse:
            ref = _block_row(b3[nv - 1], sub_blk, sub_blk // 2 - 1)
        else:
            ref = _block_row(b3[0], sub_blk, sub_blk // 2)
        d = _neg_abs(b3 - ref[None])
    return d.reshape(c, width)


def _neg_abs(x):
    bits = lax.bitcast_convert_type(x, jnp.uint32) | jnp.uint32(0x80000000)
    return lax.bitcast_convert_type(bits, F32)


def _hg_chunks(probs, pair_xor):
    c, width = probs[0]["q"].shape
    heads = [slice(h * HG_DK, (h + 1) * HG_DK) for h in range(width // HG_DK)]
    for p in probs:
        p["kk"] = (1.0 - p["f"]).astype(BF16)
        b = p["b"]
        qe = p["q"] * jnp.exp2(b).astype(BF16)
        p["o"] = [_dot_nt(qe[:, h], s.astype(BF16)) for h, s in zip(heads, p["st"])]
        b_end = b[0:1, :] if p["reverse"] else b[c - 1:c, :]
        ke = p["kk"] * jnp.exp2(b_end - b).astype(BF16)
        st_scale = jnp.exp2(b_end)
        p["st"] = [s * st_scale[:, h] + _dot_tn(p["v"][:, h], ke[:, h]) for h, s in zip(heads, p["st"])]

    blk = c
    while blk >= 2:
        for p in probs:
            e = jnp.exp2(_level_exponent(p["b"], blk, p["reverse"])).astype(BF16)
            qw = p["q"] * e
            kw = p["kk"] * e
            for n, h in enumerate(heads):
                pltpu.store(p["pairs"].at[n], _dot_nt(qw[:, h], kw[:, h]),
                            mask=None if blk == c else pair_xor < blk)
        blk //= 2
    for p in probs:
        pairs = []
        for n, h in enumerate(heads):
            pltpu.store(p["pairs"].at[n], _dot_nt(p["q"][:, h], p["kk"][:, h]), mask=pair_xor == 0)
            pairs.append(jnp.where(p["causal"], p["pairs"][n], 0.0).astype(BF16))
        p["a"] = pairs
    return ([jnp.concatenate(p["o"], axis=-1) for p in probs], [p["a"] for p in probs],
            [p["st"] for p in probs])


def _hg_running_decay(f, tri):
    width = f.shape[-1]
    g = jnp.log2(f)
    g1 = g.astype(BF16)
    g2 = (g - g1.astype(F32)).astype(BF16)
    parts = _dot(tri, jnp.concatenate([g1, g2], axis=-1))
    return parts[:, 0:width] + parts[:, width:2 * width]


def _hgrn_kernel(qs_ref, v_ref, ff_ref, fb_ref, sg_ref, gain_ref, y_ref, of_ref, ob_ref, pair_ref,
                 *, n_ctx):
    t_all, width = qs_ref.shape
    kdim = HG_DK
    c = HG_CHUNK
    n_chunks = t_all // c
    n_ctx_chunks = n_ctx // c
    row = _chunk_token(lax.broadcasted_iota(jnp.int32, (c, c), 0))
    col = _chunk_token(lax.broadcasted_iota(jnp.int32, (c, c), 1))
    pair_xor = row ^ col
    tri_f = jnp.where(col <= row, 1.0, 0.0).astype(BF16)
    tri_b = jnp.where(col >= row, 1.0, 0.0).astype(BF16)

    n_heads = width // kdim
    f_refs = (ff_ref, fb_ref)
    o_refs = (of_ref, ob_ref)
    tris = (tri_f, tri_b)

    def chunk_ids(i):
        i = jnp.minimum(i, n_chunks - 1)
        ib = jnp.where(i < n_ctx_chunks, n_ctx_chunks - 1 - i, n_chunks - 1 - (i - n_ctx_chunks))
        return (i, ib)

    def rows_of(chunk):
        return pl.ds(pl.multiple_of(chunk * c, c), c)

    def finish(chunks, pairs):
        for o_ref, chunk, a in zip(o_refs, chunks, pairs):
            rows = rows_of(chunk)
            add = [_dot(a[h], v_ref[rows, h * kdim:(h + 1) * kdim]) for h in range(n_heads)]
            o_ref[rows, :] += jnp.concatenate(add, axis=-1)

    def body(i, carry):
        sts, decays, prev_chunks, prev_pairs = carry
        finish(prev_chunks, prev_pairs)
        chunks = chunk_ids(i)
        probs = [dict(q=qs_ref[rows_of(ch), :], v=v_ref[rows_of(ch), :], f=f_ref[rows_of(ch), :],
                      b=b, st=st, reverse=d == 1, causal=(col >= row) if d else (col <= row),
                      pairs=pair_ref.at[d])
                 for d, (ch, f_ref, b, st) in enumerate(zip(chunks, f_refs, decays, sts))]
        outs, pairs, sts = _hg_chunks(probs, pair_xor)
        for o_ref, ch, o in zip(o_refs, chunks, outs):
            o_ref[rows_of(ch), :] = o
        decays = [_hg_running_decay(f_ref[rows_of(ch), :], tri)
                  for ch, f_ref, tri in zip(chunk_ids(i + 1), f_refs, tris)]
        return sts, decays, chunks, pairs

    first = chunk_ids(0)
    for o_ref, ch in zip(o_refs, first):
        o_ref[rows_of(ch), :] = jnp.zeros((c, width), F32)
    zero_st = [[jnp.zeros((kdim, kdim), F32)] * n_heads] * 2
    zero_pairs = [[jnp.zeros((c, c), BF16)] * n_heads] * 2
    decays0 = [_hg_running_decay(f_ref[rows_of(ch), :], tri)
               for ch, f_ref, tri in zip(first, f_refs, tris)]
    _, _, last_chunks, last_pairs = lax.fori_loop(
        0, n_chunks, body, (zero_st, decays0, first, zero_pairs))
    finish(last_chunks, last_pairs)

    def readout(i, carry):
        rows = pl.ds(pl.multiple_of(i * c, c), c)
        for h in range(width // kdim):
            sl = slice(h * kdim, (h + 1) * kdim)
            o = of_ref[rows, sl] + ob_ref[rows, sl]
            ms = jnp.mean(o * o, axis=-1, keepdims=True)
            y = o * lax.rsqrt(ms + EPS) * gain_ref[...]
            y_ref[rows, sl] = (y * sg_ref[rows, sl].astype(F32)).astype(BF16)
        return carry
    lax.fori_loop(0, n_chunks, readout, 0, unroll=2)


def _hgrn(qs, v, ff, fb, sg, gain, n_ctx):
    bsz, t_all, width = qs.shape
    bw = HG_HEADS_PER_STEP * HG_DK
    spec = pl.BlockSpec((None, t_all, bw), lambda b, h: (b, 0, h))
    return pl.pallas_call(
        functools.partial(_hgrn_kernel, n_ctx=n_ctx),
        grid=(bsz, width // bw),
        in_specs=[spec] * 5 + [pl.BlockSpec((1, HG_DK), lambda b, h: (0, 0))],
        out_specs=spec,
        out_shape=jax.ShapeDtypeStruct((bsz, t_all, width), BF16),
        scratch_shapes=[pltpu.VMEM((t_all, bw), F32), pltpu.VMEM((t_all, bw), F32),
                        pltpu.VMEM((2, HG_HEADS_PER_STEP, HG_CHUNK, HG_CHUNK), F32)],
        compiler_params=_params("parallel", "parallel"),
        name="hgrn2_mixer",
    )(qs, v, ff, fb, sg, gain)


def _att_kernel(q_ref, k_ref, v_ref, o_ref, *, n_ctx):
    qi = pl.program_id(2)
    n_ctx_tiles = n_ctx // q_ref.shape[0]

    def run(nk):
        k = k_ref[0:nk, :]
        v = v_ref[0:nk, :]
        heads = [slice(g * ATT_HD, (g + 1) * ATT_HD) for g in range(ATT_GROUP)]
        s_next = _dot_nt(q_ref[:, heads[0]], k)
        for g, sl in enumerate(heads):
            s = s_next
            if g + 1 < ATT_GROUP:
                s_next = _dot_nt(q_ref[:, heads[g + 1]], k)
            p = jnp.exp2(s - jnp.max(s, axis=-1, keepdims=True))
            den = jnp.sum(p, axis=-1, keepdims=True)
            o_ref[:, sl] = (_dot(p.astype(BF16), v) / den).astype(BF16)

    @pl.when(qi < n_ctx_tiles)
    def _():
        run(n_ctx)

    @pl.when(qi >= n_ctx_tiles)
    def _():
        run(k_ref.shape[0])


def _attention(q, k, v, n_ctx):
    bsz, t_all, nq = q.shape
    gw = ATT_GROUP * ATT_HD
    return pl.pallas_call(
        functools.partial(_att_kernel, n_ctx=n_ctx),
        grid=(bsz, ATT_KV_HEADS, t_all // TM),
        in_specs=[pl.BlockSpec((None, TM, gw), lambda b, h, i: (b, i, h)),
                  pl.BlockSpec((None, t_all, ATT_HD), lambda b, h, i: (b, 0, h)),
                  pl.BlockSpec((None, t_all, ATT_HD), lambda b, h, i: (b, 0, h))],
        out_specs=pl.BlockSpec((None, TM, gw), lambda b, h, i: (b, i, h)),
        out_shape=jax.ShapeDtypeStruct((bsz, t_all, nq), BF16),
        compiler_params=_params("parallel", "parallel", "arbitrary"),
        name="gqa_mixer",
    )(q, k, v)


def _lru_kernel(x_ref, gl_ref, cw_ref, cb_ref, wa_ref, ba_ref, wx_ref, bx_ref, lam_ref, y_ref,
                af_ref, uf_ref, ab_ref, ub_ref, *, n_ctx):
    t_all, w = x_ref.shape
    c = SEQ_CHUNK
    ns = V7X_SUBLANES
    nv = c // ns
    n_chunks = t_all // c
    n_ctx_chunks = n_ctx // c
    sub = lax.broadcasted_iota(jnp.int32, (ns, 1), 0)

    def rows_of(chunk):
        return pl.ds(pl.multiple_of(chunk * c, c), c)

    def vreg_rows(chunk, v):
        return pl.ds(pl.multiple_of(chunk * c + v * ns, ns), ns)

    a_refs = (af_ref, ab_ref)
    u_refs = (uf_ref, ub_ref)
    rate = [(-0.5 * LRU_C) * jax.nn.softplus(-lam_ref[d:d + 1, :]) for d in range(2)]
    half_ba = [0.5 * ba_ref[d:d + 1, :] for d in range(2)]
    half_bx = [0.5 * bx_ref[d:d + 1, :] for d in range(2)]

    def one_later(cur, prev):
        return pltpu.roll(jnp.where(sub == ns - 1, prev, cur), 1, 0)

    def one_earlier(cur, nxt):
        return pltpu.roll(jnp.where(sub == 0, nxt, cur), ns - 1, 0)

    def gates(i, carry):
        x3 = x_ref[rows_of(i), :].reshape(nv, ns, w)
        has_prev = jnp.logical_and(i > 0, i != n_ctx_chunks)
        has_next = jnp.logical_and(i < n_chunks - 1, i != n_ctx_chunks - 1)
        ip = jnp.maximum(i - 1, 0)
        inx = jnp.minimum(i + 1, n_chunks - 1)
        prev_1 = jnp.where(has_prev, x_ref[vreg_rows(ip, nv - 1), :], 0.0)
        prev_2 = jnp.where(has_prev, x_ref[vreg_rows(ip, nv - 2), :], 0.0)
        next_0 = jnp.where(has_next, x_ref[vreg_rows(inx, 0), :], 0.0)
        ext = jnp.concatenate([one_later(x3[nv - 2], prev_2)[None], one_later(x3[nv - 1], prev_1)[None],
                               x3, one_earlier(x3[0], next_0)[None]], axis=0)
        xc = cb_ref[...] + ext[0:nv] * cw_ref[0:1, :]
        for j in range(1, LRU_CONV):
            xc = xc + ext[j:j + nv] * cw_ref[j:j + 1, :]
        xc = xc.reshape(c, w)
        xcb = xc.astype(BF16)
        half_xc = 0.5 * xc
        for d in range(2):
            th_r = jnp.tanh(_dot(xcb, wa_ref[d]) + half_ba[d])
            th_i = jnp.tanh(_dot(xcb, wx_ref[d]) + half_bx[d])
            log_a = rate[d] * (th_r + 1.0)
            a = jnp.exp(log_a)
            one_minus_a2 = jnp.tanh(log_a) * (-1.0 - a * a)
            a_refs[d][rows_of(i), :] = a
            u_refs[d][rows_of(i), :] = jnp.sqrt(one_minus_a2) * ((th_i + 1.0) * half_xc)
        return carry
    lax.fori_loop(0, n_chunks, gates, 0, unroll=2)

    def chunk_scan(a_ref, u_ref, chunk, h_in, reverse):
        a3 = a_ref[rows_of(chunk), :].reshape(nv, ns, w)
        u3 = u_ref[rows_of(chunk), :].reshape(nv, ns, w)
        order = list(range(nv - 1, -1, -1) if reverse else range(nv))
        h, p = u3[order[0]], a3[order[0]]
        hs, ps = {order[0]: h}, {order[0]: p}
        for v in order[1:]:
            h = a3[v] * h + u3[v]
            p = a3[v] * p
            hs[v], ps[v] = h, p
        s = 1
        while s < ns:
            sh = (ns - s) if reverse else s
            keep = (sub + s < ns) if reverse else (sub >= s)
            h = h + p * jnp.where(keep, pltpu.roll(h, sh, 0), 0.0)
            p = p * jnp.where(keep, pltpu.roll(p, sh, 0), 1.0)
            s *= 2
        end = h + p * h_in
        if reverse:
            enter = jnp.where(sub == ns - 1, h_in, pltpu.roll(end, ns - 1, 0))
            h_out = end[0:1, :]
        else:
            enter = jnp.where(sub == 0, h_in, pltpu.roll(end, 1, 0))
            h_out = end[ns - 1:ns, :]
        full = jnp.concatenate([(hs[v] + ps[v] * enter)[None] for v in range(nv)], axis=0)
        u_ref[rows_of(chunk), :] = full.reshape(c, w)
        return h_out

    def scan_body(i, carry):
        hf, hb = carry
        hf = chunk_scan(af_ref, uf_ref, i, hf, False)
        ib = jnp.where(i < n_ctx_chunks, n_ctx_chunks - 1 - i, n_chunks - 1 - (i - n_ctx_chunks))
        hb = chunk_scan(ab_ref, ub_ref, ib, hb, True)
        return hf, hb
    zero = jnp.zeros((1, w), F32)
    lax.fori_loop(0, n_chunks, scan_body, (zero, zero))

    def out(i, carry):
        rows = rows_of(i)
        y_ref[rows, :] = ((uf_ref[rows, :] + ub_ref[rows, :]) * gl_ref[rows, :].astype(F32)).astype(BF16)
        return carry
    lax.fori_loop(0, n_chunks, out, 0)


def _rglru(lx, lg, conv_w, conv_b, w_a, b_a, w_x, b_x, lam, n_ctx):
    bsz, t_all, width = lx.shape
    bw = width // LRU_BLOCKS
    seq = pl.BlockSpec((None, t_all, bw), lambda b, j: (b, 0, j))
    vec = lambda rows: pl.BlockSpec((rows, bw), lambda b, j: (0, j))
    wspec = pl.BlockSpec((2, None, bw, bw), lambda b, j: (0, j, 0, 0))
    return pl.pallas_call(
        functools.partial(_lru_kernel, n_ctx=n_ctx),
        grid=(bsz, LRU_BLOCKS),
        in_specs=[seq, seq, vec(LRU_CONV), vec(1), wspec, vec(2), wspec, vec(2), vec(2)],
        out_specs=seq,
        out_shape=jax.ShapeDtypeStruct((bsz, t_all, width), BF16),
        scratch_shapes=[pltpu.VMEM((t_all, bw), F32)] * 4,
        compiler_params=_params("parallel", "parallel"),
        name="rglru_mixer",
    )(lx, lg, conv_w, conv_b, w_a, b_a, w_x, b_x, lam)


def _merge_kernel(x_ref, a_ref, b_ref, r_ref, mg_ref, wbr_ref, wo_ref, gate_ref, o_ref):
    d = x_ref.shape[-1]
    y = None
    for i, br in enumerate((a_ref, b_ref, r_ref)):
        term = mg_ref[:, i * d:(i + 1) * d].astype(F32) * _dot(br[...], wbr_ref[i])
        y = term if y is None else y + term
    o_ref[...] = x_ref[...] + gate_ref[...] * _dot(y.astype(BF16), wo_ref[...])


def _ffn_kernel(x_ref, g_ref, sh_ref, sc_ref, gate_ref, wg_ref, wu_ref, wd_ref, *rest, final):
    x = x_ref[...]
    h = _norm_mod(x, g_ref[...], sh_ref[...], sc_ref[...]).astype(BF16)
    act = (_silu(_dot(h, wg_ref[...])) * _dot(h, wu_ref[...])).astype(BF16)
    y = x + gate_ref[...] * _dot(act, wd_ref[...])
    if final:
        gf_ref, o_ref = rest
        ms = jnp.mean(y * y, axis=-1, keepdims=True)
        y = y * lax.rsqrt(ms + EPS) * gf_ref[...]
    else:
        (o_ref,) = rest
    o_ref[...] = y


def _merge_ffn(x, ya, yb, yr, mg, mods, w_br, w_o, g2, w_g, w_u, w_d, n_ctx_tiles, g_final):
    bsz, t_all, d = x.shape
    final = g_final is not None
    toff = n_ctx_tiles if final else 0
    n_t = t_all // TM - toff
    tile = lambda w: pl.BlockSpec((None, TM, w), lambda b, t: (b, t + toff, 0))
    out_tile = pl.BlockSpec((None, TM, d), lambda b, t: (b, t, 0))
    mod = lambda which: _mod_spec(mods, which, n_ctx_tiles, toff)
    full = lambda a: pl.BlockSpec(a.shape, lambda b, t: (0,) * a.ndim, pipeline_mode=pl.Buffered(1))
    out_sds = jax.ShapeDtypeStruct((bsz, n_t * TM, d), F32)
    x1 = pl.pallas_call(
        _merge_kernel,
        grid=(bsz, n_t),
        in_specs=[tile(d), tile(d), tile(d), tile(d), tile(N_BRANCH * d), full(w_br), full(w_o), mod(2)],
        out_specs=out_tile,
        out_shape=out_sds,
        compiler_params=_params("parallel", "parallel"),
        name="merge",
    )(x, ya, yb, yr, mg, w_br, w_o, mods)
    x1_tile = pl.BlockSpec((None, TM, d), lambda b, t: (b, t, 0))
    args = [x1, g2, mods, mods, mods, w_g, w_u, w_d]
    specs = [x1_tile, _vec_spec(d), mod(3), mod(4), mod(5), full(w_g), full(w_u), full(w_d)]
    if final:
        args.append(g_final)
        specs.append(_vec_spec(d))
    return pl.pallas_call(
        functools.partial(_ffn_kernel, final=final),
        grid=(bsz, n_t),
        in_specs=specs,
        out_specs=out_tile,
        out_shape=out_sds,
        compiler_params=_params("parallel", "parallel"),
        name="ffn",
    )(*args)


def _rope_tables(n_ctx, n_lat):
    rows = n_lat // GRID_W
    r = np.repeat(np.arange(rows), GRID_W).astype(np.float32)
    col = np.tile(np.arange(GRID_W), rows).astype(np.float32)
    half = ATT_HD // 2
    inv = np.power(np.float32(ROPE_THETA), -np.arange(0, half, 2, dtype=np.float32) / np.float32(half))
    ar = jnp.asarray(r[:, None] * inv[None, :], F32)
    ac = jnp.asarray(col[:, None] * inv[None, :], F32)
    cr, sr, cc, sc = jnp.cos(ar), jnp.sin(ar), jnp.cos(ac), jnp.sin(ac)
    cos_l = jnp.concatenate([cr, cr, cc, cc], axis=-1)
    sin_l = jnp.concatenate([-sr, sr, -sc, sc], axis=-1)
    cos_t = jnp.concatenate([jnp.ones((n_ctx, ATT_HD), F32), cos_l], axis=0)
    sin_t = jnp.concatenate([jnp.zeros((n_ctx, ATT_HD), F32), sin_l], axis=0)
    return _to_chunk_layout(cos_t[None])[0], _to_chunk_layout(sin_t[None])[0]


def kernel(x, c, ctx, c_ctx, w_mod, b_mod, norm1, w_in, hg_lb_logits, hg_out_norm, att_q_norm, att_k_norm, lru_conv_w, lru_conv_b, lru_w_a, lru_b_a, lru_w_x, lru_b_x, lru_lambda, w_branch, w_out, norm2, w_ffn_in, w_ffn_out, norm_final):
    bsz, n_lat, d = x.shape
    n_ctx = ctx.shape[1]
    depth = w_in.shape[0]
    assert n_ctx % TM == 0 and n_lat % TM == 0 and n_lat % GRID_W == 0
    n_ctx_tiles = n_ctx // TM
    hg_w = HG_HEADS * HG_DK
    ffn_h = w_ffn_out.shape[1]

    c_all = jnp.concatenate([c, c_ctx[None, :]], axis=0)
    mods = _mod_vectors(c_all, w_mod, b_mod).reshape(depth, bsz + 1, N_MOD, 1, d)
    lb = _lower_bounds(hg_lb_logits)
    cos_t, sin_t = _rope_tables(n_ctx, n_lat)

    xs = _stream_in(ctx, x)
    for l in range(depth):
        last = l == depth - 1
        w_hg = w_in[l, :, :5 * hg_w].astype(BF16)
        w_rest = w_in[l, :, 5 * hg_w:].astype(BF16)
        hg, rest = _inproj(xs, mods[l], norm1[l][None, :], w_hg, w_rest,
                           lb[0, l][None, :], lb[1, l][None, :],
                           att_q_norm[l][None, :], att_k_norm[l][None, :], cos_t, sin_t, n_ctx_tiles)
        qs, hv, ff, fb, sg = hg
        aq, ak, av, lx, lg, mg = rest
        ya = _hgrn(qs, hv, ff, fb, sg, hg_out_norm[l][None, :], n_ctx)
        yb = _attention(aq, ak, av, n_ctx)
        yr = _rglru(lx, lg, lru_conv_w[l], lru_conv_b[l][None, :], (0.5 * lru_w_a[l]).astype(BF16),
                    lru_b_a[l], (0.5 * lru_w_x[l]).astype(BF16), lru_b_x[l], lru_lambda[l], n_ctx)
        xs = _merge_ffn(xs, ya, yb, yr, mg, mods[l], w_branch[l].astype(BF16), w_out[l].astype(BF16),
                        norm2[l][None, :], w_ffn_in[l, :, :ffn_h].astype(BF16),
                        w_ffn_in[l, :, ffn_h:].astype(BF16), w_ffn_out[l].astype(BF16),
                        n_ctx_tiles, norm_final[None, :] if last else None)
    return _stream_out(xs)
```

```python
import functools

import numpy as np
import jax
import jax.numpy as jnp
from jax import lax
from jax.experimental import pallas as pl
from jax.experimental.pallas import tpu as pltpu

F32 = jnp.float32
BF16 = jnp.bfloat16

EPS = 1e-6
GRID_W = 64
HG_HEADS = 8
HG_DK = 128
ATT_HEADS = 8
ATT_KV_HEADS = 2
ATT_GROUP = ATT_HEADS // ATT_KV_HEADS
ATT_HD = 128
ROPE_THETA = 10000.0
LRU_BLOCKS = 4
LRU_CONV = 4
LRU_C = 8.0
N_BRANCH = 3
N_MOD = 6

V7X_VMEM_BYTES = 64 * 1024 * 1024
V7X_LANES = 128
V7X_SUBLANES = 8
VMEM_LIMIT = (V7X_VMEM_BYTES * 3) // 4

TM = 256
SEQ_CHUNK = 128
HG_CHUNK = SEQ_CHUNK
HG_HEADS_PER_STEP = 2


def _silu(x):
    return x * jax.nn.sigmoid(x)


def _gelu_tanh(x):
    return 0.5 * x * (1.0 + jnp.tanh(np.sqrt(2.0 / np.pi) * (x + 0.044715 * (x * x * x))))


def _dot(a, b):
    return jnp.dot(a, b, preferred_element_type=F32)


def _dot_nt(a, b):
    return lax.dot_general(a, b, (((1,), (1,)), ((), ())), preferred_element_type=F32)


def _dot_tn(a, b):
    return lax.dot_general(a, b, (((0,), (0,)), ((), ())), preferred_element_type=F32)


def _params(*sem):
    return pltpu.CompilerParams(dimension_semantics=sem, vmem_limit_bytes=VMEM_LIMIT)


def _mod_kernel(c_ref, w_ref, b_ref, o_ref):
    ca = _silu(c_ref[...]).astype(BF16)
    o_ref[...] = _dot(ca, w_ref[...].astype(BF16)) + b_ref[...]


def _mod_vectors(c_all, w_mod, b_mod):
    depth, d, n = w_mod.shape
    rows = c_all.shape[0]
    tn = n // 4
    return pl.pallas_call(
        _mod_kernel,
        grid=(depth, n // tn),
        in_specs=[pl.BlockSpec((rows, d), lambda l, j: (0, 0)),
                  pl.BlockSpec((None, d, tn), lambda l, j: (l, 0, j)),
                  pl.BlockSpec((None, 1, tn), lambda l, j: (l, 0, j))],
        out_specs=pl.BlockSpec((None, rows, tn), lambda l, j: (l, 0, j)),
        out_shape=jax.ShapeDtypeStruct((depth, rows, n), F32),
        compiler_params=_params("parallel", "parallel"),
        name="mod_vectors",
    )(c_all, w_mod, b_mod.reshape(depth, 1, n))


def _lb_kernel(x_ref, o_ref):
    depth = x_ref.shape[1]
    rows = [x_ref[:, l, :] for l in range(depth)]
    m = functools.reduce(jnp.maximum, rows)
    e = [jnp.exp(r - m) for r in rows]
    inv = 1.0 / functools.reduce(lambda a, b: a + b, e)
    acc = jnp.zeros_like(rows[0])
    o_ref[:, 0, :] = acc
    for l in range(1, depth):
        acc = acc + e[l] * inv
        o_ref[:, l, :] = acc


def _lower_bounds(logits):
    return pl.pallas_call(
        _lb_kernel,
        out_shape=jax.ShapeDtypeStruct(logits.shape, F32),
        name="hg_lower_bounds",
    )(logits.astype(F32))


def _norm_mod(x, g, shift, scale):
    ms = jnp.mean(x * x, axis=-1, keepdims=True)
    y = x * lax.rsqrt(ms + EPS) * g
    return y * (1.0 + scale) + shift


def _inproj_hg_kernel(x_ref, g_ref, sh_ref, sc_ref, w_ref, lbf_ref, lbb_ref,
                      qs_ref, v_ref, ff_ref, fb_ref, sg_ref):
    h = _norm_mod(x_ref[...], g_ref[...], sh_ref[...], sc_ref[...]).astype(BF16)
    w = qs_ref.shape[-1]
    z = _dot(h, w_ref[:, 0:w])
    qs_ref[...] = (_silu(z) * (HG_DK ** -0.5)).astype(BF16)
    v_ref[...] = _dot(h, w_ref[:, w:2 * w]).astype(BF16)
    for k, (lb_ref, f_ref) in enumerate(((lbf_ref, ff_ref), (lbb_ref, fb_ref))):
        lb = lb_ref[...]
        z = _dot(h, w_ref[:, (2 + k) * w:(3 + k) * w])
        f_ref[...] = lb + (1.0 - lb) * jax.nn.sigmoid(z)
    sg_ref[...] = _silu(_dot(h, w_ref[:, 4 * w:5 * w])).astype(BF16)


def _inproj_rest_kernel(x_ref, g_ref, sh_ref, sc_ref, w_ref, qg_ref, kg_ref, cos_ref, sin_ref,
                        q_ref, k_ref, v_ref, lx_ref, lg_ref, mg_ref, *, col0):
    h = _norm_mod(x_ref[...], g_ref[...], sh_ref[...], sc_ref[...]).astype(BF16)
    cos = cos_ref[...]
    sin = sin_ref[...]
    lane = lax.broadcasted_iota(jnp.int32, cos.shape, 1)
    quarter = ATT_HD // 4
    first = (lane % (2 * quarter)) < quarter

    def norm_rope(z, gain, scale):
        ms = jnp.mean(z * z, axis=-1, keepdims=True)
        y = z * lax.rsqrt(ms + EPS) * gain
        swapped = jnp.where(first, pltpu.roll(y, ATT_HD - quarter, 1), pltpu.roll(y, quarter, 1))
        return ((y * cos + swapped * sin) * scale).astype(BF16)

    nq = q_ref.shape[-1]
    nkv = k_ref.shape[-1]
    nl = lx_ref.shape[-1]
    off = col0
    z = _dot(h, w_ref[:, off:off + nq])
    for i in range(nq // ATT_HD):
        sl = slice(i * ATT_HD, (i + 1) * ATT_HD)
        q_ref[:, sl] = norm_rope(z[:, sl], qg_ref[...], ATT_HD ** -0.5 * np.log2(np.e))
    off += nq
    z = _dot(h, w_ref[:, off:off + nkv])
    for i in range(nkv // ATT_HD):
        sl = slice(i * ATT_HD, (i + 1) * ATT_HD)
        k_ref[:, sl] = norm_rope(z[:, sl], kg_ref[...], 1.0)
    off += nkv
    v_ref[...] = _dot(h, w_ref[:, off:off + nkv]).astype(BF16)
    off += nkv
    lx_ref[...] = _dot(h, w_ref[:, off:off + nl])
    off += nl
    lg_ref[...] = _gelu_tanh(_dot(h, w_ref[:, off:off + nl])).astype(BF16)
    off += nl
    mg_ref[...] = jax.nn.sigmoid(_dot(h, w_ref[:, off:off + mg_ref.shape[-1]])).astype(BF16)


def _tile_spec(width, dtype_unused=None):
    return pl.BlockSpec((None, TM, width), lambda b, t: (b, t, 0))


def _vec_spec(width):
    return pl.BlockSpec((1, width), lambda b, t: (0, 0))


def _mod_spec(mods, which, n_ctx_tiles, toff=0):
    ctx_row = mods.shape[0] - 1
    d = mods.shape[-1]
    return pl.BlockSpec((None, None, 1, d),
                        lambda b, t: (jnp.where(t + toff < n_ctx_tiles, ctx_row, b), which, 0, 0))


def _layer_weight(w_all, l, cols=None):
    shape = w_all.shape[1:] if cols is None else (*w_all.shape[1:-1], cols)
    return pl.BlockSpec((None, *shape), lambda b, t: (l,) + (0,) * len(shape),
                        pipeline_mode=pl.Buffered(1))


def _inproj(x, mods, g1, w_in, l, lbf, lbb, qg, kg, cos_t, sin_t, n_ctx_tiles):
    bsz, t_all, d = x.shape
    grid = (bsz, t_all // TM)
    hw = HG_HEADS * HG_DK
    nq = ATT_HEADS * ATT_HD
    nkv = ATT_KV_HEADS * ATT_HD
    common = [_tile_spec(d), _vec_spec(d), _mod_spec(mods, 0, n_ctx_tiles),
              _mod_spec(mods, 1, n_ctx_tiles)]
    sds = lambda w, dt: jax.ShapeDtypeStruct((bsz, t_all, w), dt)
    hg = pl.pallas_call(
        _inproj_hg_kernel,
        grid=grid,
        in_specs=common + [_layer_weight(w_in, l, 5 * hw), _vec_spec(hw), _vec_spec(hw)],
        out_specs=[_tile_spec(hw)] * 5,
        out_shape=[sds(hw, BF16), sds(hw, BF16), sds(hw, F32), sds(hw, F32), sds(hw, BF16)],
        compiler_params=_params("parallel", "parallel"),
        name="inproj_hgrn",
    )(x, g1, mods, mods, w_in, lbf, lbb)
    nl = hw
    ng = w_in.shape[-1] - 5 * hw - nq - 2 * nkv - 2 * nl
    rest = pl.pallas_call(
        functools.partial(_inproj_rest_kernel, col0=5 * hw),
        grid=grid,
        in_specs=common + [_layer_weight(w_in, l),
                           _vec_spec(ATT_HD), _vec_spec(ATT_HD),
                           pl.BlockSpec((TM, ATT_HD), lambda b, t: (t, 0)),
                           pl.BlockSpec((TM, ATT_HD), lambda b, t: (t, 0))],
        out_specs=[_tile_spec(nq), _tile_spec(nkv), _tile_spec(nkv), _tile_spec(nl), _tile_spec(nl),
                   _tile_spec(ng)],
        out_shape=[sds(nq, BF16), sds(nkv, BF16), sds(nkv, BF16), sds(nl, F32), sds(nl, BF16),
                   sds(ng, BF16)],
        compiler_params=_params("parallel", "parallel"),
        name="inproj_rest",
    )(x, g1, mods, mods, w_in, qg, kg, cos_t, sin_t)
    return hg, rest


def _chunk_token(r):
    nv = SEQ_CHUNK // V7X_SUBLANES
    return (r % V7X_SUBLANES) * nv + r // V7X_SUBLANES


def _to_chunk_layout(a):
    nv = SEQ_CHUNK // V7X_SUBLANES
    shp = a.shape
    a = a.reshape(shp[0], shp[1] // SEQ_CHUNK, V7X_SUBLANES, nv, *shp[2:])
    return jnp.swapaxes(a, 2, 3).reshape(shp)


def _split3(g):
    g1 = g.astype(BF16)
    r1 = g - g1.astype(F32)
    g2 = r1.astype(BF16)
    g3 = (r1 - g2.astype(F32)).astype(BF16)
    return jnp.concatenate([g1, g2, g3], axis=-1)


def _move_tokens(x, to_chunk):
    rows, d = x.shape
    nv = SEQ_CHUNK // V7X_SUBLANES
    r = lax.broadcasted_iota(jnp.int32, (rows, rows), 0)
    i = r % SEQ_CHUNK
    src = _chunk_token(i) if to_chunk else (i % nv) * V7X_SUBLANES + i // nv
    pick = lax.broadcasted_iota(jnp.int32, (rows, rows), 1) == (r - i) + src
    parts = _dot(jnp.where(pick, 1.0, 0.0).astype(BF16), _split3(x))
    return (parts[:, 0:d] + parts[:, d:2 * d]) + parts[:, 2 * d:3 * d]


def _stream_in_kernel(ctx_ref, x_ref, o_ref, *, n_ctx_tiles):
    @pl.when(pl.program_id(1) < n_ctx_tiles)
    def _():
        o_ref[...] = _move_tokens(ctx_ref[...], True)

    @pl.when(pl.program_id(1) >= n_ctx_tiles)
    def _():
        o_ref[...] = _move_tokens(x_ref[...], True)


def _stream_out_kernel(x_ref, o_ref):
    o_ref[...] = _move_tokens(x_ref[...], False)


def _stream_in(ctx, x):
    bsz, n_ctx, d = ctx.shape
    n_ctx_tiles = n_ctx // TM
    n_tiles = n_ctx_tiles + x.shape[1] // TM
    return pl.pallas_call(
        functools.partial(_stream_in_kernel, n_ctx_tiles=n_ctx_tiles),
        grid=(bsz, n_tiles),
        in_specs=[pl.BlockSpec((None, TM, d), lambda b, t: (b, jnp.minimum(t, n_ctx_tiles - 1), 0)),
                  pl.BlockSpec((None, TM, d), lambda b, t: (b, jnp.maximum(t - n_ctx_tiles, 0), 0))],
        out_specs=pl.BlockSpec((None, TM, d), lambda b, t: (b, t, 0)),
        out_shape=jax.ShapeDtypeStruct((bsz, n_tiles * TM, d), x.dtype),
        compiler_params=_params("parallel", "arbitrary"),
        name="stream_in",
    )(ctx, x)


def _stream_out(xs):
    bsz, n, d = xs.shape
    blk = pl.BlockSpec((None, TM, d), lambda b, t: (b, t, 0))
    return pl.pallas_call(
        _stream_out_kernel,
        grid=(bsz, n // TM),
        in_specs=[blk],
        out_specs=blk,
        out_shape=jax.ShapeDtypeStruct((bsz, n, d), xs.dtype),
        compiler_params=_params("parallel", "parallel"),
        name="stream_out",
    )(xs)


def _block_row(y, blk, ref_row):
    pos = lax.broadcasted_iota(jnp.int32, (V7X_SUBLANES, 1), 0) % blk
    s = 1
    while ref_row + s < blk:
        y = jnp.where((pos >= ref_row + s) & (pos < ref_row + 2 * s), pltpu.roll(y, s, 0), y)
        s *= 2
    s = 1
    while ref_row - s >= 0:
        y = jnp.where((pos <= ref_row - s) & (pos > ref_row - 2 * s),
                      pltpu.roll(y, V7X_SUBLANES - s, 0), y)
        s *= 2
    return y


def _level_exponent(b, blk, reverse):
    c, width = b.shape
    nv = c // V7X_SUBLANES
    half = blk // 2
    b3 = b.reshape(nv, V7X_SUBLANES, width)
    if blk <= nv:
        ref = half - 1 if reverse else half
        d = []
        for v0 in range(0, nv, blk):
            r = b3[v0 + ref:v0 + ref + 1]
            lo, hi = b3[v0:v0 + half], b3[v0 + half:v0 + blk]
            d += [lo - r, r - hi] if reverse else [r - lo, hi - r]
        d = jnp.concatenate(d, axis=0)
    else:
        sub_blk = blk // nv
        if reverse:
            ref = _block_row(b3[nv - 1], sub_blk, sub_blk // 2 - 1)
        else:
            ref = _block_row(b3[0], sub_blk, sub_blk // 2)
        d = _neg_abs(b3 - ref[None])
    return d.reshape(c, width)


def _neg_abs(x):
    bits = lax.bitcast_convert_type(x, jnp.uint32) | jnp.uint32(0x80000000)
    return lax.bitcast_convert_type(bits, F32)


def _hg_chunks(probs, pair_xor):
    c, width = probs[0]["q"].shape
    heads = [slice(h * HG_DK, (h + 1) * HG_DK) for h in range(width // HG_DK)]
    for p in probs:
        p["kk"] = (1.0 - p["f"]).astype(BF16)
        b = p["b"]
        qe = p["q"] * jnp.exp2(b).astype(BF16)
        p["o"] = [_dot_nt(qe[:, h], s.astype(BF16)) for h, s in zip(heads, p["st"])]
        b_end = b[0:1, :] if p["reverse"] else b[c - 1:c, :]
        ke = p["kk"] * jnp.exp2(b_end - b).astype(BF16)
        st_scale = jnp.exp2(b_end)
        p["st"] = [s * st_scale[:, h] + _dot_tn(p["v"][:, h], ke[:, h]) for h, s in zip(heads, p["st"])]

    blk = c
    while blk >= 2:
        for p in probs:
            e = jnp.exp2(_level_exponent(p["b"], blk, p["reverse"])).astype(BF16)
            qw = p["q"] * e
            kw = p["kk"] * e
            for n, h in enumerate(heads):
                pltpu.store(p["pairs"].at[n], _dot_nt(qw[:, h], kw[:, h]),
                            mask=None if blk == c else pair_xor < blk)
        blk //= 2
    for p in probs:
        pairs = []
        for n, h in enumerate(heads):
            pltpu.store(p["pairs"].at[n], _dot_nt(p["q"][:, h], p["kk"][:, h]), mask=pair_xor == 0)
            pairs.append(jnp.where(p["causal"], p["pairs"][n], 0.0).astype(BF16))
        p["a"] = pairs
    return ([jnp.concatenate(p["o"], axis=-1) for p in probs], [p["a"] for p in probs],
            [p["st"] for p in probs])


def _hg_running_decay(f, tri):
    width = f.shape[-1]
    g = jnp.log2(f)
    g1 = g.astype(BF16)
    g2 = (g - g1.astype(F32)).astype(BF16)
    parts = _dot(tri, jnp.concatenate([g1, g2], axis=-1))
    return parts[:, 0:width] + parts[:, width:2 * width]


def _hgrn_kernel(qs_ref, v_ref, ff_ref, fb_ref, sg_ref, gain_ref, y_ref, of_ref, ob_ref, pair_ref,
                 *, n_ctx):
    t_all, width = qs_ref.shape
    kdim = HG_DK
    c = HG_CHUNK
    n_chunks = t_all // c
    n_ctx_chunks = n_ctx // c
    row = _chunk_token(lax.broadcasted_iota(jnp.int32, (c, c), 0))
    col = _chunk_token(lax.broadcasted_iota(jnp.int32, (c, c), 1))
    pair_xor = row ^ col
    tri_f = jnp.where(col <= row, 1.0, 0.0).astype(BF16)
    tri_b = jnp.where(col >= row, 1.0, 0.0).astype(BF16)

    n_heads = width // kdim
    f_refs = (ff_ref, fb_ref)
    o_refs = (of_ref, ob_ref)
    tris = (tri_f, tri_b)

    def chunk_ids(i):
        i = jnp.minimum(i, n_chunks - 1)
        ib = jnp.where(i < n_ctx_chunks, n_ctx_chunks - 1 - i, n_chunks - 1 - (i - n_ctx_chunks))
        return (i, ib)

    def rows_of(chunk):
        return pl.ds(pl.multiple_of(chunk * c, c), c)

    def finish(chunks, pairs):
        for o_ref, chunk, a in zip(o_refs, chunks, pairs):
            rows = rows_of(chunk)
            add = [_dot(a[h], v_ref[rows, h * kdim:(h + 1) * kdim]) for h in range(n_heads)]
            o_ref[rows, :] += jnp.concatenate(add, axis=-1)

    def body(i, carry):
        sts, decays, prev_chunks, prev_pairs = carry
        finish(prev_chunks, prev_pairs)
        chunks = chunk_ids(i)
        probs = [dict(q=qs_ref[rows_of(ch), :], v=v_ref[rows_of(ch), :], f=f_ref[rows_of(ch), :],
                      b=b, st=st, reverse=d == 1, causal=(col >= row) if d else (col <= row),
                      pairs=pair_ref.at[d])
                 for d, (ch, f_ref, b, st) in enumerate(zip(chunks, f_refs, decays, sts))]
        outs, pairs, sts = _hg_chunks(probs, pair_xor)
        for o_ref, ch, o in zip(o_refs, chunks, outs):
            o_ref[rows_of(ch), :] = o
        decays = [_hg_running_decay(f_ref[rows_of(ch), :], tri)
                  for ch, f_ref, tri in zip(chunk_ids(i + 1), f_refs, tris)]
        return sts, decays, chunks, pairs

    first = chunk_ids(0)
    for o_ref, ch in zip(o_refs, first):
        o_ref[rows_of(ch), :] = jnp.zeros((c, width), F32)
    zero_st = [[jnp.zeros((kdim, kdim), F32)] * n_heads] * 2
    zero_pairs = [[jnp.zeros((c, c), BF16)] * n_heads] * 2
    decays0 = [_hg_running_decay(f_ref[rows_of(ch), :], tri)
               for ch, f_ref, tri in zip(first, f_refs, tris)]
    _, _, last_chunks, last_pairs = lax.fori_loop(
        0, n_chunks, body, (zero_st, decays0, first, zero_pairs))
    finish(last_chunks, last_pairs)

    def readout(i, carry):
        rows = pl.ds(pl.multiple_of(i * c, c), c)
        for h in range(width // kdim):
            sl = slice(h * kdim, (h + 1) * kdim)
            o = of_ref[rows, sl] + ob_ref[rows, sl]
            ms = jnp.mean(o * o, axis=-1, keepdims=True)
            y = o * lax.rsqrt(ms + EPS) * gain_ref[...]
            y_ref[rows, sl] = (y * sg_ref[rows, sl].astype(F32)).astype(BF16)
        return carry
    lax.fori_loop(0, n_chunks, readout, 0, unroll=2)


def _hgrn(qs, v, ff, fb, sg, gain, n_ctx):
    bsz, t_all, width = qs.shape
    bw = HG_HEADS_PER_STEP * HG_DK
    spec = pl.BlockSpec((None, t_all, bw), lambda b, h: (b, 0, h))
    return pl.pallas_call(
        functools.partial(_hgrn_kernel, n_ctx=n_ctx),
        grid=(bsz, width // bw),
        in_specs=[spec] * 5 + [pl.BlockSpec((1, HG_DK), lambda b, h: (0, 0))],
        out_specs=spec,
        out_shape=jax.ShapeDtypeStruct((bsz, t_all, width), BF16),
        scratch_shapes=[pltpu.VMEM((t_all, bw), F32), pltpu.VMEM((t_all, bw), F32),
                        pltpu.VMEM((2, HG_HEADS_PER_STEP, HG_CHUNK, HG_CHUNK), F32)],
        compiler_params=_params("parallel", "parallel"),
        name="hgrn2_mixer",
    )(qs, v, ff, fb, sg, gain)


def _att_kernel(q_ref, k_ref, v_ref, o_ref, *, n_ctx):
    qi = pl.program_id(2)
    n_ctx_tiles = n_ctx // q_ref.shape[0]

    def run(nk):
        k = k_ref[0:nk, :]
        v = v_ref[0:nk, :]
        heads = [slice(g * ATT_HD, (g + 1) * ATT_HD) for g in range(ATT_GROUP)]
        s_next = _dot_nt(q_ref[:, heads[0]], k)
        for g, sl in enumerate(heads):
            s = s_next
            if g + 1 < ATT_GROUP:
                s_next = _dot_nt(q_ref[:, heads[g + 1]], k)
            p = jnp.exp2(s - jnp.max(s, axis=-1, keepdims=True))
            den = jnp.sum(p, axis=-1, keepdims=True)
            o_ref[:, sl] = (_dot(p.astype(BF16), v) / den).astype(BF16)

    @pl.when(qi < n_ctx_tiles)
    def _():
        run(n_ctx)

    @pl.when(qi >= n_ctx_tiles)
    def _():
        run(k_ref.shape[0])


def _attention(q, k, v, n_ctx):
    bsz, t_all, nq = q.shape
    gw = ATT_GROUP * ATT_HD
    return pl.pallas_call(
        functools.partial(_att_kernel, n_ctx=n_ctx),
        grid=(bsz, ATT_KV_HEADS, t_all // TM),
        in_specs=[pl.BlockSpec((None, TM, gw), lambda b, h, i: (b, i, h)),
                  pl.BlockSpec((None, t_all, ATT_HD), lambda b, h, i: (b, 0, h)),
                  pl.BlockSpec((None, t_all, ATT_HD), lambda b, h, i: (b, 0, h))],
        out_specs=pl.BlockSpec((None, TM, gw), lambda b, h, i: (b, i, h)),
        out_shape=jax.ShapeDtypeStruct((bsz, t_all, nq), BF16),
        compiler_params=_params("parallel", "parallel", "arbitrary"),
        name="gqa_mixer",
    )(q, k, v)


def _lru_kernel(x_ref, gl_ref, cw_ref, cb_ref, wa_ref, ba_ref, wx_ref, bx_ref, lam_ref, y_ref,
                af_ref, uf_ref, ab_ref, ub_ref, *, n_ctx):
    t_all, w = x_ref.shape
    c = SEQ_CHUNK
    ns = V7X_SUBLANES
    nv = c // ns
    n_chunks = t_all // c
    n_ctx_chunks = n_ctx // c
    sub = lax.broadcasted_iota(jnp.int32, (ns, 1), 0)

    def rows_of(chunk):
        return pl.ds(pl.multiple_of(chunk * c, c), c)

    def vreg_rows(chunk, v):
        return pl.ds(pl.multiple_of(chunk * c + v * ns, ns), ns)

    a_refs = (af_ref, ab_ref)
    u_refs = (uf_ref, ub_ref)
    rate = [(-0.5 * LRU_C) * jax.nn.softplus(-lam_ref[d:d + 1, :]) for d in range(2)]
    half_ba = [0.5 * ba_ref[d:d + 1, :] for d in range(2)]
    half_bx = [0.5 * bx_ref[d:d + 1, :] for d in range(2)]

    def one_later(cur, prev):
        return pltpu.roll(jnp.where(sub == ns - 1, prev, cur), 1, 0)

    def one_earlier(cur, nxt):
        return pltpu.roll(jnp.where(sub == 0, nxt, cur), ns - 1, 0)

    def gates(i, carry):
        x3 = x_ref[rows_of(i), :].reshape(nv, ns, w)
        has_prev = jnp.logical_and(i > 0, i != n_ctx_chunks)
        has_next = jnp.logical_and(i < n_chunks - 1, i != n_ctx_chunks - 1)
        ip = jnp.maximum(i - 1, 0)
        inx = jnp.minimum(i + 1, n_chunks - 1)
        prev_1 = jnp.where(has_prev, x_ref[vreg_rows(ip, nv - 1), :], 0.0)
        prev_2 = jnp.where(has_prev, x_ref[vreg_rows(ip, nv - 2), :], 0.0)
        next_0 = jnp.where(has_next, x_ref[vreg_rows(inx, 0), :], 0.0)
        ext = jnp.concatenate([one_later(x3[nv - 2], prev_2)[None], one_later(x3[nv - 1], prev_1)[None],
                               x3, one_earlier(x3[0], next_0)[None]], axis=0)
        xc = cb_ref[...] + ext[0:nv] * cw_ref[0:1, :]
        for j in range(1, LRU_CONV):
            xc = xc + ext[j:j + nv] * cw_ref[j:j + 1, :]
        xc = xc.reshape(c, w)
        xcb = xc.astype(BF16)
        half_xc = 0.5 * xc
        for d in range(2):
            th_r = jnp.tanh(_dot(xcb, wa_ref[d]) + half_ba[d])
            th_i = jnp.tanh(_dot(xcb, wx_ref[d]) + half_bx[d])
            log_a = rate[d] * (th_r + 1.0)
            a = jnp.exp(log_a)
            one_minus_a2 = jnp.tanh(log_a) * (-1.0 - a * a)
            a_refs[d][rows_of(i), :] = a
            u_refs[d][rows_of(i), :] = jnp.sqrt(one_minus_a2) * ((th_i + 1.0) * half_xc)
        return carry
    lax.fori_loop(0, n_chunks, gates, 0, unroll=2)

    def chunk_scan(a_ref, u_ref, chunk, h_in, reverse):
        a3 = a_ref[rows_of(chunk), :].reshape(nv, ns, w)
        u3 = u_ref[rows_of(chunk), :].reshape(nv, ns, w)
        order = list(range(nv - 1, -1, -1) if reverse else range(nv))
        h, p = u3[order[0]], a3[order[0]]
        hs, ps = {order[0]: h}, {order[0]: p}
        for v in order[1:]:
            h = a3[v] * h + u3[v]
            p = a3[v] * p
            hs[v], ps[v] = h, p
        s = 1
        while s < ns:
            sh = (ns - s) if reverse else s
            keep = (sub + s < ns) if reverse else (sub >= s)
            h = h + p * jnp.where(keep, pltpu.roll(h, sh, 0), 0.0)
            p = p * jnp.where(keep, pltpu.roll(p, sh, 0), 1.0)
            s *= 2
        end = h + p * h_in
        if reverse:
            enter = jnp.where(sub == ns - 1, h_in, pltpu.roll(end, ns - 1, 0))
            h_out = end[0:1, :]
        else:
            enter = jnp.where(sub == 0, h_in, pltpu.roll(end, 1, 0))
            h_out = end[ns - 1:ns, :]
        full = jnp.concatenate([(hs[v] + ps[v] * enter)[None] for v in range(nv)], axis=0)
        u_ref[rows_of(chunk), :] = full.reshape(c, w)
        return h_out

    def scan_body(i, carry):
        hf, hb = carry
        hf = chunk_scan(af_ref, uf_ref, i, hf, False)
        ib = jnp.where(i < n_ctx_chunks, n_ctx_chunks - 1 - i, n_chunks - 1 - (i - n_ctx_chunks))
        hb = chunk_scan(ab_ref, ub_ref, ib, hb, True)
        return hf, hb
    zero = jnp.zeros((1, w), F32)
    lax.fori_loop(0, n_chunks, scan_body, (zero, zero))

    def out(i, carry):
        rows = rows_of(i)
        y_ref[rows, :] = ((uf_ref[rows, :] + ub_ref[rows, :]) * gl_ref[rows, :].astype(F32)).astype(BF16)
        return carry
    lax.fori_loop(0, n_chunks, out, 0)


def _rglru(lx, lg, conv_w, conv_b, w_a, b_a, w_x, b_x, lam, n_ctx):
    bsz, t_all, width = lx.shape
    bw = width // LRU_BLOCKS
    seq = pl.BlockSpec((None, t_all, bw), lambda b, j: (b, 0, j))
    vec = lambda rows: pl.BlockSpec((rows, bw), lambda b, j: (0, j))
    wspec = pl.BlockSpec((2, None, bw, bw), lambda b, j: (0, j, 0, 0))
    return pl.pallas_call(
        functools.partial(_lru_kernel, n_ctx=n_ctx),
        grid=(bsz, LRU_BLOCKS),
        in_specs=[seq, seq, vec(LRU_CONV), vec(1), wspec, vec(2), wspec, vec(2), vec(2)],
        out_specs=seq,
        out_shape=jax.ShapeDtypeStruct((bsz, t_all, width), BF16),
        scratch_shapes=[pltpu.VMEM((t_all, bw), F32)] * 4,
        compiler_params=_params("parallel", "parallel"),
        name="rglru_mixer",
    )(lx, lg, conv_w, conv_b, w_a, b_a, w_x, b_x, lam)


def _merge_kernel(x_ref, a_ref, b_ref, r_ref, mg_ref, wbr_ref, wo_ref, gate_ref, o_ref):
    d = x_ref.shape[-1]
    y = None
    for i, br in enumerate((a_ref, b_ref, r_ref)):
        term = mg_ref[:, i * d:(i + 1) * d].astype(F32) * _dot(br[...], wbr_ref[i])
        y = term if y is None else y + term
    o_ref[...] = x_ref[...] + gate_ref[...] * _dot(y.astype(BF16), wo_ref[...])


def _ffn_kernel(x_ref, g_ref, sh_ref, sc_ref, gate_ref, wi_ref, wd_ref, *rest, final):
    x = x_ref[...]
    h = _norm_mod(x, g_ref[...], sh_ref[...], sc_ref[...]).astype(BF16)
    hidden = wd_ref.shape[0]
    act = (_silu(_dot(h, wi_ref[:, 0:hidden])) * _dot(h, wi_ref[:, hidden:2 * hidden])).astype(BF16)
    y = x + gate_ref[...] * _dot(act, wd_ref[...])
    if final:
        gf_ref, o_ref = rest
        ms = jnp.mean(y * y, axis=-1, keepdims=True)
        y = y * lax.rsqrt(ms + EPS) * gf_ref[...]
    else:
        (o_ref,) = rest
    o_ref[...] = y


def _merge_ffn(x, ya, yb, yr, mg, mods, w_br, w_o, g2, w_i, w_d, l, n_ctx_tiles, g_final):
    bsz, t_all, d = x.shape
    final = g_final is not None
    toff = n_ctx_tiles if final else 0
    n_t = t_all // TM - toff
    tile = lambda w: pl.BlockSpec((None, TM, w), lambda b, t: (b, t + toff, 0))
    out_tile = pl.BlockSpec((None, TM, d), lambda b, t: (b, t, 0))
    mod = lambda which: _mod_spec(mods, which, n_ctx_tiles, toff)
    full = lambda a: _layer_weight(a, l)
    out_sds = jax.ShapeDtypeStruct((bsz, n_t * TM, d), F32)
    x1 = pl.pallas_call(
        _merge_kernel,
        grid=(bsz, n_t),
        in_specs=[tile(d), tile(d), tile(d), tile(d), tile(N_BRANCH * d), full(w_br), full(w_o), mod(2)],
        out_specs=out_tile,
        out_shape=out_sds,
        compiler_params=_params("parallel", "parallel"),
        name="merge",
    )(x, ya, yb, yr, mg, w_br, w_o, mods)
    x1_tile = pl.BlockSpec((None, TM, d), lambda b, t: (b, t, 0))
    args = [x1, g2, mods, mods, mods, w_i, w_d]
    specs = [x1_tile, _vec_spec(d), mod(3), mod(4), mod(5), full(w_i), full(w_d)]
    if final:
        args.append(g_final)
        specs.append(_vec_spec(d))
    return pl.pallas_call(
        functools.partial(_ffn_kernel, final=final),
        grid=(bsz, n_t),
        in_specs=specs,
        out_specs=out_tile,
        out_shape=out_sds,
        compiler_params=_params("parallel", "parallel"),
        name="ffn",
    )(*args)


def _rope_tables(n_ctx, n_lat):
    rows = n_lat // GRID_W
    r = np.repeat(np.arange(rows), GRID_W).astype(np.float32)
    col = np.tile(np.arange(GRID_W), rows).astype(np.float32)
    half = ATT_HD // 2
    inv = np.power(np.float32(ROPE_THETA), -np.arange(0, half, 2, dtype=np.float32) / np.float32(half))
    ar = jnp.asarray(r[:, None] * inv[None, :], F32)
    ac = jnp.asarray(col[:, None] * inv[None, :], F32)
    cr, sr, cc, sc = jnp.cos(ar), jnp.sin(ar), jnp.cos(ac), jnp.sin(ac)
    cos_l = jnp.concatenate([cr, cr, cc, cc], axis=-1)
    sin_l = jnp.concatenate([-sr, sr, -sc, sc], axis=-1)
    cos_t = jnp.concatenate([jnp.ones((n_ctx, ATT_HD), F32), cos_l], axis=0)
    sin_t = jnp.concatenate([jnp.zeros((n_ctx, ATT_HD), F32), sin_l], axis=0)
    return _to_chunk_layout(cos_t[None])[0], _to_chunk_layout(sin_t[None])[0]


def kernel(x, c, ctx, c_ctx, w_mod, b_mod, norm1, w_in, hg_lb_logits, hg_out_norm, att_q_norm, att_k_norm, lru_conv_w, lru_conv_b, lru_w_a, lru_b_a, lru_w_x, lru_b_x, lru_lambda, w_branch, w_out, norm2, w_ffn_in, w_ffn_out, norm_final):
    bsz, n_lat, d = x.shape
    n_ctx = ctx.shape[1]
    depth = w_in.shape[0]
    assert n_ctx % TM == 0 and n_lat % TM == 0 and n_lat % GRID_W == 0
    n_ctx_tiles = n_ctx // TM
    w_in, w_branch, w_out, w_ffn_in, w_ffn_out = (
        w.astype(BF16) for w in (w_in, w_branch, w_out, w_ffn_in, w_ffn_out))

    c_all = jnp.concatenate([c, c_ctx[None, :]], axis=0)
    mods = _mod_vectors(c_all, w_mod, b_mod).reshape(depth, bsz + 1, N_MOD, 1, d)
    lb = _lower_bounds(hg_lb_logits)
    cos_t, sin_t = _rope_tables(n_ctx, n_lat)

    xs = _stream_in(ctx, x)
    for l in range(depth):
        last = l == depth - 1
        hg, rest = _inproj(xs, mods[l], norm1[l][None, :], w_in, l,
                           lb[0, l][None, :], lb[1, l][None, :],
                           att_q_norm[l][None, :], att_k_norm[l][None, :], cos_t, sin_t, n_ctx_tiles)
        qs, hv, ff, fb, sg = hg
        aq, ak, av, lx, lg, mg = rest
        ya = _hgrn(qs, hv, ff, fb, sg, hg_out_norm[l][None, :], n_ctx)
        yb = _attention(aq, ak, av, n_ctx)
        yr = _rglru(lx, lg, lru_conv_w[l], lru_conv_b[l][None, :], (0.5 * lru_w_a[l]).astype(BF16),
                    lru_b_a[l], (0.5 * lru_w_x[l]).astype(BF16), lru_b_x[l], lru_lambda[l], n_ctx)
        xs = _merge_ffn(xs, ya, yb, yr, mg, mods[l], w_branch, w_out, norm2[l][None, :],
                        w_ffn_in, w_ffn_out, l, n_ctx_tiles, norm_final[None, :] if last else None)
    return _stream_out(xs)
```

```python
import functools

import numpy as np
import jax
import jax.numpy as jnp
from jax import lax
from jax.experimental import pallas as pl
from jax.experimental.pallas import tpu as pltpu

F32 = jnp.float32
BF16 = jnp.bfloat16

EPS = 1e-6
GRID_W = 64
HG_HEADS = 8
HG_DK = 128
ATT_HEADS = 8
ATT_KV_HEADS = 2
ATT_GROUP = ATT_HEADS // ATT_KV_HEADS
ATT_HD = 128
ROPE_THETA = 10000.0
LRU_BLOCKS = 4
LRU_CONV = 4
LRU_C = 8.0
N_BRANCH = 3
N_MOD = 6

V7X_VMEM_BYTES = 64 * 1024 * 1024
V7X_LANES = 128
V7X_SUBLANES = 8
VMEM_LIMIT = (V7X_VMEM_BYTES * 3) // 4

TM = 256
SEQ_CHUNK = 128
HG_CHUNK = SEQ_CHUNK
HG_HEADS_PER_STEP = 2


def _silu(x):
    return x * jax.nn.sigmoid(x)


def _gelu_tanh(x):
    return 0.5 * x * (1.0 + jnp.tanh(np.sqrt(2.0 / np.pi) * (x + 0.044715 * (x * x * x))))


def _dot(a, b):
    return jnp.dot(a, b, preferred_element_type=F32)


def _dot_nt(a, b):
    return lax.dot_general(a, b, (((1,), (1,)), ((), ())), preferred_element_type=F32)


def _dot_tn(a, b):
    return lax.dot_general(a, b, (((0,), (0,)), ((), ())), preferred_element_type=F32)


def _params(*sem):
    return pltpu.CompilerParams(dimension_semantics=sem, vmem_limit_bytes=VMEM_LIMIT)


def _mod_kernel(c_ref, w_ref, b_ref, o_ref):
    ca = _silu(c_ref[...]).astype(BF16)
    o_ref[...] = _dot(ca, w_ref[...].astype(BF16)) + b_ref[...]


def _mod_vectors(c_all, w_mod, b_mod):
    depth, d, n = w_mod.shape
    rows = c_all.shape[0]
    tn = n // 4
    return pl.pallas_call(
        _mod_kernel,
        grid=(depth, n // tn),
        in_specs=[pl.BlockSpec((rows, d), lambda l, j: (0, 0)),
                  pl.BlockSpec((None, d, tn), lambda l, j: (l, 0, j)),
                  pl.BlockSpec((None, 1, tn), lambda l, j: (l, 0, j))],
        out_specs=pl.BlockSpec((None, rows, tn), lambda l, j: (l, 0, j)),
        out_shape=jax.ShapeDtypeStruct((depth, rows, n), F32),
        compiler_params=_params("parallel", "parallel"),
        name="mod_vectors",
    )(c_all, w_mod, b_mod.reshape(depth, 1, n))


def _lb_kernel(x_ref, o_ref):
    depth = x_ref.shape[1]
    rows = [x_ref[:, l, :] for l in range(depth)]
    m = functools.reduce(jnp.maximum, rows)
    e = [jnp.exp(r - m) for r in rows]
    inv = 1.0 / functools.reduce(lambda a, b: a + b, e)
    acc = jnp.zeros_like(rows[0])
    o_ref[:, 0, :] = acc
    for l in range(1, depth):
        acc = acc + e[l] * inv
        o_ref[:, l, :] = acc


def _lower_bounds(logits):
    return pl.pallas_call(
        _lb_kernel,
        out_shape=jax.ShapeDtypeStruct(logits.shape, F32),
        name="hg_lower_bounds",
    )(logits.astype(F32))


def _norm_mod(x, g, shift, scale):
    ms = jnp.mean(x * x, axis=-1, keepdims=True)
    y = x * lax.rsqrt(ms + EPS) * g
    return y * (1.0 + scale) + shift


def _inproj_hg_kernel(x_ref, g_ref, sh_ref, sc_ref, w_ref, lbf_ref, lbb_ref,
                      qs_ref, v_ref, ff_ref, fb_ref, sg_ref):
    h = _norm_mod(x_ref[...], g_ref[...], sh_ref[...], sc_ref[...]).astype(BF16)
    w = qs_ref.shape[-1]
    z = _dot(h, w_ref[:, 0:w])
    qs_ref[...] = (_silu(z) * (HG_DK ** -0.5)).astype(BF16)
    v_ref[...] = _dot(h, w_ref[:, w:2 * w]).astype(BF16)
    for k, (lb_ref, f_ref) in enumerate(((lbf_ref, ff_ref), (lbb_ref, fb_ref))):
        lb = lb_ref[...]
        z = _dot(h, w_ref[:, (2 + k) * w:(3 + k) * w])
        f_ref[...] = lb + (1.0 - lb) * jax.nn.sigmoid(z)
    sg_ref[...] = _silu(_dot(h, w_ref[:, 4 * w:5 * w])).astype(BF16)


def _inproj_rest_kernel(x_ref, g_ref, sh_ref, sc_ref, w_ref, qg_ref, kg_ref, cos_ref, sin_ref,
                        q_ref, k_ref, v_ref, lx_ref, lg_ref, mg_ref, *, col0):
    h = _norm_mod(x_ref[...], g_ref[...], sh_ref[...], sc_ref[...]).astype(BF16)
    cos = cos_ref[...]
    sin = sin_ref[...]
    lane = lax.broadcasted_iota(jnp.int32, cos.shape, 1)
    quarter = ATT_HD // 4
    first = (lane % (2 * quarter)) < quarter

    def norm_rope(z, gain, scale):
        ms = jnp.mean(z * z, axis=-1, keepdims=True)
        y = z * lax.rsqrt(ms + EPS) * gain
        swapped = jnp.where(first, pltpu.roll(y, ATT_HD - quarter, 1), pltpu.roll(y, quarter, 1))
        return ((y * cos + swapped * sin) * scale).astype(BF16)

    nq = q_ref.shape[-1]
    nkv = k_ref.shape[-1]
    nl = lx_ref.shape[-1]
    off = col0
    z = _dot(h, w_ref[:, off:off + nq])
    for i in range(nq // ATT_HD):
        sl = slice(i * ATT_HD, (i + 1) * ATT_HD)
        q_ref[:, sl] = norm_rope(z[:, sl], qg_ref[...], ATT_HD ** -0.5 * np.log2(np.e))
    off += nq
    z = _dot(h, w_ref[:, off:off + nkv])
    for i in range(nkv // ATT_HD):
        sl = slice(i * ATT_HD, (i + 1) * ATT_HD)
        k_ref[:, sl] = norm_rope(z[:, sl], kg_ref[...], 1.0)
    off += nkv
    v_ref[...] = _dot(h, w_ref[:, off:off + nkv]).astype(BF16)
    off += nkv
    lx_ref[...] = _dot(h, w_ref[:, off:off + nl])
    off += nl
    lg_ref[...] = _gelu_tanh(_dot(h, w_ref[:, off:off + nl])).astype(BF16)
    off += nl
    mg_ref[...] = jax.nn.sigmoid(_dot(h, w_ref[:, off:off + mg_ref.shape[-1]])).astype(BF16)


def _tile_spec(width, dtype_unused=None):
    return pl.BlockSpec((None, TM, width), lambda b, t: (b, t, 0))


def _vec_spec(width):
    return pl.BlockSpec((1, width), lambda b, t: (0, 0))


def _mod_spec(mods, which, n_ctx_tiles, toff=0):
    ctx_row = mods.shape[0] - 1
    d = mods.shape[-1]
    return pl.BlockSpec((None, None, 1, d),
                        lambda b, t: (jnp.where(t + toff < n_ctx_tiles, ctx_row, b), which, 0, 0))


def _layer_weight(w_all, l, cols=None):
    shape = w_all.shape[1:] if cols is None else (*w_all.shape[1:-1], cols)
    return pl.BlockSpec((None, *shape), lambda b, t: (l,) + (0,) * len(shape),
                        pipeline_mode=pl.Buffered(1))


def _inproj(x, mods, g1, w_in, l, lbf, lbb, qg, kg, cos_t, sin_t, n_ctx_tiles):
    bsz, t_all, d = x.shape
    grid = (bsz, t_all // TM)
    hw = HG_HEADS * HG_DK
    nq = ATT_HEADS * ATT_HD
    nkv = ATT_KV_HEADS * ATT_HD
    common = [_tile_spec(d), _vec_spec(d), _mod_spec(mods, 0, n_ctx_tiles),
              _mod_spec(mods, 1, n_ctx_tiles)]
    sds = lambda w, dt: jax.ShapeDtypeStruct((bsz, t_all, w), dt)
    hg = pl.pallas_call(
        _inproj_hg_kernel,
        grid=grid,
        in_specs=common + [_layer_weight(w_in, l, 5 * hw), _vec_spec(hw), _vec_spec(hw)],
        out_specs=[_tile_spec(hw)] * 5,
        out_shape=[sds(hw, BF16), sds(hw, BF16), sds(hw, F32), sds(hw, F32), sds(hw, BF16)],
        compiler_params=_params("parallel", "parallel"),
        name="inproj_hgrn",
    )(x, g1, mods, mods, w_in, lbf, lbb)
    nl = hw
    ng = w_in.shape[-1] - 5 * hw - nq - 2 * nkv - 2 * nl
    rest = pl.pallas_call(
        functools.partial(_inproj_rest_kernel, col0=5 * hw),
        grid=grid,
        in_specs=common + [_layer_weight(w_in, l),
                           _vec_spec(ATT_HD), _vec_spec(ATT_HD),
                           pl.BlockSpec((TM, ATT_HD), lambda b, t: (t, 0)),
                           pl.BlockSpec((TM, ATT_HD), lambda b, t: (t, 0))],
        out_specs=[_tile_spec(nq), _tile_spec(nkv), _tile_spec(nkv), _tile_spec(nl), _tile_spec(nl),
                   _tile_spec(ng)],
        out_shape=[sds(nq, BF16), sds(nkv, BF16), sds(nkv, BF16), sds(nl, F32), sds(nl, BF16),
                   sds(ng, BF16)],
        compiler_params=_params("parallel", "parallel"),
        name="inproj_rest",
    )(x, g1, mods, mods, w_in, qg, kg, cos_t, sin_t)
    return hg, rest


def _chunk_token(r):
    nv = SEQ_CHUNK // V7X_SUBLANES
    return (r % V7X_SUBLANES) * nv + r // V7X_SUBLANES


def _to_chunk_layout(a):
    nv = SEQ_CHUNK // V7X_SUBLANES
    shp = a.shape
    a = a.reshape(shp[0], shp[1] // SEQ_CHUNK, V7X_SUBLANES, nv, *shp[2:])
    return jnp.swapaxes(a, 2, 3).reshape(shp)


def _split3(g):
    g1 = g.astype(BF16)
    r1 = g - g1.astype(F32)
    g2 = r1.astype(BF16)
    g3 = (r1 - g2.astype(F32)).astype(BF16)
    return jnp.concatenate([g1, g2, g3], axis=-1)


def _move_tokens(x, to_chunk):
    rows, d = x.shape
    nv = SEQ_CHUNK // V7X_SUBLANES
    r = lax.broadcasted_iota(jnp.int32, (rows, rows), 0)
    i = r % SEQ_CHUNK
    src = _chunk_token(i) if to_chunk else (i % nv) * V7X_SUBLANES + i // nv
    pick = lax.broadcasted_iota(jnp.int32, (rows, rows), 1) == (r - i) + src
    parts = _dot(jnp.where(pick, 1.0, 0.0).astype(BF16), _split3(x))
    return (parts[:, 0:d] + parts[:, d:2 * d]) + parts[:, 2 * d:3 * d]


def _stream_in_kernel(ctx_ref, x_ref, o_ref, *, n_ctx_tiles):
    @pl.when(pl.program_id(1) < n_ctx_tiles)
    def _():
        o_ref[...] = _move_tokens(ctx_ref[...], True)

    @pl.when(pl.program_id(1) >= n_ctx_tiles)
    def _():
        o_ref[...] = _move_tokens(x_ref[...], True)


def _stream_out_kernel(x_ref, o_ref):
    o_ref[...] = _move_tokens(x_ref[...], False)


def _stream_in(ctx, x):
    bsz, n_ctx, d = ctx.shape
    n_ctx_tiles = n_ctx // TM
    n_tiles = n_ctx_tiles + x.shape[1] // TM
    return pl.pallas_call(
        functools.partial(_stream_in_kernel, n_ctx_tiles=n_ctx_tiles),
        grid=(bsz, n_tiles),
        in_specs=[pl.BlockSpec((None, TM, d), lambda b, t: (b, jnp.minimum(t, n_ctx_tiles - 1), 0)),
                  pl.BlockSpec((None, TM, d), lambda b, t: (b, jnp.maximum(t - n_ctx_tiles, 0), 0))],
        out_specs=pl.BlockSpec((None, TM, d), lambda b, t: (b, t, 0)),
        out_shape=jax.ShapeDtypeStruct((bsz, n_tiles * TM, d), x.dtype),
        compiler_params=_params("parallel", "arbitrary"),
        name="stream_in",
    )(ctx, x)


def _stream_out(xs):
    bsz, n, d = xs.shape
    blk = pl.BlockSpec((None, TM, d), lambda b, t: (b, t, 0))
    return pl.pallas_call(
        _stream_out_kernel,
        grid=(bsz, n // TM),
        in_specs=[blk],
        out_specs=blk,
        out_shape=jax.ShapeDtypeStruct((bsz, n, d), xs.dtype),
        compiler_params=_params("parallel", "parallel"),
        name="stream_out",
    )(xs)


def _block_row(y, blk, ref_row):
    pos = lax.broadcasted_iota(jnp.int32, (V7X_SUBLANES, 1), 0) % blk
    s = 1
    while ref_row + s < blk:
        y = jnp.where((pos >= ref_row + s) & (pos < ref_row + 2 * s), pltpu.roll(y, s, 0), y)
        s *= 2
    s = 1
    while ref_row - s >= 0:
        y = jnp.where((pos <= ref_row - s) & (pos > ref_row - 2 * s),
                      pltpu.roll(y, V7X_SUBLANES - s, 0), y)
        s *= 2
    return y


def _level_exponent(b, blk, reverse):
    c, width = b.shape
    nv = c // V7X_SUBLANES
    half = blk // 2
    b3 = b.reshape(nv, V7X_SUBLANES, width)
    if blk <= nv:
        ref = half - 1 if reverse else half
        d = []
        for v0 in range(0, nv, blk):
            r = b3[v0 + ref:v0 + ref + 1]
            lo, hi = b3[v0:v0 + half], b3[v0 + half:v0 + blk]
            d += [lo - r, r - hi] if reverse else [r - lo, hi - r]
        d = jnp.concatenate(d, axis=0)
    else:
        sub_blk = blk // nv
        if reverse:
            ref = _block_row(b3[nv - 1], sub_blk, sub_blk // 2 - 1)
        else:
            ref = _block_row(b3[0], sub_blk, sub_blk // 2)
        d = _neg_abs(b3 - ref[None])
    return d.reshape(c, width)


def _neg_abs(x):
    bits = lax.bitcast_convert_type(x, jnp.uint32) | jnp.uint32(0x80000000)
    return lax.bitcast_convert_type(bits, F32)


def _hg_chunks(probs, pair_xor):
    c, width = probs[0]["q"].shape
    heads = [slice(h * HG_DK, (h + 1) * HG_DK) for h in range(width // HG_DK)]
    for p in probs:
        p["kk"] = (1.0 - p["f"]).astype(BF16)
        b = p["b"]
        qe = p["q"] * jnp.exp2(b).astype(BF16)
        p["o"] = [_dot_nt(qe[:, h], s.astype(BF16)) for h, s in zip(heads, p["st"])]
        b_end = b[0:1, :] if p["reverse"] else b[c - 1:c, :]
        ke = p["kk"] * jnp.exp2(b_end - b).astype(BF16)
        st_scale = jnp.exp2(b_end)
        p["st"] = [s * st_scale[:, h] + _dot_tn(p["v"][:, h], ke[:, h]) for h, s in zip(heads, p["st"])]

    blk = c
    while blk >= 4:
        for p in probs:
            e = jnp.exp2(_level_exponent(p["b"], blk, p["reverse"])).astype(BF16)
            qw = p["q"] * e
            kw = p["kk"] * e
            for n, h in enumerate(heads):
                pltpu.store(p["pairs"].at[n], _dot_nt(qw[:, h], kw[:, h]),
                            mask=None if blk == c else pair_xor < blk)
        blk //= 2
    nv = c // V7X_SUBLANES
    vreg = lax.broadcasted_iota(jnp.int32, (nv, 1, 1), 0)
    for p in probs:
        q3, k3, v3 = (p[name].astype(F32).reshape(nv, V7X_SUBLANES, width) for name in ("q", "kk", "v"))
        f3 = p["f"].reshape(nv, V7X_SUBLANES, width)
        if p["reverse"]:
            k_nb, v_nb = (jnp.concatenate([a[1:], a[nv - 1:]], axis=0) for a in (k3, v3))
            w_nb, has_nb = q3 * k_nb * f3, vreg % 2 == 0
        else:
            k_nb, v_nb = (jnp.concatenate([a[:1], a[:nv - 1]], axis=0) for a in (k3, v3))
            w_nb, has_nb = q3 * k_nb * f3, vreg % 2 == 1
        w_self = q3 * k3
        pairs = []
        for n, h in enumerate(heads):
            near = (jnp.sum(w_self[:, :, h], axis=-1, keepdims=True) * v3[:, :, h]
                    + jnp.where(has_nb, jnp.sum(w_nb[:, :, h], axis=-1, keepdims=True), 0.0) * v_nb[:, :, h])
            p["o"][n] = p["o"][n] + near.reshape(c, HG_DK)
            keep = p["causal"] & (pair_xor >= 2)
            pairs.append(jnp.where(keep, p["pairs"][n], 0.0).astype(BF16))
        p["a"] = pairs
    return ([jnp.concatenate(p["o"], axis=-1) for p in probs], [p["a"] for p in probs],
            [p["st"] for p in probs])


def _hg_running_decay(f, tri):
    width = f.shape[-1]
    g = jnp.log2(f)
    g1 = g.astype(BF16)
    g2 = (g - g1.astype(F32)).astype(BF16)
    parts = _dot(tri, jnp.concatenate([g1, g2], axis=-1))
    return parts[:, 0:width] + parts[:, width:2 * width]


def _hgrn_kernel(qs_ref, v_ref, ff_ref, fb_ref, sg_ref, gain_ref, y_ref, of_ref, ob_ref, pair_ref,
                 *, n_ctx):
    t_all, width = qs_ref.shape
    kdim = HG_DK
    c = HG_CHUNK
    n_chunks = t_all // c
    n_ctx_chunks = n_ctx // c
    row = _chunk_token(lax.broadcasted_iota(jnp.int32, (c, c), 0))
    col = _chunk_token(lax.broadcasted_iota(jnp.int32, (c, c), 1))
    pair_xor = row ^ col
    tri_f = jnp.where(col <= row, 1.0, 0.0).astype(BF16)
    tri_b = jnp.where(col >= row, 1.0, 0.0).astype(BF16)

    n_heads = width // kdim
    f_refs = (ff_ref, fb_ref)
    o_refs = (of_ref, ob_ref)
    tris = (tri_f, tri_b)

    def chunk_ids(i):
        i = jnp.minimum(i, n_chunks - 1)
        ib = jnp.where(i < n_ctx_chunks, n_ctx_chunks - 1 - i, n_chunks - 1 - (i - n_ctx_chunks))
        return (i, ib)

    def rows_of(chunk):
        return pl.ds(pl.multiple_of(chunk * c, c), c)

    def finish(chunks, pairs):
        for o_ref, chunk, a in zip(o_refs, chunks, pairs):
            rows = rows_of(chunk)
            add = [_dot(a[h], v_ref[rows, h * kdim:(h + 1) * kdim]) for h in range(n_heads)]
            o_ref[rows, :] += jnp.concatenate(add, axis=-1)

    def body(i, carry):
        sts, decays, prev_chunks, prev_pairs = carry
        finish(prev_chunks, prev_pairs)
        chunks = chunk_ids(i)
        probs = [dict(q=qs_ref[rows_of(ch), :], v=v_ref[rows_of(ch), :], f=f_ref[rows_of(ch), :],
                      b=b, st=st, reverse=d == 1, causal=(col >= row) if d else (col <= row),
                      pairs=pair_ref.at[d])
                 for d, (ch, f_ref, b, st) in enumerate(zip(chunks, f_refs, decays, sts))]
        outs, pairs, sts = _hg_chunks(probs, pair_xor)
        for o_ref, ch, o in zip(o_refs, chunks, outs):
            o_ref[rows_of(ch), :] = o
        decays = [_hg_running_decay(f_ref[rows_of(ch), :], tri)
                  for ch, f_ref, tri in zip(chunk_ids(i + 1), f_refs, tris)]
        return sts, decays, chunks, pairs

    first = chunk_ids(0)
    for o_ref, ch in zip(o_refs, first):
        o_ref[rows_of(ch), :] = jnp.zeros((c, width), F32)
    zero_st = [[jnp.zeros((kdim, kdim), F32)] * n_heads] * 2
    zero_pairs = [[jnp.zeros((c, c), BF16)] * n_heads] * 2
    decays0 = [_hg_running_decay(f_ref[rows_of(ch), :], tri)
               for ch, f_ref, tri in zip(first, f_refs, tris)]
    _, _, last_chunks, last_pairs = lax.fori_loop(
        0, n_chunks, body, (zero_st, decays0, first, zero_pairs))
    finish(last_chunks, last_pairs)

    def readout(i, carry):
        rows = pl.ds(pl.multiple_of(i * c, c), c)
        for h in range(width // kdim):
            sl = slice(h * kdim, (h + 1) * kdim)
            o = of_ref[rows, sl] + ob_ref[rows, sl]
            ms = jnp.mean(o * o, axis=-1, keepdims=True)
            y = o * lax.rsqrt(ms + EPS) * gain_ref[...]
            y_ref[rows, sl] = (y * sg_ref[rows, sl].astype(F32)).astype(BF16)
        return carry
    lax.fori_loop(0, n_chunks, readout, 0, unroll=2)


def _hgrn(qs, v, ff, fb, sg, gain, n_ctx):
    bsz, t_all, width = qs.shape
    bw = HG_HEADS_PER_STEP * HG_DK
    spec = pl.BlockSpec((None, t_all, bw), lambda b, h: (b, 0, h))
    return pl.pallas_call(
        functools.partial(_hgrn_kernel, n_ctx=n_ctx),
        grid=(bsz, width // bw),
        in_specs=[spec] * 5 + [pl.BlockSpec((1, HG_DK), lambda b, h: (0, 0))],
        out_specs=spec,
        out_shape=jax.ShapeDtypeStruct((bsz, t_all, width), BF16),
        scratch_shapes=[pltpu.VMEM((t_all, bw), F32), pltpu.VMEM((t_all, bw), F32),
                        pltpu.VMEM((2, HG_HEADS_PER_STEP, HG_CHUNK, HG_CHUNK), F32)],
        compiler_params=_params("parallel", "parallel"),
        name="hgrn2_mixer",
    )(qs, v, ff, fb, sg, gain)


def _att_kernel(q_ref, k_ref, v_ref, o_ref, *, n_ctx):
    qi = pl.program_id(2)
    n_ctx_tiles = n_ctx // q_ref.shape[0]

    def run(nk):
        k = k_ref[0:nk, :]
        v = v_ref[0:nk, :]
        heads = [slice(g * ATT_HD, (g + 1) * ATT_HD) for g in range(ATT_GROUP)]
        s_next = _dot_nt(q_ref[:, heads[0]], k)
        for g, sl in enumerate(heads):
            s = s_next
            if g + 1 < ATT_GROUP:
                s_next = _dot_nt(q_ref[:, heads[g + 1]], k)
            p = jnp.exp2(s - jnp.max(s, axis=-1, keepdims=True))
            den = jnp.sum(p, axis=-1, keepdims=True)
            o_ref[:, sl] = (_dot(p.astype(BF16), v) / den).astype(BF16)

    @pl.when(qi < n_ctx_tiles)
    def _():
        run(n_ctx)

    @pl.when(qi >= n_ctx_tiles)
    def _():
        run(k_ref.shape[0])


def _attention(q, k, v, n_ctx):
    bsz, t_all, nq = q.shape
    gw = ATT_GROUP * ATT_HD
    return pl.pallas_call(
        functools.partial(_att_kernel, n_ctx=n_ctx),
        grid=(bsz, ATT_KV_HEADS, t_all // TM),
        in_specs=[pl.BlockSpec((None, TM, gw), lambda b, h, i: (b, i, h)),
                  pl.BlockSpec((None, t_all, ATT_HD), lambda b, h, i: (b, 0, h)),
                  pl.BlockSpec((None, t_all, ATT_HD), lambda b, h, i: (b, 0, h))],
        out_specs=pl.BlockSpec((None, TM, gw), lambda b, h, i: (b, i, h)),
        out_shape=jax.ShapeDtypeStruct((bsz, t_all, nq), BF16),
        compiler_params=_params("parallel", "parallel", "arbitrary"),
        name="gqa_mixer",
    )(q, k, v)


def _lru_kernel(x_ref, gl_ref, cw_ref, cb_ref, wa_ref, ba_ref, wx_ref, bx_ref, lam_ref, y_ref,
                af_ref, uf_ref, ab_ref, ub_ref, *, n_ctx):
    t_all, w = x_ref.shape
    c = SEQ_CHUNK
    ns = V7X_SUBLANES
    nv = c // ns
    n_chunks = t_all // c
    n_ctx_chunks = n_ctx // c
    sub = lax.broadcasted_iota(jnp.int32, (ns, 1), 0)

    def rows_of(chunk):
        return pl.ds(pl.multiple_of(chunk * c, c), c)

    def vreg_rows(chunk, v):
        return pl.ds(pl.multiple_of(chunk * c + v * ns, ns), ns)

    a_refs = (af_ref, ab_ref)
    u_refs = (uf_ref, ub_ref)
    rate = [(-0.5 * LRU_C) * jax.nn.softplus(-lam_ref[d:d + 1, :]) for d in range(2)]
    half_ba = [0.5 * ba_ref[d:d + 1, :] for d in range(2)]
    half_bx = [0.5 * bx_ref[d:d + 1, :] for d in range(2)]

    def one_later(cur, prev):
        return pltpu.roll(jnp.where(sub == ns - 1, prev, cur), 1, 0)

    def one_earlier(cur, nxt):
        return pltpu.roll(jnp.where(sub == 0, nxt, cur), ns - 1, 0)

    def gates(i, carry):
        x3 = x_ref[rows_of(i), :].reshape(nv, ns, w)
        has_prev = jnp.logical_and(i > 0, i != n_ctx_chunks)
        has_next = jnp.logical_and(i < n_chunks - 1, i != n_ctx_chunks - 1)
        ip = jnp.maximum(i - 1, 0)
        inx = jnp.minimum(i + 1, n_chunks - 1)
        prev_1 = jnp.where(has_prev, x_ref[vreg_rows(ip, nv - 1), :], 0.0)
        prev_2 = jnp.where(has_prev, x_ref[vreg_rows(ip, nv - 2), :], 0.0)
        next_0 = jnp.where(has_next, x_ref[vreg_rows(inx, 0), :], 0.0)
        ext = jnp.concatenate([one_later(x3[nv - 2], prev_2)[None], one_later(x3[nv - 1], prev_1)[None],
                               x3, one_earlier(x3[0], next_0)[None]], axis=0)
        xc = cb_ref[...] + ext[0:nv] * cw_ref[0:1, :]
        for j in range(1, LRU_CONV):
            xc = xc + ext[j:j + nv] * cw_ref[j:j + 1, :]
        xc = xc.reshape(c, w)
        xcb = xc.astype(BF16)
        half_xc = 0.5 * xc
        for d in range(2):
            th_r = jnp.tanh(_dot(xcb, wa_ref[d]) + half_ba[d])
            th_i = jnp.tanh(_dot(xcb, wx_ref[d]) + half_bx[d])
            log_a = rate[d] * (th_r + 1.0)
            a = jnp.exp(log_a)
            one_minus_a2 = jnp.tanh(log_a) * (-1.0 - a * a)
            a_refs[d][rows_of(i), :] = a
            u_refs[d][rows_of(i), :] = jnp.sqrt(one_minus_a2) * ((th_i + 1.0) * half_xc)
        return carry
    lax.fori_loop(0, n_chunks, gates, 0, unroll=2)

    def chunk_scan(a_ref, u_ref, chunk, h_in, reverse):
        a3 = a_ref[rows_of(chunk), :].reshape(nv, ns, w)
        u3 = u_ref[rows_of(chunk), :].reshape(nv, ns, w)
        order = list(range(nv - 1, -1, -1) if reverse else range(nv))
        h, p = u3[order[0]], a3[order[0]]
        hs, ps = {order[0]: h}, {order[0]: p}
        for v in order[1:]:
            h = a3[v] * h + u3[v]
            p = a3[v] * p
            hs[v], ps[v] = h, p
        s = 1
        while s < ns:
            sh = (ns - s) if reverse else s
            keep = (sub + s < ns) if reverse else (sub >= s)
            h = h + p * jnp.where(keep, pltpu.roll(h, sh, 0), 0.0)
            p = p * jnp.where(keep, pltpu.roll(p, sh, 0), 1.0)
            s *= 2
        end = h + p * h_in
        if reverse:
            enter = jnp.where(sub == ns - 1, h_in, pltpu.roll(end, ns - 1, 0))
            h_out = end[0:1, :]
        else:
            enter = jnp.where(sub == 0, h_in, pltpu.roll(end, 1, 0))
            h_out = end[ns - 1:ns, :]
        full = jnp.concatenate([(hs[v] + ps[v] * enter)[None] for v in range(nv)], axis=0)
        u_ref[rows_of(chunk), :] = full.reshape(c, w)
        return h_out

    def scan_body(i, carry):
        hf, hb = carry
        hf = chunk_scan(af_ref, uf_ref, i, hf, False)
        ib = jnp.where(i < n_ctx_chunks, n_ctx_chunks - 1 - i, n_chunks - 1 - (i - n_ctx_chunks))
        hb = chunk_scan(ab_ref, ub_ref, ib, hb, True)
        return hf, hb
    zero = jnp.zeros((1, w), F32)
    lax.fori_loop(0, n_chunks, scan_body, (zero, zero))

    def out(i, carry):
        rows = rows_of(i)
        y_ref[rows, :] = ((uf_ref[rows, :] + ub_ref[rows, :]) * gl_ref[rows, :].astype(F32)).astype(BF16)
        return carry
    lax.fori_loop(0, n_chunks, out, 0)


def _rglru(lx, lg, conv_w, conv_b, w_a, b_a, w_x, b_x, lam, n_ctx):
    bsz, t_all, width = lx.shape
    bw = width // LRU_BLOCKS
    seq = pl.BlockSpec((None, t_all, bw), lambda b, j: (b, 0, j))
    vec = lambda rows: pl.BlockSpec((rows, bw), lambda b, j: (0, j))
    wspec = pl.BlockSpec((2, None, bw, bw), lambda b, j: (0, j, 0, 0))
    return pl.pallas_call(
        functools.partial(_lru_kernel, n_ctx=n_ctx),
        grid=(bsz, LRU_BLOCKS),
        in_specs=[seq, seq, vec(LRU_CONV), vec(1), wspec, vec(2), wspec, vec(2), vec(2)],
        out_specs=seq,
        out_shape=jax.ShapeDtypeStruct((bsz, t_all, width), BF16),
        scratch_shapes=[pltpu.VMEM((t_all, bw), F32)] * 4,
        compiler_params=_params("parallel", "parallel"),
        name="rglru_mixer",
    )(lx, lg, conv_w, conv_b, w_a, b_a, w_x, b_x, lam)


def _merge_kernel(x_ref, a_ref, b_ref, r_ref, mg_ref, wbr_ref, wo_ref, gate_ref, o_ref):
    d = x_ref.shape[-1]
    y = None
    for i, br in enumerate((a_ref, b_ref, r_ref)):
        term = mg_ref[:, i * d:(i + 1) * d].astype(F32) * _dot(br[...], wbr_ref[i])
        y = term if y is None else y + term
    o_ref[...] = x_ref[...] + gate_ref[...] * _dot(y.astype(BF16), wo_ref[...])


def _ffn_kernel(x_ref, g_ref, sh_ref, sc_ref, gate_ref, wi_ref, wd_ref, *rest, final):
    x = x_ref[...]
    h = _norm_mod(x, g_ref[...], sh_ref[...], sc_ref[...]).astype(BF16)
    hidden = wd_ref.shape[0]
    act = (_silu(_dot(h, wi_ref[:, 0:hidden])) * _dot(h, wi_ref[:, hidden:2 * hidden])).astype(BF16)
    y = x + gate_ref[...] * _dot(act, wd_ref[...])
    if final:
        gf_ref, o_ref = rest
        ms = jnp.mean(y * y, axis=-1, keepdims=True)
        y = y * lax.rsqrt(ms + EPS) * gf_ref[...]
    else:
        (o_ref,) = rest
    o_ref[...] = y


def _merge_ffn(x, ya, yb, yr, mg, mods, w_br, w_o, g2, w_i, w_d, l, n_ctx_tiles, g_final):
    bsz, t_all, d = x.shape
    final = g_final is not None
    toff = n_ctx_tiles if final else 0
    n_t = t_all // TM - toff
    tile = lambda w: pl.BlockSpec((None, TM, w), lambda b, t: (b, t + toff, 0))
    out_tile = pl.BlockSpec((None, TM, d), lambda b, t: (b, t, 0))
    mod = lambda which: _mod_spec(mods, which, n_ctx_tiles, toff)
    full = lambda a: _layer_weight(a, l)
    out_sds = jax.ShapeDtypeStruct((bsz, n_t * TM, d), F32)
    x1 = pl.pallas_call(
        _merge_kernel,
        grid=(bsz, n_t),
        in_specs=[tile(d), tile(d), tile(d), tile(d), tile(N_BRANCH * d), full(w_br), full(w_o), mod(2)],
        out_specs=out_tile,
        out_shape=out_sds,
        compiler_params=_params("parallel", "parallel"),
        name="merge",
    )(x, ya, yb, yr, mg, w_br, w_o, mods)
    x1_tile = pl.BlockSpec((None, TM, d), lambda b, t: (b, t, 0))
    args = [x1, g2, mods, mods, mods, w_i, w_d]
    specs = [x1_tile, _vec_spec(d), mod(3), mod(4), mod(5), full(w_i), full(w_d)]
    if final:
        args.append(g_final)
        specs.append(_vec_spec(d))
    return pl.pallas_call(
        functools.partial(_ffn_kernel, final=final),
        grid=(bsz, n_t),
        in_specs=specs,
        out_specs=out_tile,
        out_shape=out_sds,
        compiler_params=_params("parallel", "parallel"),
        name="ffn",
    )(*args)


def _rope_tables(n_ctx, n_lat):
    rows = n_lat // GRID_W
    r = np.repeat(np.arange(rows), GRID_W).astype(np.float32)
    col = np.tile(np.arange(GRID_W), rows).astype(np.float32)
    half = ATT_HD // 2
    inv = np.power(np.float32(ROPE_THETA), -np.arange(0, half, 2, dtype=np.float32) / np.float32(half))
    ar = jnp.asarray(r[:, None] * inv[None, :], F32)
    ac = jnp.asarray(col[:, None] * inv[None, :], F32)
    cr, sr, cc, sc = jnp.cos(ar), jnp.sin(ar), jnp.cos(ac), jnp.sin(ac)
    cos_l = jnp.concatenate([cr, cr, cc, cc], axis=-1)
    sin_l = jnp.concatenate([-sr, sr, -sc, sc], axis=-1)
    cos_t = jnp.concatenate([jnp.ones((n_ctx, ATT_HD), F32), cos_l], axis=0)
    sin_t = jnp.concatenate([jnp.zeros((n_ctx, ATT_HD), F32), sin_l], axis=0)
    return _to_chunk_layout(cos_t[None])[0], _to_chunk_layout(sin_t[None])[0]


def kernel(x, c, ctx, c_ctx, w_mod, b_mod, norm1, w_in, hg_lb_logits, hg_out_norm, att_q_norm, att_k_norm, lru_conv_w, lru_conv_b, lru_w_a, lru_b_a, lru_w_x, lru_b_x, lru_lambda, w_branch, w_out, norm2, w_ffn_in, w_ffn_out, norm_final):
    bsz, n_lat, d = x.shape
    n_ctx = ctx.shape[1]
    depth = w_in.shape[0]
    assert n_ctx % TM == 0 and n_lat % TM == 0 and n_lat % GRID_W == 0
    n_ctx_tiles = n_ctx // TM
    w_in, w_branch, w_out, w_ffn_in, w_ffn_out = (
        w.astype(BF16) for w in (w_in, w_branch, w_out, w_ffn_in, w_ffn_out))

    c_all = jnp.concatenate([c, c_ctx[None, :]], axis=0)
    mods = _mod_vectors(c_all, w_mod, b_mod).reshape(depth, bsz + 1, N_MOD, 1, d)
    lb = _lower_bounds(hg_lb_logits)
    cos_t, sin_t = _rope_tables(n_ctx, n_lat)

    xs = _stream_in(ctx, x)
    for l in range(depth):
        last = l == depth - 1
        hg, rest = _inproj(xs, mods[l], norm1[l][None, :], w_in, l,
                           lb[0, l][None, :], lb[1, l][None, :],
                           att_q_norm[l][None, :], att_k_norm[l][None, :], cos_t, sin_t, n_ctx_tiles)
        qs, hv, ff, fb, sg = hg
        aq, ak, av, lx, lg, mg = rest
        ya = _hgrn(qs, hv, ff, fb, sg, hg_out_norm[l][None, :], n_ctx)
        yb = _attention(aq, ak, av, n_ctx)
        yr = _rglru(lx, lg, lru_conv_w[l], lru_conv_b[l][None, :], (0.5 * lru_w_a[l]).astype(BF16),
                    lru_b_a[l], (0.5 * lru_w_x[l]).astype(BF16), lru_b_x[l], lru_lambda[l], n_ctx)
        xs = _merge_ffn(xs, ya, yb, yr, mg, mods[l], w_branch, w_out, norm2[l][None, :],
                        w_ffn_in, w_ffn_out, l, n_ctx_tiles, norm_final[None, :] if last else None)
    return _stream_out(xs)
```

```python
import functools

import numpy as np
import jax
import jax.numpy as jnp
from jax import lax
from jax.experimental import pallas as pl
from jax.experimental.pallas import tpu as pltpu

F32 = jnp.float32
BF16 = jnp.bfloat16

EPS = 1e-6
GRID_W = 64
HG_HEADS = 8
HG_DK = 128
ATT_HEADS = 8
ATT_KV_HEADS = 2
ATT_GROUP = ATT_HEADS // ATT_KV_HEADS
ATT_HD = 128
ROPE_THETA = 10000.0
LRU_BLOCKS = 4
LRU_CONV = 4
LRU_C = 8.0
N_BRANCH = 3
N_MOD = 6

V7X_VMEM_BYTES = 64 * 1024 * 1024
V7X_SUBLANES = 8
VMEM_LIMIT = (V7X_VMEM_BYTES * 3) // 4

TM = 256
SEQ_CHUNK = 128
HG_CHUNK = SEQ_CHUNK
HG_HEADS_PER_STEP = 2


def _silu(x):
    return x * jax.nn.sigmoid(x)


def _gelu_tanh(x):
    return 0.5 * x * (1.0 + jnp.tanh(np.sqrt(2.0 / np.pi) * (x + 0.044715 * (x * x * x))))


def _dot(a, b):
    return jnp.dot(a, b, preferred_element_type=F32)


def _dot_nt(a, b):
    return lax.dot_general(a, b, (((1,), (1,)), ((), ())), preferred_element_type=F32)


def _dot_tn(a, b):
    return lax.dot_general(a, b, (((0,), (0,)), ((), ())), preferred_element_type=F32)


def _params(*sem):
    return pltpu.CompilerParams(dimension_semantics=sem, vmem_limit_bytes=VMEM_LIMIT)


def _mod_kernel(c_ref, w_ref, b_ref, o_ref):
    ca = _silu(c_ref[...]).astype(BF16)
    o_ref[...] = _dot(ca, w_ref[...].astype(BF16)) + b_ref[...]


def _mod_vectors(c_all, w_mod, b_mod):
    depth, d, n = w_mod.shape
    rows = c_all.shape[0]
    tn = n // 4
    return pl.pallas_call(
        _mod_kernel,
        grid=(depth, n // tn),
        in_specs=[pl.BlockSpec((rows, d), lambda l, j: (0, 0)),
                  pl.BlockSpec((None, d, tn), lambda l, j: (l, 0, j)),
                  pl.BlockSpec((None, 1, tn), lambda l, j: (l, 0, j))],
        out_specs=pl.BlockSpec((None, rows, tn), lambda l, j: (l, 0, j)),
        out_shape=jax.ShapeDtypeStruct((depth, rows, n), F32),
        compiler_params=_params("parallel", "parallel"),
        name="mod_vectors",
    )(c_all, w_mod, b_mod.reshape(depth, 1, n))


def _lb_kernel(x_ref, o_ref):
    depth = x_ref.shape[1]
    rows = [x_ref[:, l, :] for l in range(depth)]
    m = functools.reduce(jnp.maximum, rows)
    e = [jnp.exp(r - m) for r in rows]
    inv = 1.0 / functools.reduce(lambda a, b: a + b, e)
    acc = jnp.zeros_like(rows[0])
    o_ref[:, 0, :] = acc
    for l in range(1, depth):
        acc = acc + e[l] * inv
        o_ref[:, l, :] = acc


def _lower_bounds(logits):
    return pl.pallas_call(
        _lb_kernel,
        out_shape=jax.ShapeDtypeStruct(logits.shape, F32),
        name="hg_lower_bounds",
    )(logits.astype(F32))


def _norm_mod(x, g, shift, scale):
    ms = jnp.mean(x * x, axis=-1, keepdims=True)
    y = x * lax.rsqrt(ms + EPS) * g
    return y * (1.0 + scale) + shift


def _inproj_hg_kernel(x_ref, g_ref, sh_ref, sc_ref, w_ref, lbf_ref, lbb_ref,
                      qs_ref, v_ref, ff_ref, fb_ref, sg_ref):
    h = _norm_mod(x_ref[...], g_ref[...], sh_ref[...], sc_ref[...]).astype(BF16)
    w = qs_ref.shape[-1]
    z = _dot(h, w_ref[:, 0:w])
    qs_ref[...] = (_silu(z) * (HG_DK ** -0.5)).astype(BF16)
    v_ref[...] = _dot(h, w_ref[:, w:2 * w]).astype(BF16)
    for k, (lb_ref, f_ref) in enumerate(((lbf_ref, ff_ref), (lbb_ref, fb_ref))):
        lb = lb_ref[...]
        z = _dot(h, w_ref[:, (2 + k) * w:(3 + k) * w])
        f_ref[...] = lb + (1.0 - lb) * jax.nn.sigmoid(z)
    sg_ref[...] = _silu(_dot(h, w_ref[:, 4 * w:5 * w])).astype(BF16)


def _inproj_rest_kernel(x_ref, g_ref, sh_ref, sc_ref, w_ref, qg_ref, kg_ref, cos_ref, sin_ref,
                        q_ref, k_ref, v_ref, lx_ref, lg_ref, mg_ref, *, col0):
    h = _norm_mod(x_ref[...], g_ref[...], sh_ref[...], sc_ref[...]).astype(BF16)
    cos = cos_ref[...]
    sin = sin_ref[...]
    lane = lax.broadcasted_iota(jnp.int32, cos.shape, 1)
    quarter = ATT_HD // 4
    first = (lane % (2 * quarter)) < quarter

    def norm_rope(z, gain, scale):
        ms = jnp.mean(z * z, axis=-1, keepdims=True)
        y = z * lax.rsqrt(ms + EPS) * gain
        swapped = jnp.where(first, pltpu.roll(y, ATT_HD - quarter, 1), pltpu.roll(y, quarter, 1))
        return ((y * cos + swapped * sin) * scale).astype(BF16)

    nq = q_ref.shape[-1]
    nkv = k_ref.shape[-1]
    nl = lx_ref.shape[-1]
    off = col0
    z = _dot(h, w_ref[:, off:off + nq])
    for i in range(nq // ATT_HD):
        sl = slice(i * ATT_HD, (i + 1) * ATT_HD)
        q_ref[:, sl] = norm_rope(z[:, sl], qg_ref[...], ATT_HD ** -0.5 * np.log2(np.e))
    off += nq
    z = _dot(h, w_ref[:, off:off + nkv])
    for i in range(nkv // ATT_HD):
        sl = slice(i * ATT_HD, (i + 1) * ATT_HD)
        k_ref[:, sl] = norm_rope(z[:, sl], kg_ref[...], 1.0)
    off += nkv
    v_ref[...] = _dot(h, w_ref[:, off:off + nkv]).astype(BF16)
    off += nkv
    lx_ref[...] = _dot(h, w_ref[:, off:off + nl])
    off += nl
    lg_ref[...] = _gelu_tanh(_dot(h, w_ref[:, off:off + nl])).astype(BF16)
    off += nl
    mg_ref[...] = jax.nn.sigmoid(_dot(h, w_ref[:, off:off + mg_ref.shape[-1]])).astype(BF16)


def _tile_spec(width):
    return pl.BlockSpec((None, TM, width), lambda b, t: (b, t, 0))


def _vec_spec(width):
    return pl.BlockSpec((1, width), lambda b, t: (0, 0))


def _mod_spec(mods, which, n_ctx_tiles, toff=0):
    ctx_row = mods.shape[0] - 1
    d = mods.shape[-1]
    return pl.BlockSpec((None, None, 1, d),
                        lambda b, t: (jnp.where(t + toff < n_ctx_tiles, ctx_row, b), which, 0, 0))


def _layer_weight(w_all, l, cols=None):
    shape = w_all.shape[1:] if cols is None else (*w_all.shape[1:-1], cols)
    return pl.BlockSpec((None, *shape), lambda b, t: (l,) + (0,) * len(shape),
                        pipeline_mode=pl.Buffered(1))


def _inproj(x, mods, g1, w_in, l, lbf, lbb, qg, kg, cos_t, sin_t, n_ctx_tiles):
    bsz, t_all, d = x.shape
    grid = (bsz, t_all // TM)
    hw = HG_HEADS * HG_DK
    nq = ATT_HEADS * ATT_HD
    nkv = ATT_KV_HEADS * ATT_HD
    common = [_tile_spec(d), _vec_spec(d), _mod_spec(mods, 0, n_ctx_tiles),
              _mod_spec(mods, 1, n_ctx_tiles)]
    sds = lambda w, dt: jax.ShapeDtypeStruct((bsz, t_all, w), dt)
    hg = pl.pallas_call(
        _inproj_hg_kernel,
        grid=grid,
        in_specs=common + [_layer_weight(w_in, l, 5 * hw), _vec_spec(hw), _vec_spec(hw)],
        out_specs=[_tile_spec(hw)] * 5,
        out_shape=[sds(hw, BF16), sds(hw, BF16), sds(hw, F32), sds(hw, F32), sds(hw, BF16)],
        compiler_params=_params("parallel", "parallel"),
        name="inproj_hgrn",
    )(x, g1, mods, mods, w_in, lbf, lbb)
    nl = hw
    ng = w_in.shape[-1] - 5 * hw - nq - 2 * nkv - 2 * nl
    rest = pl.pallas_call(
        functools.partial(_inproj_rest_kernel, col0=5 * hw),
        grid=grid,
        in_specs=common + [_layer_weight(w_in, l),
                           _vec_spec(ATT_HD), _vec_spec(ATT_HD),
                           pl.BlockSpec((TM, ATT_HD), lambda b, t: (t, 0)),
                           pl.BlockSpec((TM, ATT_HD), lambda b, t: (t, 0))],
        out_specs=[_tile_spec(nq), _tile_spec(nkv), _tile_spec(nkv), _tile_spec(nl), _tile_spec(nl),
                   _tile_spec(ng)],
        out_shape=[sds(nq, BF16), sds(nkv, BF16), sds(nkv, BF16), sds(nl, F32), sds(nl, BF16),
                   sds(ng, BF16)],
        compiler_params=_params("parallel", "parallel"),
        name="inproj_rest",
    )(x, g1, mods, mods, w_in, qg, kg, cos_t, sin_t)
    return hg, rest


def _chunk_token(r):
    nv = SEQ_CHUNK // V7X_SUBLANES
    return (r % V7X_SUBLANES) * nv + r // V7X_SUBLANES


def _to_chunk_layout(a):
    nv = SEQ_CHUNK // V7X_SUBLANES
    shp = a.shape
    a = a.reshape(shp[0], shp[1] // SEQ_CHUNK, V7X_SUBLANES, nv, *shp[2:])
    return jnp.swapaxes(a, 2, 3).reshape(shp)


def _split3(g):
    g1 = g.astype(BF16)
    r1 = g - g1.astype(F32)
    g2 = r1.astype(BF16)
    g3 = (r1 - g2.astype(F32)).astype(BF16)
    return jnp.concatenate([g1, g2, g3], axis=-1)


def _move_tokens(x, to_chunk):
    rows, d = x.shape
    nv = SEQ_CHUNK // V7X_SUBLANES
    r = lax.broadcasted_iota(jnp.int32, (rows, rows), 0)
    i = r % SEQ_CHUNK
    src = _chunk_token(i) if to_chunk else (i % nv) * V7X_SUBLANES + i // nv
    pick = lax.broadcasted_iota(jnp.int32, (rows, rows), 1) == (r - i) + src
    parts = _dot(jnp.where(pick, 1.0, 0.0).astype(BF16), _split3(x))
    return (parts[:, 0:d] + parts[:, d:2 * d]) + parts[:, 2 * d:3 * d]


def _stream_in_kernel(ctx_ref, x_ref, o_ref, *, n_ctx_tiles):
    @pl.when(pl.program_id(1) < n_ctx_tiles)
    def _():
        o_ref[...] = _move_tokens(ctx_ref[...], True)

    @pl.when(pl.program_id(1) >= n_ctx_tiles)
    def _():
        o_ref[...] = _move_tokens(x_ref[...], True)


def _stream_out_kernel(x_ref, o_ref):
    o_ref[...] = _move_tokens(x_ref[...], False)


def _stream_in(ctx, x):
    bsz, n_ctx, d = ctx.shape
    n_ctx_tiles = n_ctx // TM
    n_tiles = n_ctx_tiles + x.shape[1] // TM
    return pl.pallas_call(
        functools.partial(_stream_in_kernel, n_ctx_tiles=n_ctx_tiles),
        grid=(bsz, n_tiles),
        in_specs=[pl.BlockSpec((None, TM, d), lambda b, t: (b, jnp.minimum(t, n_ctx_tiles - 1), 0)),
                  pl.BlockSpec((None, TM, d), lambda b, t: (b, jnp.maximum(t - n_ctx_tiles, 0), 0))],
        out_specs=pl.BlockSpec((None, TM, d), lambda b, t: (b, t, 0)),
        out_shape=jax.ShapeDtypeStruct((bsz, n_tiles * TM, d), x.dtype),
        compiler_params=_params("parallel", "arbitrary"),
        name="stream_in",
    )(ctx, x)


def _stream_out(xs):
    bsz, n, d = xs.shape
    blk = pl.BlockSpec((None, TM, d), lambda b, t: (b, t, 0))
    return pl.pallas_call(
        _stream_out_kernel,
        grid=(bsz, n // TM),
        in_specs=[blk],
        out_specs=blk,
        out_shape=jax.ShapeDtypeStruct((bsz, n, d), xs.dtype),
        compiler_params=_params("parallel", "parallel"),
        name="stream_out",
    )(xs)


def _block_row(y, blk, ref_row):
    pos = lax.broadcasted_iota(jnp.int32, (V7X_SUBLANES, 1), 0) % blk
    s = 1
    while ref_row + s < blk:
        y = jnp.where((pos >= ref_row + s) & (pos < ref_row + 2 * s), pltpu.roll(y, s, 0), y)
        s *= 2
    s = 1
    while ref_row - s >= 0:
        y = jnp.where((pos <= ref_row - s) & (pos > ref_row - 2 * s),
                      pltpu.roll(y, V7X_SUBLANES - s, 0), y)
        s *= 2
    return y


def _level_exponent(b, blk, reverse):
    c, width = b.shape
    nv = c // V7X_SUBLANES
    half = blk // 2
    b3 = b.reshape(nv, V7X_SUBLANES, width)
    if blk <= nv:
        ref = half - 1 if reverse else half
        d = []
        for v0 in range(0, nv, blk):
            r = b3[v0 + ref:v0 + ref + 1]
            lo, hi = b3[v0:v0 + half], b3[v0 + half:v0 + blk]
            d += [lo - r, r - hi] if reverse else [r - lo, hi - r]
        d = jnp.concatenate(d, axis=0)
    else:
        sub_blk = blk // nv
        if reverse:
            ref = _block_row(b3[nv - 1], sub_blk, sub_blk // 2 - 1)
        else:
            ref = _block_row(b3[0], sub_blk, sub_blk // 2)
        d = _neg_abs(b3 - ref[None])
    return d.reshape(c, width)


def _neg_abs(x):
    bits = lax.bitcast_convert_type(x, jnp.uint32) | jnp.uint32(0x80000000)
    return lax.bitcast_convert_type(bits, F32)


def _hg_chunks(probs, pair_xor):
    c, width = probs[0]["q"].shape
    heads = [slice(h * HG_DK, (h + 1) * HG_DK) for h in range(width // HG_DK)]
    for p in probs:
        p["kk"] = (1.0 - p["f"]).astype(BF16)
        b = p["b"]
        qe = p["q"] * jnp.exp2(b).astype(BF16)
        p["o"] = [_dot_nt(qe[:, h], s.astype(BF16)) for h, s in zip(heads, p["st"])]
        b_end = b[0:1, :] if p["reverse"] else b[c - 1:c, :]
        ke = p["kk"] * jnp.exp2(b_end - b).astype(BF16)
        st_scale = jnp.exp2(b_end)
        p["st"] = [s * st_scale[:, h] + _dot_tn(p["v"][:, h], ke[:, h]) for h, s in zip(heads, p["st"])]

    blk = c
    while blk >= 4:
        for p in probs:
            e = jnp.exp2(_level_exponent(p["b"], blk, p["reverse"])).astype(BF16)
            qw = p["q"] * e
            kw = p["kk"] * e
            for n, h in enumerate(heads):
                pltpu.store(p["pairs"].at[n], _dot_nt(qw[:, h], kw[:, h]),
                            mask=None if blk == c else pair_xor < blk)
        blk //= 2
    nv = c // V7X_SUBLANES
    vreg = lax.broadcasted_iota(jnp.int32, (nv, 1, 1), 0)
    for p in probs:
        q3, k3, v3 = (p[name].astype(F32).reshape(nv, V7X_SUBLANES, width) for name in ("q", "kk", "v"))
        f3 = p["f"].reshape(nv, V7X_SUBLANES, width)
        if p["reverse"]:
            k_nb, v_nb = (jnp.concatenate([a[1:], a[nv - 1:]], axis=0) for a in (k3, v3))
            w_nb, has_nb = q3 * k_nb * f3, vreg % 2 == 0
        else:
            k_nb, v_nb = (jnp.concatenate([a[:1], a[:nv - 1]], axis=0) for a in (k3, v3))
            w_nb, has_nb = q3 * k_nb * f3, vreg % 2 == 1
        w_self = q3 * k3
        pairs = []
        for n, h in enumerate(heads):
            near = (jnp.sum(w_self[:, :, h], axis=-1, keepdims=True) * v3[:, :, h]
                    + jnp.where(has_nb, jnp.sum(w_nb[:, :, h], axis=-1, keepdims=True), 0.0) * v_nb[:, :, h])
            p["o"][n] = p["o"][n] + near.reshape(c, HG_DK)
            keep = p["causal"] & (pair_xor >= 2)
            pairs.append(jnp.where(keep, p["pairs"][n], 0.0).astype(BF16))
        p["a"] = pairs
    return ([jnp.concatenate(p["o"], axis=-1) for p in probs], [p["a"] for p in probs],
            [p["st"] for p in probs])


def _hg_running_decay(f, tri):
    width = f.shape[-1]
    g = jnp.log2(f)
    g1 = g.astype(BF16)
    g2 = (g - g1.astype(F32)).astype(BF16)
    parts = _dot(tri, jnp.concatenate([g1, g2], axis=-1))
    return parts[:, 0:width] + parts[:, width:2 * width]


def _hgrn_kernel(qs_ref, v_ref, ff_ref, fb_ref, sg_ref, gain_ref, y_ref, of_ref, ob_ref, pair_ref,
                 *, n_ctx):
    t_all, width = qs_ref.shape
    kdim = HG_DK
    c = HG_CHUNK
    n_chunks = t_all // c
    n_ctx_chunks = n_ctx // c
    row = _chunk_token(lax.broadcasted_iota(jnp.int32, (c, c), 0))
    col = _chunk_token(lax.broadcasted_iota(jnp.int32, (c, c), 1))
    pair_xor = row ^ col
    tri_f = jnp.where(col <= row, 1.0, 0.0).astype(BF16)
    tri_b = jnp.where(col >= row, 1.0, 0.0).astype(BF16)

    n_heads = width // kdim
    f_refs = (ff_ref, fb_ref)
    o_refs = (of_ref, ob_ref)
    tris = (tri_f, tri_b)

    def chunk_ids(i):
        i = jnp.minimum(i, n_chunks - 1)
        ib = jnp.where(i < n_ctx_chunks, n_ctx_chunks - 1 - i, n_chunks - 1 - (i - n_ctx_chunks))
        return (i, ib)

    def rows_of(chunk):
        return pl.ds(pl.multiple_of(chunk * c, c), c)

    def finish(chunks, pairs):
        for o_ref, chunk, a in zip(o_refs, chunks, pairs):
            rows = rows_of(chunk)
            add = [_dot(a[h], v_ref[rows, h * kdim:(h + 1) * kdim]) for h in range(n_heads)]
            o_ref[rows, :] += jnp.concatenate(add, axis=-1)

    def body(i, carry):
        sts, decays, prev_chunks, prev_pairs = carry
        finish(prev_chunks, prev_pairs)
        chunks = chunk_ids(i)
        probs = [dict(q=qs_ref[rows_of(ch), :], v=v_ref[rows_of(ch), :], f=f_ref[rows_of(ch), :],
                      b=b, st=st, reverse=d == 1, causal=(col >= row) if d else (col <= row),
                      pairs=pair_ref.at[d])
                 for d, (ch, f_ref, b, st) in enumerate(zip(chunks, f_refs, decays, sts))]
        outs, pairs, sts = _hg_chunks(probs, pair_xor)
        for o_ref, ch, o in zip(o_refs, chunks, outs):
            o_ref[rows_of(ch), :] = o
        decays = [_hg_running_decay(f_ref[rows_of(ch), :], tri)
                  for ch, f_ref, tri in zip(chunk_ids(i + 1), f_refs, tris)]
        return sts, decays, chunks, pairs

    first = chunk_ids(0)
    for o_ref, ch in zip(o_refs, first):
        o_ref[rows_of(ch), :] = jnp.zeros((c, width), F32)
    zero_st = [[jnp.zeros((kdim, kdim), F32)] * n_heads] * 2
    zero_pairs = [[jnp.zeros((c, c), BF16)] * n_heads] * 2
    decays0 = [_hg_running_decay(f_ref[rows_of(ch), :], tri)
               for ch, f_ref, tri in zip(first, f_refs, tris)]
    _, _, last_chunks, last_pairs = lax.fori_loop(
        0, n_chunks, body, (zero_st, decays0, first, zero_pairs))
    finish(last_chunks, last_pairs)

    def readout(i, carry):
        rows = pl.ds(pl.multiple_of(i * c, c), c)
        for h in range(width // kdim):
            sl = slice(h * kdim, (h + 1) * kdim)
            o = of_ref[rows, sl] + ob_ref[rows, sl]
            ms = jnp.mean(o * o, axis=-1, keepdims=True)
            y = o * lax.rsqrt(ms + EPS) * gain_ref[...]
            y_ref[rows, sl] = (y * sg_ref[rows, sl].astype(F32)).astype(BF16)
        return carry
    lax.fori_loop(0, n_chunks, readout, 0, unroll=2)


def _hgrn(qs, v, ff, fb, sg, gain, n_ctx):
    bsz, t_all, width = qs.shape
    bw = HG_HEADS_PER_STEP * HG_DK
    spec = pl.BlockSpec((None, t_all, bw), lambda b, h: (b, 0, h))
    return pl.pallas_call(
        functools.partial(_hgrn_kernel, n_ctx=n_ctx),
        grid=(bsz, width // bw),
        in_specs=[spec] * 5 + [pl.BlockSpec((1, HG_DK), lambda b, h: (0, 0))],
        out_specs=spec,
        out_shape=jax.ShapeDtypeStruct((bsz, t_all, width), BF16),
        scratch_shapes=[pltpu.VMEM((t_all, bw), F32), pltpu.VMEM((t_all, bw), F32),
                        pltpu.VMEM((2, HG_HEADS_PER_STEP, HG_CHUNK, HG_CHUNK), F32)],
        compiler_params=_params("parallel", "parallel"),
        name="hgrn2_mixer",
    )(qs, v, ff, fb, sg, gain)


def _att_kernel(q_ref, k_ref, v_ref, o_ref, *, n_ctx):
    qi = pl.program_id(2)
    n_ctx_tiles = n_ctx // q_ref.shape[0]

    def run(nk):
        k = k_ref[0:nk, :]
        v = v_ref[0:nk, :]
        heads = [slice(g * ATT_HD, (g + 1) * ATT_HD) for g in range(ATT_GROUP)]
        s_next = _dot_nt(q_ref[:, heads[0]], k)
        for g, sl in enumerate(heads):
            s = s_next
            if g + 1 < ATT_GROUP:
                s_next = _dot_nt(q_ref[:, heads[g + 1]], k)
            p = jnp.exp2(s - jnp.max(s, axis=-1, keepdims=True))
            den = jnp.sum(p, axis=-1, keepdims=True)
            o_ref[:, sl] = (_dot(p.astype(BF16), v) / den).astype(BF16)

    @pl.when(qi < n_ctx_tiles)
    def _():
        run(n_ctx)

    @pl.when(qi >= n_ctx_tiles)
    def _():
        run(k_ref.shape[0])


def _attention(q, k, v, n_ctx):
    bsz, t_all, nq = q.shape
    gw = ATT_GROUP * ATT_HD
    return pl.pallas_call(
        functools.partial(_att_kernel, n_ctx=n_ctx),
        grid=(bsz, ATT_KV_HEADS, t_all // TM),
        in_specs=[pl.BlockSpec((None, TM, gw), lambda b, h, i: (b, i, h)),
                  pl.BlockSpec((None, t_all, ATT_HD), lambda b, h, i: (b, 0, h)),
                  pl.BlockSpec((None, t_all, ATT_HD), lambda b, h, i: (b, 0, h))],
        out_specs=pl.BlockSpec((None, TM, gw), lambda b, h, i: (b, i, h)),
        out_shape=jax.ShapeDtypeStruct((bsz, t_all, nq), BF16),
        compiler_params=_params("parallel", "parallel", "arbitrary"),
        name="gqa_mixer",
    )(q, k, v)


def _lru_kernel(x_ref, gl_ref, cw_ref, cb_ref, wa_ref, ba_ref, wx_ref, bx_ref, lam_ref, y_ref,
                af_ref, uf_ref, ab_ref, ub_ref, *, n_ctx):
    t_all, w = x_ref.shape
    c = SEQ_CHUNK
    ns = V7X_SUBLANES
    nv = c // ns
    n_chunks = t_all // c
    n_ctx_chunks = n_ctx // c
    sub = lax.broadcasted_iota(jnp.int32, (ns, 1), 0)

    def rows_of(chunk):
        return pl.ds(pl.multiple_of(chunk * c, c), c)

    def vreg_rows(chunk, v):
        return pl.ds(pl.multiple_of(chunk * c + v * ns, ns), ns)

    a_refs = (af_ref, ab_ref)
    u_refs = (uf_ref, ub_ref)
    rate = [(-0.5 * LRU_C) * jax.nn.softplus(-lam_ref[d:d + 1, :]) for d in range(2)]
    half_ba = [0.5 * ba_ref[d:d + 1, :] for d in range(2)]
    half_bx = [0.5 * bx_ref[d:d + 1, :] for d in range(2)]

    def one_later(cur, prev):
        return pltpu.roll(jnp.where(sub == ns - 1, prev, cur), 1, 0)

    def one_earlier(cur, nxt):
        return pltpu.roll(jnp.where(sub == 0, nxt, cur), ns - 1, 0)

    def gates(i, carry):
        x3 = x_ref[rows_of(i), :].reshape(nv, ns, w)
        has_prev = jnp.logical_and(i > 0, i != n_ctx_chunks)
        has_next = jnp.logical_and(i < n_chunks - 1, i != n_ctx_chunks - 1)
        ip = jnp.maximum(i - 1, 0)
        inx = jnp.minimum(i + 1, n_chunks - 1)
        prev_1 = jnp.where(has_prev, x_ref[vreg_rows(ip, nv - 1), :], 0.0)
        prev_2 = jnp.where(has_prev, x_ref[vreg_rows(ip, nv - 2), :], 0.0)
        next_0 = jnp.where(has_next, x_ref[vreg_rows(inx, 0), :], 0.0)
        ext = jnp.concatenate([one_later(x3[nv - 2], prev_2)[None], one_later(x3[nv - 1], prev_1)[None],
                               x3, one_earlier(x3[0], next_0)[None]], axis=0)
        xc = cb_ref[...] + ext[0:nv] * cw_ref[0:1, :]
        for j in range(1, LRU_CONV):
            xc = xc + ext[j:j + nv] * cw_ref[j:j + 1, :]
        xc = xc.reshape(c, w)
        xcb = xc.astype(BF16)
        half_xc = 0.5 * xc
        for d in range(2):
            th_r = jnp.tanh(_dot(xcb, wa_ref[d]) + half_ba[d])
            th_i = jnp.tanh(_dot(xcb, wx_ref[d]) + half_bx[d])
            log_a = rate[d] * (th_r + 1.0)
            a = jnp.exp(log_a)
            one_minus_a2 = jnp.tanh(log_a) * (-1.0 - a * a)
            a_refs[d][rows_of(i), :] = a
            u_refs[d][rows_of(i), :] = jnp.sqrt(one_minus_a2) * ((th_i + 1.0) * half_xc)
        return carry
    lax.fori_loop(0, n_chunks, gates, 0, unroll=2)

    def chunk_scan(a_ref, u_ref, chunk, h_in, reverse):
        a3 = a_ref[rows_of(chunk), :].reshape(nv, ns, w)
        u3 = u_ref[rows_of(chunk), :].reshape(nv, ns, w)
        order = list(range(nv - 1, -1, -1) if reverse else range(nv))
        h, p = u3[order[0]], a3[order[0]]
        hs, ps = {order[0]: h}, {order[0]: p}
        for v in order[1:]:
            h = a3[v] * h + u3[v]
            p = a3[v] * p
            hs[v], ps[v] = h, p
        s = 1
        while s < ns:
            sh = (ns - s) if reverse else s
            keep = (sub + s < ns) if reverse else (sub >= s)
            h = h + p * jnp.where(keep, pltpu.roll(h, sh, 0), 0.0)
            p = p * jnp.where(keep, pltpu.roll(p, sh, 0), 1.0)
            s *= 2
        end = h + p * h_in
        if reverse:
            enter = jnp.where(sub == ns - 1, h_in, pltpu.roll(end, ns - 1, 0))
            h_out = end[0:1, :]
        else:
            enter = jnp.where(sub == 0, h_in, pltpu.roll(end, 1, 0))
            h_out = end[ns - 1:ns, :]
        full = jnp.concatenate([(hs[v] + ps[v] * enter)[None] for v in range(nv)], axis=0)
        u_ref[rows_of(chunk), :] = full.reshape(c, w)
        return h_out

    def scan_body(i, carry):
        hf, hb = carry
        hf = chunk_scan(af_ref, uf_ref, i, hf, False)
        ib = jnp.where(i < n_ctx_chunks, n_ctx_chunks - 1 - i, n_chunks - 1 - (i - n_ctx_chunks))
        hb = chunk_scan(ab_ref, ub_ref, ib, hb, True)
        return hf, hb
    zero = jnp.zeros((1, w), F32)
    lax.fori_loop(0, n_chunks, scan_body, (zero, zero))

    def out(i, carry):
        rows = rows_of(i)
        y_ref[rows, :] = ((uf_ref[rows, :] + ub_ref[rows, :]) * gl_ref[rows, :].astype(F32)).astype(BF16)
        return carry
    lax.fori_loop(0, n_chunks, out, 0)


def _rglru(lx, lg, conv_w, conv_b, w_a, b_a, w_x, b_x, lam, n_ctx):
    bsz, t_all, width = lx.shape
    bw = width // LRU_BLOCKS
    seq = pl.BlockSpec((None, t_all, bw), lambda b, j: (b, 0, j))
    vec = lambda rows: pl.BlockSpec((rows, bw), lambda b, j: (0, j))
    wspec = pl.BlockSpec((2, None, bw, bw), lambda b, j: (0, j, 0, 0))
    return pl.pallas_call(
        functools.partial(_lru_kernel, n_ctx=n_ctx),
        grid=(bsz, LRU_BLOCKS),
        in_specs=[seq, seq, vec(LRU_CONV), vec(1), wspec, vec(2), wspec, vec(2), vec(2)],
        out_specs=seq,
        out_shape=jax.ShapeDtypeStruct((bsz, t_all, width), BF16),
        scratch_shapes=[pltpu.VMEM((t_all, bw), F32)] * 4,
        compiler_params=_params("parallel", "parallel"),
        name="rglru_mixer",
    )(lx, lg, conv_w, conv_b, w_a, b_a, w_x, b_x, lam)


def _merge_kernel(x_ref, a_ref, b_ref, r_ref, mg_ref, wbr_ref, wo_ref, gate_ref, o_ref):
    d = x_ref.shape[-1]
    y = None
    for i, br in enumerate((a_ref, b_ref, r_ref)):
        term = mg_ref[:, i * d:(i + 1) * d].astype(F32) * _dot(br[...], wbr_ref[i])
        y = term if y is None else y + term
    o_ref[...] = x_ref[...] + gate_ref[...] * _dot(y.astype(BF16), wo_ref[...])


def _ffn_kernel(x_ref, g_ref, sh_ref, sc_ref, gate_ref, wi_ref, wd_ref, *rest, final):
    x = x_ref[...]
    h = _norm_mod(x, g_ref[...], sh_ref[...], sc_ref[...]).astype(BF16)
    hidden = wd_ref.shape[0]
    act = (_silu(_dot(h, wi_ref[:, 0:hidden])) * _dot(h, wi_ref[:, hidden:2 * hidden])).astype(BF16)
    y = x + gate_ref[...] * _dot(act, wd_ref[...])
    if final:
        gf_ref, o_ref = rest
        ms = jnp.mean(y * y, axis=-1, keepdims=True)
        y = y * lax.rsqrt(ms + EPS) * gf_ref[...]
    else:
        (o_ref,) = rest
    o_ref[...] = y


def _merge_ffn(x, ya, yb, yr, mg, mods, w_br, w_o, g2, w_i, w_d, l, n_ctx_tiles, g_final):
    bsz, t_all, d = x.shape
    final = g_final is not None
    toff = n_ctx_tiles if final else 0
    n_t = t_all // TM - toff
    tile = lambda w: pl.BlockSpec((None, TM, w), lambda b, t: (b, t + toff, 0))
    out_tile = pl.BlockSpec((None, TM, d), lambda b, t: (b, t, 0))
    mod = lambda which: _mod_spec(mods, which, n_ctx_tiles, toff)
    full = lambda a: _layer_weight(a, l)
    out_sds = jax.ShapeDtypeStruct((bsz, n_t * TM, d), F32)
    x1 = pl.pallas_call(
        _merge_kernel,
        grid=(bsz, n_t),
        in_specs=[tile(d), tile(d), tile(d), tile(d), tile(N_BRANCH * d), full(w_br), full(w_o), mod(2)],
        out_specs=out_tile,
        out_shape=out_sds,
        compiler_params=_params("parallel", "parallel"),
        name="merge",
    )(x, ya, yb, yr, mg, w_br, w_o, mods)
    x1_tile = pl.BlockSpec((None, TM, d), lambda b, t: (b, t, 0))
    args = [x1, g2, mods, mods, mods, w_i, w_d]
    specs = [x1_tile, _vec_spec(d), mod(3), mod(4), mod(5), full(w_i), full(w_d)]
    if final:
        args.append(g_final)
        specs.append(_vec_spec(d))
    return pl.pallas_call(
        functools.partial(_ffn_kernel, final=final),
        grid=(bsz, n_t),
        in_specs=specs,
        out_specs=out_tile,
        out_shape=out_sds,
        compiler_params=_params("parallel", "parallel"),
        name="ffn",
    )(*args)


def _rope_tables(n_ctx, n_lat):
    rows = n_lat // GRID_W
    r = np.repeat(np.arange(rows), GRID_W).astype(np.float32)
    col = np.tile(np.arange(GRID_W), rows).astype(np.float32)
    half = ATT_HD // 2
    inv = np.power(np.float32(ROPE_THETA), -np.arange(0, half, 2, dtype=np.float32) / np.float32(half))
    ar = jnp.asarray(r[:, None] * inv[None, :], F32)
    ac = jnp.asarray(col[:, None] * inv[None, :], F32)
    cr, sr, cc, sc = jnp.cos(ar), jnp.sin(ar), jnp.cos(ac), jnp.sin(ac)
    cos_l = jnp.concatenate([cr, cr, cc, cc], axis=-1)
    sin_l = jnp.concatenate([-sr, sr, -sc, sc], axis=-1)
    cos_t = jnp.concatenate([jnp.ones((n_ctx, ATT_HD), F32), cos_l], axis=0)
    sin_t = jnp.concatenate([jnp.zeros((n_ctx, ATT_HD), F32), sin_l], axis=0)
    return _to_chunk_layout(cos_t[None])[0], _to_chunk_layout(sin_t[None])[0]


def kernel(x, c, ctx, c_ctx, w_mod, b_mod, norm1, w_in, hg_lb_logits, hg_out_norm, att_q_norm, att_k_norm, lru_conv_w, lru_conv_b, lru_w_a, lru_b_a, lru_w_x, lru_b_x, lru_lambda, w_branch, w_out, norm2, w_ffn_in, w_ffn_out, norm_final):
    bsz, n_lat, d = x.shape
    n_ctx = ctx.shape[1]
    depth = w_in.shape[0]
    assert n_ctx % TM == 0 and n_lat % TM == 0 and n_lat % GRID_W == 0
    n_ctx_tiles = n_ctx // TM
    w_in, w_branch, w_out, w_ffn_in, w_ffn_out = (
        w.astype(BF16) for w in (w_in, w_branch, w_out, w_ffn_in, w_ffn_out))

    c_all = jnp.concatenate([c, c_ctx[None, :]], axis=0)
    mods = _mod_vectors(c_all, w_mod, b_mod).reshape(depth, bsz + 1, N_MOD, 1, d)
    lb = _lower_bounds(hg_lb_logits)
    cos_t, sin_t = _rope_tables(n_ctx, n_lat)

    xs = _stream_in(ctx, x)
    for l in range(depth):
        last = l == depth - 1
        hg, rest = _inproj(xs, mods[l], norm1[l][None, :], w_in, l,
                           lb[0, l][None, :], lb[1, l][None, :],
                           att_q_norm[l][None, :], att_k_norm[l][None, :], cos_t, sin_t, n_ctx_tiles)
        qs, hv, ff, fb, sg = hg
        aq, ak, av, lx, lg, mg = rest
        ya = _hgrn(qs, hv, ff, fb, sg, hg_out_norm[l][None, :], n_ctx)
        yb = _attention(aq, ak, av, n_ctx)
        yr = _rglru(lx, lg, lru_conv_w[l], lru_conv_b[l][None, :], (0.5 * lru_w_a[l]).astype(BF16),
                    lru_b_a[l], (0.5 * lru_w_x[l]).astype(BF16), lru_b_x[l], lru_lambda[l], n_ctx)
        xs = _merge_ffn(xs, ya, yb, yr, mg, mods[l], w_branch, w_out, norm2[l][None, :],
                        w_ffn_in, w_ffn_out, l, n_ctx_tiles, norm_final[None, :] if last else None)
    return _stream_out(xs)
```

```python
import functools

import numpy as np
import jax
import jax.numpy as jnp
from jax import lax
from jax.experimental import pallas as pl
from jax.experimental.pallas import tpu as pltpu

F32 = jnp.float32
BF16 = jnp.bfloat16

EPS = 1e-6
GRID_W = 64
HG_HEADS = 8
HG_DK = 128
ATT_HEADS = 8
ATT_KV_HEADS = 2
ATT_GROUP = ATT_HEADS // ATT_KV_HEADS
ATT_HD = 128
ROPE_THETA = 10000.0
LRU_BLOCKS = 4
LRU_CONV = 4
LRU_C = 8.0
N_BRANCH = 3
N_MOD = 6

V7X_VMEM_BYTES = 64 * 1024 * 1024
V7X_SUBLANES = 8
VMEM_LIMIT = (V7X_VMEM_BYTES * 3) // 4

TM = 256
SEQ_CHUNK = 128
HG_CHUNK = SEQ_CHUNK
HG_HEADS_PER_STEP = 2


def _silu(x):
    return x * jax.nn.sigmoid(x)


def _gelu_tanh(x):
    return 0.5 * x * (1.0 + jnp.tanh(np.sqrt(2.0 / np.pi) * (x + 0.044715 * (x * x * x))))


def _dot(a, b):
    return jnp.dot(a, b, preferred_element_type=F32)


def _dot_nt(a, b):
    return lax.dot_general(a, b, (((1,), (1,)), ((), ())), preferred_element_type=F32)


def _dot_tn(a, b):
    return lax.dot_general(a, b, (((0,), (0,)), ((), ())), preferred_element_type=F32)


def _params(*sem):
    return pltpu.CompilerParams(dimension_semantics=sem, vmem_limit_bytes=VMEM_LIMIT)


def _mod_kernel(c_ref, w_ref, b_ref, o_ref):
    ca = _silu(c_ref[...]).astype(BF16)
    o_ref[...] = _dot(ca, w_ref[...].astype(BF16)) + b_ref[...]


def _mod_vectors(c_all, w_mod, b_mod):
    depth, d, n = w_mod.shape
    rows = c_all.shape[0]
    tn = n // 4
    return pl.pallas_call(
        _mod_kernel,
        grid=(depth, n // tn),
        in_specs=[pl.BlockSpec((rows, d), lambda l, j: (0, 0)),
                  pl.BlockSpec((None, d, tn), lambda l, j: (l, 0, j)),
                  pl.BlockSpec((None, 1, tn), lambda l, j: (l, 0, j))],
        out_specs=pl.BlockSpec((None, rows, tn), lambda l, j: (l, 0, j)),
        out_shape=jax.ShapeDtypeStruct((depth, rows, n), F32),
        compiler_params=_params("parallel", "parallel"),
        name="mod_vectors",
    )(c_all, w_mod, b_mod.reshape(depth, 1, n))


def _lb_kernel(x_ref, o_ref):
    depth = x_ref.shape[1]
    rows = [x_ref[:, l, :] for l in range(depth)]
    m = functools.reduce(jnp.maximum, rows)
    e = [jnp.exp(r - m) for r in rows]
    inv = 1.0 / functools.reduce(lambda a, b: a + b, e)
    acc = jnp.zeros_like(rows[0])
    o_ref[:, 0, :] = acc
    for l in range(1, depth):
        acc = acc + e[l] * inv
        o_ref[:, l, :] = acc


def _lower_bounds(logits):
    return pl.pallas_call(
        _lb_kernel,
        out_shape=jax.ShapeDtypeStruct(logits.shape, F32),
        name="hg_lower_bounds",
    )(logits.astype(F32))


def _norm_mod(x, g, shift, scale):
    ms = jnp.mean(x * x, axis=-1, keepdims=True)
    y = x * lax.rsqrt(ms + EPS) * g
    return y * (1.0 + scale) + shift


def _inproj_hg_kernel(x_ref, g_ref, sh_ref, sc_ref, w_ref, lbf_ref, lbb_ref,
                      qs_ref, v_ref, ff_ref, fb_ref, sg_ref):
    h = _norm_mod(x_ref[...], g_ref[...], sh_ref[...], sc_ref[...]).astype(BF16)
    w = qs_ref.shape[-1]
    z = _dot(h, w_ref[:, 0:w])
    qs_ref[...] = (_silu(z) * (HG_DK ** -0.5)).astype(BF16)
    v_ref[...] = _dot(h, w_ref[:, w:2 * w]).astype(BF16)
    for k, (lb_ref, f_ref) in enumerate(((lbf_ref, ff_ref), (lbb_ref, fb_ref))):
        lb = lb_ref[...]
        z = _dot(h, w_ref[:, (2 + k) * w:(3 + k) * w])
        f_ref[...] = lb + (1.0 - lb) * jax.nn.sigmoid(z)
    sg_ref[...] = _silu(_dot(h, w_ref[:, 4 * w:5 * w])).astype(BF16)


def _inproj_rest_kernel(x_ref, g_ref, sh_ref, sc_ref, w_ref, qg_ref, kg_ref, cos_ref, sin_ref,
                        q_ref, k_ref, v_ref, lx_ref, lg_ref, mg_ref, *, col0):
    h = _norm_mod(x_ref[...], g_ref[...], sh_ref[...], sc_ref[...]).astype(BF16)
    cos = cos_ref[...]
    sin = sin_ref[...]
    lane = lax.broadcasted_iota(jnp.int32, cos.shape, 1)
    quarter = ATT_HD // 4
    first = (lane % (2 * quarter)) < quarter

    def norm_rope(z, gain, scale):
        ms = jnp.mean(z * z, axis=-1, keepdims=True)
        y = z * lax.rsqrt(ms + EPS) * gain
        swapped = jnp.where(first, pltpu.roll(y, ATT_HD - quarter, 1), pltpu.roll(y, quarter, 1))
        return ((y * cos + swapped * sin) * scale).astype(BF16)

    nq = q_ref.shape[-1]
    nkv = k_ref.shape[-1]
    nl = lx_ref.shape[-1]
    off = col0
    z = _dot(h, w_ref[:, off:off + nq])
    for i in range(nq // ATT_HD):
        sl = slice(i * ATT_HD, (i + 1) * ATT_HD)
        q_ref[:, sl] = norm_rope(z[:, sl], qg_ref[...], ATT_HD ** -0.5 * np.log2(np.e))
    off += nq
    z = _dot(h, w_ref[:, off:off + nkv])
    for i in range(nkv // ATT_HD):
        sl = slice(i * ATT_HD, (i + 1) * ATT_HD)
        k_ref[:, sl] = norm_rope(z[:, sl], kg_ref[...], 1.0)
    off += nkv
    v_ref[...] = _dot(h, w_ref[:, off:off + nkv]).astype(BF16)
    off += nkv
    lx_ref[...] = _dot(h, w_ref[:, off:off + nl])
    off += nl
    lg_ref[...] = _gelu_tanh(_dot(h, w_ref[:, off:off + nl])).astype(BF16)
    off += nl
    mg_ref[...] = jax.nn.sigmoid(_dot(h, w_ref[:, off:off + mg_ref.shape[-1]])).astype(BF16)


def _tile_spec(width):
    return pl.BlockSpec((None, TM, width), lambda b, t: (b, t, 0))


def _vec_spec(width):
    return pl.BlockSpec((1, width), lambda b, t: (0, 0))


def _mod_spec(mods, which, n_ctx_tiles, toff=0):
    ctx_row = mods.shape[0] - 1
    d = mods.shape[-1]
    return pl.BlockSpec((None, None, 1, d),
                        lambda b, t: (jnp.where(t + toff < n_ctx_tiles, ctx_row, b), which, 0, 0))


def _layer_weight(w_all, l, cols=None):
    shape = w_all.shape[1:] if cols is None else (*w_all.shape[1:-1], cols)
    return pl.BlockSpec((None, *shape), lambda b, t: (l,) + (0,) * len(shape),
                        pipeline_mode=pl.Buffered(1))


def _inproj(x, mods, g1, w_in, l, lbf, lbb, qg, kg, cos_t, sin_t, n_ctx_tiles):
    bsz, t_all, d = x.shape
    grid = (bsz, t_all // TM)
    hw = HG_HEADS * HG_DK
    nq = ATT_HEADS * ATT_HD
    nkv = ATT_KV_HEADS * ATT_HD
    common = [_tile_spec(d), _vec_spec(d), _mod_spec(mods, 0, n_ctx_tiles),
              _mod_spec(mods, 1, n_ctx_tiles)]
    sds = lambda w, dt: jax.ShapeDtypeStruct((bsz, t_all, w), dt)
    hg = pl.pallas_call(
        _inproj_hg_kernel,
        grid=grid,
        in_specs=common + [_layer_weight(w_in, l, 5 * hw), _vec_spec(hw), _vec_spec(hw)],
        out_specs=[_tile_spec(hw)] * 5,
        out_shape=[sds(hw, BF16), sds(hw, BF16), sds(hw, F32), sds(hw, F32), sds(hw, BF16)],
        compiler_params=_params("parallel", "parallel"),
        name="inproj_hgrn",
    )(x, g1, mods, mods, w_in, lbf, lbb)
    nl = hw
    ng = w_in.shape[-1] - 5 * hw - nq - 2 * nkv - 2 * nl
    rest = pl.pallas_call(
        functools.partial(_inproj_rest_kernel, col0=5 * hw),
        grid=grid,
        in_specs=common + [_layer_weight(w_in, l),
                           _vec_spec(ATT_HD), _vec_spec(ATT_HD),
                           pl.BlockSpec((TM, ATT_HD), lambda b, t: (t, 0)),
                           pl.BlockSpec((TM, ATT_HD), lambda b, t: (t, 0))],
        out_specs=[_tile_spec(nq), _tile_spec(nkv), _tile_spec(nkv), _tile_spec(nl), _tile_spec(nl),
                   _tile_spec(ng)],
        out_shape=[sds(nq, BF16), sds(nkv, BF16), sds(nkv, BF16), sds(nl, F32), sds(nl, BF16),
                   sds(ng, BF16)],
        compiler_params=_params("parallel", "parallel"),
        name="inproj_rest",
    )(x, g1, mods, mods, w_in, qg, kg, cos_t, sin_t)
    return hg, rest


def _chunk_token(r):
    nv = SEQ_CHUNK // V7X_SUBLANES
    return (r % V7X_SUBLANES) * nv + r // V7X_SUBLANES


def _to_chunk_layout(a):
    nv = SEQ_CHUNK // V7X_SUBLANES
    shp = a.shape
    a = a.reshape(shp[0], shp[1] // SEQ_CHUNK, V7X_SUBLANES, nv, *shp[2:])
    return jnp.swapaxes(a, 2, 3).reshape(shp)


def _split3(g):
    g1 = g.astype(BF16)
    r1 = g - g1.astype(F32)
    g2 = r1.astype(BF16)
    g3 = (r1 - g2.astype(F32)).astype(BF16)
    return jnp.concatenate([g1, g2, g3], axis=-1)


def _move_tokens(x, to_chunk):
    rows, d = x.shape
    nv = SEQ_CHUNK // V7X_SUBLANES
    r = lax.broadcasted_iota(jnp.int32, (rows, rows), 0)
    i = r % SEQ_CHUNK
    src = _chunk_token(i) if to_chunk else (i % nv) * V7X_SUBLANES + i // nv
    pick = lax.broadcasted_iota(jnp.int32, (rows, rows), 1) == (r - i) + src
    parts = _dot(jnp.where(pick, 1.0, 0.0).astype(BF16), _split3(x))
    return (parts[:, 0:d] + parts[:, d:2 * d]) + parts[:, 2 * d:3 * d]


def _stream_in_kernel(ctx_ref, x_ref, o_ref, *, n_ctx_tiles):
    @pl.when(pl.program_id(1) < n_ctx_tiles)
    def _():
        o_ref[...] = _move_tokens(ctx_ref[...], True)

    @pl.when(pl.program_id(1) >= n_ctx_tiles)
    def _():
        o_ref[...] = _move_tokens(x_ref[...], True)


def _stream_in(ctx, x):
    bsz, n_ctx, d = ctx.shape
    n_ctx_tiles = n_ctx // TM
    n_tiles = n_ctx_tiles + x.shape[1] // TM
    return pl.pallas_call(
        functools.partial(_stream_in_kernel, n_ctx_tiles=n_ctx_tiles),
        grid=(bsz, n_tiles),
        in_specs=[pl.BlockSpec((None, TM, d), lambda b, t: (b, jnp.minimum(t, n_ctx_tiles - 1), 0)),
                  pl.BlockSpec((None, TM, d), lambda b, t: (b, jnp.maximum(t - n_ctx_tiles, 0), 0))],
        out_specs=pl.BlockSpec((None, TM, d), lambda b, t: (b, t, 0)),
        out_shape=jax.ShapeDtypeStruct((bsz, n_tiles * TM, d), x.dtype),
        compiler_params=_params("parallel", "arbitrary"),
        name="stream_in",
    )(ctx, x)


def _block_row(y, blk, ref_row):
    pos = lax.broadcasted_iota(jnp.int32, (V7X_SUBLANES, 1), 0) % blk
    s = 1
    while ref_row + s < blk:
        y = jnp.where((pos >= ref_row + s) & (pos < ref_row + 2 * s), pltpu.roll(y, s, 0), y)
        s *= 2
    s = 1
    while ref_row - s >= 0:
        y = jnp.where((pos <= ref_row - s) & (pos > ref_row - 2 * s),
                      pltpu.roll(y, V7X_SUBLANES - s, 0), y)
        s *= 2
    return y


def _level_exponent(b, blk, reverse):
    c, width = b.shape
    nv = c // V7X_SUBLANES
    half = blk // 2
    b3 = b.reshape(nv, V7X_SUBLANES, width)
    if blk <= nv:
        ref = half - 1 if reverse else half
        d = []
        for v0 in range(0, nv, blk):
            r = b3[v0 + ref:v0 + ref + 1]
            lo, hi = b3[v0:v0 + half], b3[v0 + half:v0 + blk]
            d += [lo - r, r - hi] if reverse else [r - lo, hi - r]
        d = jnp.concatenate(d, axis=0)
    else:
        sub_blk = blk // nv
        if reverse:
            ref = _block_row(b3[nv - 1], sub_blk, sub_blk // 2 - 1)
        else:
            ref = _block_row(b3[0], sub_blk, sub_blk // 2)
        d = _neg_abs(b3 - ref[None])
    return d.reshape(c, width)


def _neg_abs(x):
    bits = lax.bitcast_convert_type(x, jnp.uint32) | jnp.uint32(0x80000000)
    return lax.bitcast_convert_type(bits, F32)


def _hg_chunks(probs, pair_xor):
    c, width = probs[0]["q"].shape
    heads = [slice(h * HG_DK, (h + 1) * HG_DK) for h in range(width // HG_DK)]
    for p in probs:
        p["kk"] = (1.0 - p["f"]).astype(BF16)
        b = p["b"]
        qe = p["q"] * jnp.exp2(b).astype(BF16)
        p["o"] = [_dot_nt(qe[:, h], s.astype(BF16)) for h, s in zip(heads, p["st"])]
        b_end = b[0:1, :] if p["reverse"] else b[c - 1:c, :]
        ke = p["kk"] * jnp.exp2(b_end - b).astype(BF16)
        st_scale = jnp.exp2(b_end)
        p["st"] = [s * st_scale[:, h] + _dot_tn(p["v"][:, h], ke[:, h]) for h, s in zip(heads, p["st"])]

    blk = c
    while blk >= 4:
        for p in probs:
            e = jnp.exp2(_level_exponent(p["b"], blk, p["reverse"])).astype(BF16)
            qw = p["q"] * e
            kw = p["kk"] * e
            for n, h in enumerate(heads):
                pltpu.store(p["pairs"].at[n], _dot_nt(qw[:, h], kw[:, h]),
                            mask=None if blk == c else pair_xor < blk)
        blk //= 2
    nv = c // V7X_SUBLANES
    vreg = lax.broadcasted_iota(jnp.int32, (nv, 1, 1), 0)
    for p in probs:
        q3, k3, v3 = (p[name].astype(F32).reshape(nv, V7X_SUBLANES, width) for name in ("q", "kk", "v"))
        f3 = p["f"].reshape(nv, V7X_SUBLANES, width)
        if p["reverse"]:
            k_nb, v_nb = (jnp.concatenate([a[1:], a[nv - 1:]], axis=0) for a in (k3, v3))
            w_nb, has_nb = q3 * k_nb * f3, vreg % 2 == 0
        else:
            k_nb, v_nb = (jnp.concatenate([a[:1], a[:nv - 1]], axis=0) for a in (k3, v3))
            w_nb, has_nb = q3 * k_nb * f3, vreg % 2 == 1
        w_self = q3 * k3
        pairs = []
        for n, h in enumerate(heads):
            near = (jnp.sum(w_self[:, :, h], axis=-1, keepdims=True) * v3[:, :, h]
                    + jnp.where(has_nb, jnp.sum(w_nb[:, :, h], axis=-1, keepdims=True), 0.0) * v_nb[:, :, h])
            p["o"][n] = p["o"][n] + near.reshape(c, HG_DK)
            keep = p["causal"] & (pair_xor >= 2)
            pairs.append(jnp.where(keep, p["pairs"][n], 0.0).astype(BF16))
        p["a"] = pairs
    return ([jnp.concatenate(p["o"], axis=-1) for p in probs], [p["a"] for p in probs],
            [p["st"] for p in probs])


def _hg_running_decay(f, tri):
    width = f.shape[-1]
    g = jnp.log2(f)
    g1 = g.astype(BF16)
    g2 = (g - g1.astype(F32)).astype(BF16)
    parts = _dot(tri, jnp.concatenate([g1, g2], axis=-1))
    return parts[:, 0:width] + parts[:, width:2 * width]


def _hgrn_kernel(qs_ref, v_ref, ff_ref, fb_ref, sg_ref, gain_ref, y_ref, of_ref, ob_ref, pair_ref,
                 *, n_ctx):
    t_all, width = qs_ref.shape
    kdim = HG_DK
    c = HG_CHUNK
    n_chunks = t_all // c
    n_ctx_chunks = n_ctx // c
    row = _chunk_token(lax.broadcasted_iota(jnp.int32, (c, c), 0))
    col = _chunk_token(lax.broadcasted_iota(jnp.int32, (c, c), 1))
    pair_xor = row ^ col
    tri_f = jnp.where(col <= row, 1.0, 0.0).astype(BF16)
    tri_b = jnp.where(col >= row, 1.0, 0.0).astype(BF16)

    n_heads = width // kdim
    f_refs = (ff_ref, fb_ref)
    o_refs = (of_ref, ob_ref)
    tris = (tri_f, tri_b)

    def chunk_ids(i):
        i = jnp.minimum(i, n_chunks - 1)
        ib = jnp.where(i < n_ctx_chunks, n_ctx_chunks - 1 - i, n_chunks - 1 - (i - n_ctx_chunks))
        return (i, ib)

    def rows_of(chunk):
        return pl.ds(pl.multiple_of(chunk * c, c), c)

    def finish(chunks, pairs):
        for o_ref, chunk, a in zip(o_refs, chunks, pairs):
            rows = rows_of(chunk)
            add = [_dot(a[h], v_ref[rows, h * kdim:(h + 1) * kdim]) for h in range(n_heads)]
            o_ref[rows, :] += jnp.concatenate(add, axis=-1)

    def body(i, carry):
        sts, decays, prev_chunks, prev_pairs = carry
        finish(prev_chunks, prev_pairs)
        chunks = chunk_ids(i)
        probs = [dict(q=qs_ref[rows_of(ch), :], v=v_ref[rows_of(ch), :], f=f_ref[rows_of(ch), :],
                      b=b, st=st, reverse=d == 1, causal=(col >= row) if d else (col <= row),
                      pairs=pair_ref.at[d])
                 for d, (ch, f_ref, b, st) in enumerate(zip(chunks, f_refs, decays, sts))]
        outs, pairs, sts = _hg_chunks(probs, pair_xor)
        for o_ref, ch, o in zip(o_refs, chunks, outs):
            o_ref[rows_of(ch), :] = o
        decays = [_hg_running_decay(f_ref[rows_of(ch), :], tri)
                  for ch, f_ref, tri in zip(chunk_ids(i + 1), f_refs, tris)]
        return sts, decays, chunks, pairs

    first = chunk_ids(0)
    for o_ref, ch in zip(o_refs, first):
        o_ref[rows_of(ch), :] = jnp.zeros((c, width), F32)
    zero_st = [[jnp.zeros((kdim, kdim), F32)] * n_heads] * 2
    zero_pairs = [[jnp.zeros((c, c), BF16)] * n_heads] * 2
    decays0 = [_hg_running_decay(f_ref[rows_of(ch), :], tri)
               for ch, f_ref, tri in zip(first, f_refs, tris)]
    _, _, last_chunks, last_pairs = lax.fori_loop(
        0, n_chunks, body, (zero_st, decays0, first, zero_pairs))
    finish(last_chunks, last_pairs)

    def readout(i, carry):
        rows = pl.ds(pl.multiple_of(i * c, c), c)
        for h in range(width // kdim):
            sl = slice(h * kdim, (h + 1) * kdim)
            o = of_ref[rows, sl] + ob_ref[rows, sl]
            ms = jnp.mean(o * o, axis=-1, keepdims=True)
            y = o * lax.rsqrt(ms + EPS) * gain_ref[...]
            y_ref[rows, sl] = (y * sg_ref[rows, sl].astype(F32)).astype(BF16)
        return carry
    lax.fori_loop(0, n_chunks, readout, 0, unroll=2)


def _hgrn(qs, v, ff, fb, sg, gain, n_ctx):
    bsz, t_all, width = qs.shape
    bw = HG_HEADS_PER_STEP * HG_DK
    spec = pl.BlockSpec((None, t_all, bw), lambda b, h: (b, 0, h))
    return pl.pallas_call(
        functools.partial(_hgrn_kernel, n_ctx=n_ctx),
        grid=(bsz, width // bw),
        in_specs=[spec] * 5 + [pl.BlockSpec((1, HG_DK), lambda b, h: (0, 0))],
        out_specs=spec,
        out_shape=jax.ShapeDtypeStruct((bsz, t_all, width), BF16),
        scratch_shapes=[pltpu.VMEM((t_all, bw), F32), pltpu.VMEM((t_all, bw), F32),
                        pltpu.VMEM((2, HG_HEADS_PER_STEP, HG_CHUNK, HG_CHUNK), F32)],
        compiler_params=_params("parallel", "parallel"),
        name="hgrn2_mixer",
    )(qs, v, ff, fb, sg, gain)


def _att_kernel(q_ref, k_ref, v_ref, o_ref, *, n_ctx):
    qi = pl.program_id(2)
    n_ctx_tiles = n_ctx // q_ref.shape[0]

    def run(nk):
        k = k_ref[0:nk, :]
        v = v_ref[0:nk, :]
        heads = [slice(g * ATT_HD, (g + 1) * ATT_HD) for g in range(ATT_GROUP)]
        s_next = _dot_nt(q_ref[:, heads[0]], k)
        for g, sl in enumerate(heads):
            s = s_next
            if g + 1 < ATT_GROUP:
                s_next = _dot_nt(q_ref[:, heads[g + 1]], k)
            p = jnp.exp2(s - jnp.max(s, axis=-1, keepdims=True))
            den = jnp.sum(p, axis=-1, keepdims=True)
            o_ref[:, sl] = (_dot(p.astype(BF16), v) / den).astype(BF16)

    @pl.when(qi < n_ctx_tiles)
    def _():
        run(n_ctx)

    @pl.when(qi >= n_ctx_tiles)
    def _():
        run(k_ref.shape[0])


def _attention(q, k, v, n_ctx):
    bsz, t_all, nq = q.shape
    gw = ATT_GROUP * ATT_HD
    return pl.pallas_call(
        functools.partial(_att_kernel, n_ctx=n_ctx),
        grid=(bsz, ATT_KV_HEADS, t_all // TM),
        in_specs=[pl.BlockSpec((None, TM, gw), lambda b, h, i: (b, i, h)),
                  pl.BlockSpec((None, t_all, ATT_HD), lambda b, h, i: (b, 0, h)),
                  pl.BlockSpec((None, t_all, ATT_HD), lambda b, h, i: (b, 0, h))],
        out_specs=pl.BlockSpec((None, TM, gw), lambda b, h, i: (b, i, h)),
        out_shape=jax.ShapeDtypeStruct((bsz, t_all, nq), BF16),
        compiler_params=_params("parallel", "parallel", "arbitrary"),
        name="gqa_mixer",
    )(q, k, v)


def _lru_kernel(x_ref, gl_ref, cw_ref, cb_ref, wa_ref, ba_ref, wx_ref, bx_ref, lam_ref, y_ref,
                af_ref, uf_ref, ab_ref, ub_ref, *, n_ctx):
    t_all, w = x_ref.shape
    c = SEQ_CHUNK
    ns = V7X_SUBLANES
    nv = c // ns
    n_chunks = t_all // c
    n_ctx_chunks = n_ctx // c
    sub = lax.broadcasted_iota(jnp.int32, (ns, 1), 0)

    def rows_of(chunk):
        return pl.ds(pl.multiple_of(chunk * c, c), c)

    def vreg_rows(chunk, v):
        return pl.ds(pl.multiple_of(chunk * c + v * ns, ns), ns)

    a_refs = (af_ref, ab_ref)
    u_refs = (uf_ref, ub_ref)
    rate = [(-0.5 * LRU_C) * jax.nn.softplus(-lam_ref[d:d + 1, :]) for d in range(2)]
    half_ba = [0.5 * ba_ref[d:d + 1, :] for d in range(2)]
    half_bx = [0.5 * bx_ref[d:d + 1, :] for d in range(2)]

    def one_later(cur, prev):
        return pltpu.roll(jnp.where(sub == ns - 1, prev, cur), 1, 0)

    def one_earlier(cur, nxt):
        return pltpu.roll(jnp.where(sub == 0, nxt, cur), ns - 1, 0)

    def gates(i, carry):
        x3 = x_ref[rows_of(i), :].reshape(nv, ns, w)
        has_prev = jnp.logical_and(i > 0, i != n_ctx_chunks)
        has_next = jnp.logical_and(i < n_chunks - 1, i != n_ctx_chunks - 1)
        ip = jnp.maximum(i - 1, 0)
        inx = jnp.minimum(i + 1, n_chunks - 1)
        prev_1 = jnp.where(has_prev, x_ref[vreg_rows(ip, nv - 1), :], 0.0)
        prev_2 = jnp.where(has_prev, x_ref[vreg_rows(ip, nv - 2), :], 0.0)
        next_0 = jnp.where(has_next, x_ref[vreg_rows(inx, 0), :], 0.0)
        ext = jnp.concatenate([one_later(x3[nv - 2], prev_2)[None], one_later(x3[nv - 1], prev_1)[None],
                               x3, one_earlier(x3[0], next_0)[None]], axis=0)
        xc = cb_ref[...] + ext[0:nv] * cw_ref[0:1, :]
        for j in range(1, LRU_CONV):
            xc = xc + ext[j:j + nv] * cw_ref[j:j + 1, :]
        xc = xc.reshape(c, w)
        xcb = xc.astype(BF16)
        half_xc = 0.5 * xc
        for d in range(2):
            th_r = jnp.tanh(_dot(xcb, wa_ref[d]) + half_ba[d])
            th_i = jnp.tanh(_dot(xcb, wx_ref[d]) + half_bx[d])
            log_a = rate[d] * (th_r + 1.0)
            a = jnp.exp(log_a)
            one_minus_a2 = jnp.tanh(log_a) * (-1.0 - a * a)
            a_refs[d][rows_of(i), :] = a
            u_refs[d][rows_of(i), :] = jnp.sqrt(one_minus_a2) * ((th_i + 1.0) * half_xc)
        return carry
    lax.fori_loop(0, n_chunks, gates, 0, unroll=2)

    def chunk_scan(a_ref, u_ref, chunk, h_in, reverse):
        a3 = a_ref[rows_of(chunk), :].reshape(nv, ns, w)
        u3 = u_ref[rows_of(chunk), :].reshape(nv, ns, w)
        order = list(range(nv - 1, -1, -1) if reverse else range(nv))
        h, p = u3[order[0]], a3[order[0]]
        hs, ps = {order[0]: h}, {order[0]: p}
        for v in order[1:]:
            h = a3[v] * h + u3[v]
            p = a3[v] * p
            hs[v], ps[v] = h, p
        s = 1
        while s < ns:
            sh = (ns - s) if reverse else s
            keep = (sub + s < ns) if reverse else (sub >= s)
            h = h + p * jnp.where(keep, pltpu.roll(h, sh, 0), 0.0)
            p = p * jnp.where(keep, pltpu.roll(p, sh, 0), 1.0)
            s *= 2
        end = h + p * h_in
        if reverse:
            enter = jnp.where(sub == ns - 1, h_in, pltpu.roll(end, ns - 1, 0))
            h_out = end[0:1, :]
        else:
            enter = jnp.where(sub == 0, h_in, pltpu.roll(end, 1, 0))
            h_out = end[ns - 1:ns, :]
        full = jnp.concatenate([(hs[v] + ps[v] * enter)[None] for v in range(nv)], axis=0)
        u_ref[rows_of(chunk), :] = full.reshape(c, w)
        return h_out

    def scan_body(i, carry):
        hf, hb = carry
        hf = chunk_scan(af_ref, uf_ref, i, hf, False)
        ib = jnp.where(i < n_ctx_chunks, n_ctx_chunks - 1 - i, n_chunks - 1 - (i - n_ctx_chunks))
        hb = chunk_scan(ab_ref, ub_ref, ib, hb, True)
        return hf, hb
    zero = jnp.zeros((1, w), F32)
    lax.fori_loop(0, n_chunks, scan_body, (zero, zero))

    def out(i, carry):
        rows = rows_of(i)
        y_ref[rows, :] = ((uf_ref[rows, :] + ub_ref[rows, :]) * gl_ref[rows, :].astype(F32)).astype(BF16)
        return carry
    lax.fori_loop(0, n_chunks, out, 0)


def _rglru(lx, lg, conv_w, conv_b, w_a, b_a, w_x, b_x, lam, n_ctx):
    bsz, t_all, width = lx.shape
    bw = width // LRU_BLOCKS
    seq = pl.BlockSpec((None, t_all, bw), lambda b, j: (b, 0, j))
    vec = lambda rows: pl.BlockSpec((rows, bw), lambda b, j: (0, j))
    wspec = pl.BlockSpec((2, None, bw, bw), lambda b, j: (0, j, 0, 0))
    return pl.pallas_call(
        functools.partial(_lru_kernel, n_ctx=n_ctx),
        grid=(bsz, LRU_BLOCKS),
        in_specs=[seq, seq, vec(LRU_CONV), vec(1), wspec, vec(2), wspec, vec(2), vec(2)],
        out_specs=seq,
        out_shape=jax.ShapeDtypeStruct((bsz, t_all, width), BF16),
        scratch_shapes=[pltpu.VMEM((t_all, bw), F32)] * 4,
        compiler_params=_params("parallel", "parallel"),
        name="rglru_mixer",
    )(lx, lg, conv_w, conv_b, w_a, b_a, w_x, b_x, lam)


def _merge_kernel(x_ref, a_ref, b_ref, r_ref, mg_ref, wbr_ref, wo_ref, gate_ref, o_ref):
    d = x_ref.shape[-1]
    y = None
    for i, br in enumerate((a_ref, b_ref, r_ref)):
        term = mg_ref[:, i * d:(i + 1) * d].astype(F32) * _dot(br[...], wbr_ref[i])
        y = term if y is None else y + term
    o_ref[...] = x_ref[...] + gate_ref[...] * _dot(y.astype(BF16), wo_ref[...])


def _ffn_kernel(x_ref, g_ref, sh_ref, sc_ref, gate_ref, wi_ref, wd_ref, *rest, final):
    x = x_ref[...]
    h = _norm_mod(x, g_ref[...], sh_ref[...], sc_ref[...]).astype(BF16)
    hidden = wd_ref.shape[0]
    act = (_silu(_dot(h, wi_ref[:, 0:hidden])) * _dot(h, wi_ref[:, hidden:2 * hidden])).astype(BF16)
    y = x + gate_ref[...] * _dot(act, wd_ref[...])
    if final:
        gf_ref, o_ref = rest
        ms = jnp.mean(y * y, axis=-1, keepdims=True)
        y = _move_tokens(y * lax.rsqrt(ms + EPS) * gf_ref[...], False)
    else:
        (o_ref,) = rest
    o_ref[...] = y


def _merge_ffn(x, ya, yb, yr, mg, mods, w_br, w_o, g2, w_i, w_d, l, n_ctx_tiles, g_final):
    bsz, t_all, d = x.shape
    final = g_final is not None
    toff = n_ctx_tiles if final else 0
    n_t = t_all // TM - toff
    tile = lambda w: pl.BlockSpec((None, TM, w), lambda b, t: (b, t + toff, 0))
    out_tile = pl.BlockSpec((None, TM, d), lambda b, t: (b, t, 0))
    mod = lambda which: _mod_spec(mods, which, n_ctx_tiles, toff)
    full = lambda a: _layer_weight(a, l)
    out_sds = jax.ShapeDtypeStruct((bsz, n_t * TM, d), F32)
    x1 = pl.pallas_call(
        _merge_kernel,
        grid=(bsz, n_t),
        in_specs=[tile(d), tile(d), tile(d), tile(d), tile(N_BRANCH * d), full(w_br), full(w_o), mod(2)],
        out_specs=out_tile,
        out_shape=out_sds,
        compiler_params=_params("parallel", "parallel"),
        name="merge",
    )(x, ya, yb, yr, mg, w_br, w_o, mods)
    x1_tile = pl.BlockSpec((None, TM, d), lambda b, t: (b, t, 0))
    args = [x1, g2, mods, mods, mods, w_i, w_d]
    specs = [x1_tile, _vec_spec(d), mod(3), mod(4), mod(5), full(w_i), full(w_d)]
    if final:
        args.append(g_final)
        specs.append(_vec_spec(d))
    return pl.pallas_call(
        functools.partial(_ffn_kernel, final=final),
        grid=(bsz, n_t),
        in_specs=specs,
        out_specs=out_tile,
        out_shape=out_sds,
        compiler_params=_params("parallel", "parallel"),
        name="ffn",
    )(*args)


def _rope_tables(n_ctx, n_lat):
    rows = n_lat // GRID_W
    r = np.repeat(np.arange(rows), GRID_W).astype(np.float32)
    col = np.tile(np.arange(GRID_W), rows).astype(np.float32)
    half = ATT_HD // 2
    inv = np.power(np.float32(ROPE_THETA), -np.arange(0, half, 2, dtype=np.float32) / np.float32(half))
    ar = jnp.asarray(r[:, None] * inv[None, :], F32)
    ac = jnp.asarray(col[:, None] * inv[None, :], F32)
    cr, sr, cc, sc = jnp.cos(ar), jnp.sin(ar), jnp.cos(ac), jnp.sin(ac)
    cos_l = jnp.concatenate([cr, cr, cc, cc], axis=-1)
    sin_l = jnp.concatenate([-sr, sr, -sc, sc], axis=-1)
    cos_t = jnp.concatenate([jnp.ones((n_ctx, ATT_HD), F32), cos_l], axis=0)
    sin_t = jnp.concatenate([jnp.zeros((n_ctx, ATT_HD), F32), sin_l], axis=0)
    return _to_chunk_layout(cos_t[None])[0], _to_chunk_layout(sin_t[None])[0]


def kernel(x, c, ctx, c_ctx, w_mod, b_mod, norm1, w_in, hg_lb_logits, hg_out_norm, att_q_norm, att_k_norm, lru_conv_w, lru_conv_b, lru_w_a, lru_b_a, lru_w_x, lru_b_x, lru_lambda, w_branch, w_out, norm2, w_ffn_in, w_ffn_out, norm_final):
    bsz, n_lat, d = x.shape
    n_ctx = ctx.shape[1]
    depth = w_in.shape[0]
    assert n_ctx % TM == 0 and n_lat % TM == 0 and n_lat % GRID_W == 0
    n_ctx_tiles = n_ctx // TM
    w_in, w_branch, w_out, w_ffn_in, w_ffn_out = (
        w.astype(BF16) for w in (w_in, w_branch, w_out, w_ffn_in, w_ffn_out))

    c_all = jnp.concatenate([c, c_ctx[None, :]], axis=0)
    mods = _mod_vectors(c_all, w_mod, b_mod).reshape(depth, bsz + 1, N_MOD, 1, d)
    lb = _lower_bounds(hg_lb_logits)
    cos_t, sin_t = _rope_tables(n_ctx, n_lat)

    xs = _stream_in(ctx, x)
    for l in range(depth):
        last = l == depth - 1
        hg, rest = _inproj(xs, mods[l], norm1[l][None, :], w_in, l,
                           lb[0, l][None, :], lb[1, l][None, :],
                           att_q_norm[l][None, :], att_k_norm[l][None, :], cos_t, sin_t, n_ctx_tiles)
        qs, hv, ff, fb, sg = hg
        aq, ak, av, lx, lg, mg = rest
        ya = _hgrn(qs, hv, ff, fb, sg, hg_out_norm[l][None, :], n_ctx)
        yb = _attention(aq, ak, av, n_ctx)
        yr = _rglru(lx, lg, lru_conv_w[l], lru_conv_b[l][None, :], (0.5 * lru_w_a[l]).astype(BF16),
                    lru_b_a[l], (0.5 * lru_w_x[l]).astype(BF16), lru_b_x[l], lru_lambda[l], n_ctx)
        xs = _merge_ffn(xs, ya, yb, yr, mg, mods[l], w_branch, w_out, norm2[l][None, :],
                        w_ffn_in, w_ffn_out, l, n_ctx_tiles, norm_final[None, :] if last else None)
    return xs
```
